```python
import jax
import jax.numpy as jnp
from jax import lax
import numpy as np

D_MODEL = 1024
BATCH = 4
SEQ = 4096
DEPTH = 4

CHUNK = 64
LEFT_CHUNKS = 8
BAND = (LEFT_CHUNKS + 1) * CHUNK
N_AB_LAYERS = (DEPTH + 1) // 2
N_C_LAYERS = DEPTH // 2
A_HEADS = 8
A_HEAD_DIM = 64
A_WIDTH = A_HEADS * A_HEAD_DIM
REL_CLIP = 128
B_HEADS = 8
B_HEAD_SIZE = 64
B_WIDTH = B_HEADS * B_HEAD_SIZE
W_LORA = 64
A_LORA = 64
G_LORA = 128
V_LORA = 32
DECAY_SCALE = 0.606531
GN_EPS = 64e-5
B_COLS = 3 * B_WIDTH + W_LORA + A_LORA + G_LORA
B_SPLITS = (B_WIDTH, 2 * B_WIDTH, 3 * B_WIDTH, 3 * B_WIDTH + W_LORA, 3 * B_WIDTH + W_LORA + A_LORA)
AB_IN_COLS = 3 * A_WIDTH + B_COLS
AB_MIX_WIDTH = A_WIDTH + B_WIDTH
D_RNN = 1280
C_BLOCKS = 10
C_BLOCK_W = D_RNN // C_BLOCKS
CONV_W = 4
RG_C = 8.0
D_FF = 2816
NORM_EPS = 1e-6
NEG_INF = -1e30

kernel_name = 'hybrid_chunked_attn_rwkv7_rglru_macaron'


def rmsnorm(x, g):
    xf = x.astype(jnp.float32)
    y = xf * lax.rsqrt(jnp.mean(xf * xf, axis=-1, keepdims=True) + NORM_EPS)
    return (y * g.astype(jnp.float32)).astype(x.dtype)


def swiglu(h, w_gate, w_up, w_down):
    return (jax.nn.silu(h @ w_gate) * (h @ w_up)) @ w_down


def chunk_attention(q, k, v, rel_bias):
    bsz, seq = q.shape[0], q.shape[1]
    nc = seq // CHUNK
    shp = (bsz, nc, CHUNK, A_HEADS, A_HEAD_DIM)
    qc = q.reshape(shp) * (A_HEAD_DIM ** -0.5)
    pad = ((0, 0), (LEFT_CHUNKS, 0), (0, 0), (0, 0), (0, 0))
    kp = jnp.pad(k.reshape(shp), pad)
    vp = jnp.pad(v.reshape(shp), pad)
    kb = jnp.concatenate([kp[:, j:j + nc] for j in range(LEFT_CHUNKS + 1)], axis=2)
    vb = jnp.concatenate([vp[:, j:j + nc] for j in range(LEFT_CHUNKS + 1)], axis=2)
    s = jnp.einsum('bnqhd,bnkhd->bhnqk', qc, kb).astype(jnp.float32)
    dist = np.arange(CHUNK)[:, None] + LEFT_CHUNKS * CHUNK - np.arange(BAND)[None, :]
    idx = np.clip(dist, -REL_CLIP, REL_CLIP) + REL_CLIP
    s = s + rel_bias[:, idx].astype(jnp.float32)[None, :, None]
    key_chunk = np.arange(nc)[:, None] - LEFT_CHUNKS + (np.arange(BAND) // CHUNK)[None, :]
    s = jnp.where((key_chunk >= 0)[None, None, :, None, :], s, NEG_INF)
    p = jax.nn.softmax(s, axis=-1).astype(v.dtype)
    o = jnp.einsum('bhnqk,bnkhd->bnqhd', p, vb)
    return o.reshape(bsz, seq, A_WIDTH)


def _heads(t):
    return t.reshape(t.shape[0], t.shape[1], B_HEADS, B_HEAD_SIZE)


def rwkv7_scan(r, decay, k, v, kk, kka):
    def step(state, inp):
        r_t, w_t, k_t, v_t, kk_t, kka_t = inp
        sa = jnp.einsum('bhvk,bhk->bhv', state, kk_t)
        state = (state * w_t[:, :, None, :] - sa[..., None] * kka_t[:, :, None, :]
                 + v_t[..., None] * k_t[:, :, None, :])
        return state, jnp.einsum('bhvk,bhk->bhv', state, r_t)
    xs = tuple(jnp.moveaxis(t, 1, 0) for t in (r, decay, k, v, kk, kka))
    s0 = jnp.zeros((r.shape[0], B_HEADS, B_HEAD_SIZE, B_HEAD_SIZE), jnp.float32)
    _, o = lax.scan(step, s0, xs)
    return jnp.moveaxis(o, 0, 1)


def rwkv7_group(p, mu, w0, w_up, a0, a_up, g_up, k_k, k_a, r_k, ln_w, ln_b, v_first, v_mix):
    f32 = jnp.float32
    prev = jnp.pad(p, ((0, 0), (1, 0), (0, 0)))[:, :-1]
    p = p + (prev - p) * mu
    r, k, v, wd, ad, gd = jnp.split(p, B_SPLITS, axis=-1)
    v_raw = v
    if v_mix is not None:
        v0, v_down, v_up = v_mix
        v = v + (v_first - v) * jax.nn.sigmoid(v0 + (v @ v_down) @ v_up)
    decay = jnp.exp(-DECAY_SCALE * jax.nn.sigmoid((w0 + jnp.tanh(wd) @ w_up).astype(f32)))
    a = jax.nn.sigmoid(a0 + ad @ a_up).astype(f32)
    g = jax.nn.sigmoid(gd) @ g_up
    kk = _heads((k * k_k).astype(f32))
    kk = kk * lax.rsqrt(jnp.maximum(jnp.sum(kk * kk, axis=-1, keepdims=True), 1e-24))
    k = k.astype(f32) * (1.0 + (a - 1.0) * k_a.astype(f32))
    rh, kh, vh, ah = _heads(r.astype(f32)), _heads(k), _heads(v.astype(f32)), _heads(a)
    o = rwkv7_scan(rh, _heads(decay), kh, vh, kk, kk * ah)
    mean = jnp.mean(o, axis=-1, keepdims=True)
    var = jnp.mean(jnp.square(o - mean), axis=-1, keepdims=True)
    o = (o - mean) * lax.rsqrt(var + GN_EPS)
    o = (o * ln_w.astype(f32).reshape(B_HEADS, B_HEAD_SIZE)
         + ln_b.astype(f32).reshape(B_HEADS, B_HEAD_SIZE))
    o = o + jnp.sum(rh * kh * r_k.astype(f32), axis=-1, keepdims=True) * vh
    o = o.reshape(o.shape[0], o.shape[1], B_WIDTH) * g.astype(f32)
    return o.astype(p.dtype), v_raw


def _lin_combine(c1, c2):
    a1, b1 = c1
    a2, b2 = c2
    return a1 * a2, a2 * b1 + b2


def rglru_block(h, w_in, conv_w, conv_b, wa, ba, wx, bx, lam, w_out):
    f32 = jnp.float32
    gate, xb = jnp.split(h @ w_in, 2, axis=-1)
    gate = jax.nn.gelu(gate)
    xc = lax.conv_general_dilated(xb, conv_w[:, None, :], window_strides=(1,),
                                  padding=[(CONV_W - 1, 0)],
                                  dimension_numbers=('NWC', 'WIO', 'NWC'),
                                  feature_group_count=D_RNN) + conv_b
    xblk = xc.reshape(xc.shape[0], xc.shape[1], C_BLOCKS, C_BLOCK_W)
    r = jax.nn.sigmoid((jnp.einsum('bsnc,ncd->bsnd', xblk, wa).reshape(xc.shape) + ba).astype(f32))
    ig = jax.nn.sigmoid((jnp.einsum('bsnc,ncd->bsnd', xblk, wx).reshape(xc.shape) + bx).astype(f32))
    log_a = -RG_C * r * jax.nn.softplus(-lam.astype(f32))
    a = jnp.exp(log_a)
    mult = jnp.sqrt(jnp.maximum(-jnp.expm1(2.0 * log_a), 0.0))
    b = mult * ig * xc.astype(f32)
    _, hs = lax.associative_scan(_lin_combine, (a, b), axis=1)
    return (hs.astype(h.dtype) * gate) @ w_out


def setup_inputs(seed: int = 0) -> dict:
    key = jax.random.key(seed)
    ks = iter(jax.random.split(key, 48))
    f32 = jnp.float32

    def nrm(shape, scale):
        return jax.random.normal(next(ks), shape, f32) * scale

    def uni(shape, lo, hi):
        return jax.random.uniform(next(ks), shape, f32, lo, hi)

    na, nc = N_AB_LAYERS, N_C_LAYERS
    u = uni((nc, D_RNN), 0.9, 0.999)
    sig = u ** (1.0 / RG_C)
    lam = jnp.log(sig) - jnp.log1p(-sig)
    return {
        'x': nrm((BATCH, SEQ, D_MODEL), 1.0),
        'norm_ffn': 1.0 + nrm((DEPTH, 2, D_MODEL), 0.02),
        'ffn_w_gate': nrm((DEPTH, 2, D_MODEL, D_FF), D_MODEL ** -0.5),
        'ffn_w_up': nrm((DEPTH, 2, D_MODEL, D_FF), D_MODEL ** -0.5),
        'ffn_w_down': nrm((DEPTH, 2, D_FF, D_MODEL), D_FF ** -0.5),
        'norm_mix': 1.0 + nrm((DEPTH, D_MODEL), 0.02),
        'ab_w_in': nrm((na, D_MODEL, AB_IN_COLS), D_MODEL ** -0.5),
        'ab_w_out': nrm((na, AB_MIX_WIDTH, D_MODEL), AB_MIX_WIDTH ** -0.5),
        'att_rel_bias': nrm((na, A_HEADS, 2 * REL_CLIP + 1), 0.5),
        'rwkv_mu': uni((na, B_COLS), 0.0, 1.0),
        'rwkv_w0': uni((na, B_WIDTH), -5.0, 1.0),
        'rwkv_w_up': nrm((na, W_LORA, B_WIDTH), W_LORA ** -0.5),
        'rwkv_a0': nrm((na, B_WIDTH), 0.1),
        'rwkv_a_up': nrm((na, A_LORA, B_WIDTH), A_LORA ** -0.5),
        'rwkv_g_up': nrm((na, G_LORA, B_WIDTH), G_LORA ** -0.5),
        'rwkv_k_k': 0.85 + nrm((na, B_WIDTH), 0.02),
        'rwkv_k_a': 1.0 + nrm((na, B_WIDTH), 0.02),
        'rwkv_r_k': nrm((na, B_HEADS, B_HEAD_SIZE), 0.1),
        'rwkv_ln_w': 1.0 + nrm((na, B_WIDTH), 0.02),
        'rwkv_ln_b': nrm((na, B_WIDTH), 0.02),
        'rwkv_v0': nrm((na - 1, B_WIDTH), 0.1),
        'rwkv_v_down': nrm((na - 1, B_WIDTH, V_LORA), B_WIDTH ** -0.5),
        'rwkv_v_up': nrm((na - 1, V_LORA, B_WIDTH), V_LORA ** -0.5),
        'c_w_in': nrm((nc, D_MODEL, 2 * D_RNN), D_MODEL ** -0.5),
        'c_conv_w': nrm((nc, CONV_W, D_RNN), CONV_W ** -0.5),
        'c_conv_b': nrm((nc, D_RNN), 0.02),
        'c_wa': nrm((nc, C_BLOCKS, C_BLOCK_W, C_BLOCK_W), C_BLOCK_W ** -0.5),
        'c_ba': nrm((nc, D_RNN), 0.02),
        'c_wx': nrm((nc, C_BLOCKS, C_BLOCK_W, C_BLOCK_W), C_BLOCK_W ** -0.5),
        'c_bx': nrm((nc, D_RNN), 0.02),
        'c_lambda': lam,
        'c_w_out': nrm((nc, D_RNN, D_MODEL), D_RNN ** -0.5),
        'norm_final': 1.0 + nrm((D_MODEL,), 0.02),
    }


def reference(x, norm_ffn, ffn_w_gate, ffn_w_up, ffn_w_down, norm_mix, ab_w_in, ab_w_out,
              att_rel_bias, rwkv_mu, rwkv_w0, rwkv_w_up, rwkv_a0, rwkv_a_up, rwkv_g_up,
              rwkv_k_k, rwkv_k_a, rwkv_r_k, rwkv_ln_w, rwkv_ln_b, rwkv_v0, rwkv_v_down,
              rwkv_v_up, c_w_in, c_conv_w, c_conv_b, c_wa, c_ba, c_wx, c_bx, c_lambda,
              c_w_out, norm_final):
    v_first = None
    for l in range(DEPTH):
        x = x + 0.5 * swiglu(rmsnorm(x, norm_ffn[l, 0]), ffn_w_gate[l, 0], ffn_w_up[l, 0], ffn_w_down[l, 0])
        h = rmsnorm(x, norm_mix[l])
        if l % 2 == 0:
            i = l // 2
            p = h @ ab_w_in[i]
            qa, ka, va, pb = jnp.split(p, [A_WIDTH, 2 * A_WIDTH, 3 * A_WIDTH], axis=-1)
            oa = chunk_attention(qa, ka, va, att_rel_bias[i])
            v_mix = None if i == 0 else (rwkv_v0[i - 1], rwkv_v_down[i - 1], rwkv_v_up[i - 1])
            ob, v_b = rwkv7_group(pb, rwkv_mu[i], rwkv_w0[i], rwkv_w_up[i], rwkv_a0[i], rwkv_a_up[i],
                                  rwkv_g_up[i], rwkv_k_k[i], rwkv_k_a[i], rwkv_r_k[i],
                                  rwkv_ln_w[i], rwkv_ln_b[i], v_first, v_mix)
            if i == 0:
                v_first = v_b
            x = x + jnp.concatenate([oa, ob], axis=-1) @ ab_w_out[i]
        else:
            j = l // 2
            x = x + rglru_block(h, c_w_in[j], c_conv_w[j], c_conv_b[j], c_wa[j], c_ba[j],
                                c_wx[j], c_bx[j], c_lambda[j], c_w_out[j])
        x = x + 0.5 * swiglu(rmsnorm(x, norm_ffn[l, 1]), ffn_w_gate[l, 1], ffn_w_up[l, 1], ffn_w_down[l, 1])
    return rmsnorm(x, norm_final)
```

```python
import functools

import jax
import jax.numpy as jnp
import numpy as np
from jax import lax
from jax.experimental import pallas as pl
from jax.experimental.pallas import tpu as pltpu

F32 = jnp.float32
BF16 = jnp.bfloat16

D_MODEL = 1024
DEPTH = 4
CHUNK = 64
LEFT_CHUNKS = 8
A_HEADS = 8
A_HEAD_DIM = 64
A_WIDTH = A_HEADS * A_HEAD_DIM
REL_CLIP = 128
B_HEADS = 8
B_HEAD_SIZE = 64
B_WIDTH = B_HEADS * B_HEAD_SIZE
W_LORA = 64
A_LORA = 64
G_LORA = 128
V_LORA = 32
DECAY_SCALE = 0.606531
GN_EPS = 64e-5
B_COLS = 3 * B_WIDTH + W_LORA + A_LORA + G_LORA
D_RNN = 1280
C_BLOCKS = 10
C_BLOCK_W = D_RNN // C_BLOCKS
CONV_W = 4
RG_C = 8.0
D_FF = 2816
NORM_EPS = 1e-6
NEG_INF = -1e30

V7X_LANES = 128
V7X_SUBLANES = 8
V7X_MXU_DIM = 256
V7X_VMEM_BYTES = 64 * 1024 * 1024

FFN_ROWS = 512
FFN_CHUNKS = (0, 768, 1536, 2304, 2816)
PROJ_ROWS = 512
ATT_ROWS = 256
ATT_PREV = LEFT_CHUNKS * CHUNK
RWKV_CHUNK = 64
RGLRU_ROWS = 256
VMEM_LIMIT = 56 * 1024 * 1024


def _params(*sem):
    return pltpu.CompilerParams(dimension_semantics=sem, vmem_limit_bytes=VMEM_LIMIT)


def _const_spec(shape):
    nd = len(shape)
    return pl.BlockSpec(shape, lambda *_: (0,) * nd, pipeline_mode=pl.Buffered(1))


def _mm(a, b):
    return jnp.dot(a.astype(BF16), b.astype(BF16), preferred_element_type=F32)


def _mm_nt(a, b):
    return lax.dot_general(a.astype(BF16), b.astype(BF16), (((1,), (1,)), ((), ())),
                           preferred_element_type=F32)


def _mm_tn(a, b):
    return lax.dot_general(a.astype(BF16), b.astype(BF16), (((0,), (0,)), ((), ())),
                           preferred_element_type=F32)


def _rms(x, g):
    ms = jnp.mean(x * x, axis=-1, keepdims=True)
    return x * lax.rsqrt(ms + NORM_EPS) * g


def _sigmoid(x):
    return 1.0 / (1.0 + jnp.exp(-x))


def _ffn_body(x_ref, g_ref, wg_ref, wu_ref, wd_ref, gf_ref, o_ref, *, final_norm):
    x = x_ref[...]
    h = _rms(x, g_ref[...]).astype(BF16)
    acc = None
    for lo, hi in zip(FFN_CHUNKS[:-1], FFN_CHUNKS[1:]):
        gt = jnp.dot(h, wg_ref[:, lo:hi], preferred_element_type=F32)
        up = jnp.dot(h, wu_ref[:, lo:hi], preferred_element_type=F32)
        act = (gt * _sigmoid(gt) * up).astype(BF16)
        d = jnp.dot(act, wd_ref[lo:hi, :], preferred_element_type=F32)
        acc = d if acc is None else acc + d
    y = x + 0.5 * acc
    if final_norm:
        y = _rms(y, gf_ref[...])
    o_ref[...] = y


def _ffn(x, g, wg, wu, wd, gf, final_norm):
    t, d = x.shape
    f = wg.shape[1]
    tm = min(FFN_ROWS, t)
    row = pl.BlockSpec((tm, d), lambda i: (i, 0))
    return pl.pallas_call(
        functools.partial(_ffn_body, final_norm=final_norm),
        out_shape=jax.ShapeDtypeStruct((t, d), F32),
        grid=(t // tm,),
        in_specs=[row, _const_spec((1, d)), _const_spec((d, f)), _const_spec((d, f)),
                  _const_spec((f, d)), _const_spec((1, d))],
        out_specs=row,
        compiler_params=_params("parallel"),
        name="ffn_final" if final_norm else "ffn",
    )(x, g, wg, wu, wd, gf)


def _ab_in_body(x_ref, g_ref, w_ref, qkv_ref, pb_ref):
    h = _rms(x_ref[...], g_ref[...]).astype(BF16)
    na = qkv_ref.shape[-1]
    qkv_ref[...] = jnp.dot(h, w_ref[:, :na], preferred_element_type=F32).astype(BF16)
    pb_ref[...] = jnp.dot(h, w_ref[:, na:], preferred_element_type=F32)


def _ab_in(x, g, w):
    t, d = x.shape
    n = w.shape[1]
    na = 3 * A_WIDTH
    tm = min(PROJ_ROWS, t)
    return pl.pallas_call(
        _ab_in_body,
        out_shape=(jax.ShapeDtypeStruct((t, na), BF16), jax.ShapeDtypeStruct((t, n - na), F32)),
        grid=(t // tm,),
        in_specs=[pl.BlockSpec((tm, d), lambda i: (i, 0)), _const_spec((1, d)), _const_spec((d, n))],
        out_specs=(pl.BlockSpec((tm, na), lambda i: (i, 0)), pl.BlockSpec((tm, n - na), lambda i: (i, 0))),
        compiler_params=_params("parallel"),
        name="ab_in",
    )(x, g, w)


def _ab_out_body(oa_ref, ob_ref, x_ref, w_ref, o_ref):
    acc = jnp.dot(oa_ref[...], w_ref[:A_WIDTH, :], preferred_element_type=F32)
    acc = acc + jnp.dot(ob_ref[...], w_ref[A_WIDTH:, :], preferred_element_type=F32)
    o_ref[...] = x_ref[...] + acc


def _ab_out(oa, ob, x, w):
    t, d = x.shape
    tm = min(PROJ_ROWS, t)
    return pl.pallas_call(
        _ab_out_body,
        out_shape=jax.ShapeDtypeStruct((t, d), F32),
        grid=(t // tm,),
        in_specs=[pl.BlockSpec((tm, A_WIDTH), lambda i: (i, 0)), pl.BlockSpec((tm, B_WIDTH), lambda i: (i, 0)),
                  pl.BlockSpec((tm, d), lambda i: (i, 0)), _const_spec(w.shape)],
        out_specs=pl.BlockSpec((tm, d), lambda i: (i, 0)),
        compiler_params=_params("parallel"),
        name="ab_out",
    )(oa, ob, x, w)


def _attn_body(*refs, n_kblocks):
    q_ref = refs[0]
    k_refs = refs[1:1 + n_kblocks]
    v_refs = refs[1 + n_kblocks:1 + 2 * n_kblocks]
    bias_ref = refs[1 + 2 * n_kblocks]
    o_ref = refs[2 + 2 * n_kblocks]
    qb = q_ref.shape[0]
    i = pl.program_id(1)
    n_prev = n_kblocks - 1
    pens = [jnp.where(i >= n_prev - j, 0.0, NEG_INF).astype(F32) for j in range(n_prev)] + [None]
    lane = lax.broadcasted_iota(jnp.int32, (1, 2 * A_HEAD_DIM), 1)
    first = lane < A_HEAD_DIM
    scale = A_HEAD_DIM ** -0.5
    for pr in range(A_HEADS // 2):
        ls = slice(2 * A_HEAD_DIM * pr, 2 * A_HEAD_DIM * (pr + 1))
        q_pair = q_ref[:, ls]
        ks = [r[:, ls] for r in k_refs]
        vs = [r[:, ls] for r in v_refs]
        outs = []
        for hh in range(2):
            head = 2 * pr + hh
            sel = first if hh == 0 else jnp.logical_not(first)
            qm = jnp.where(sel, scale, 0.0).astype(BF16)
            qh = q_pair * qm
            s = []
            for j in range(n_kblocks):
                sj = lax.dot_general(qh, ks[j], (((1,), (1,)), ((), ())), preferred_element_type=F32)
                sj = sj + bias_ref[head, :, j * qb:(j + 1) * qb]
                if pens[j] is not None:
                    sj = sj + pens[j]
                s.append(sj)
            m = s[0].max(axis=-1, keepdims=True)
            for sj in s[1:]:
                m = jnp.maximum(m, sj.max(axis=-1, keepdims=True))
            l = None
            pv = None
            for j in range(n_kblocks):
                p = jnp.exp(s[j] - m)
                lj = p.sum(axis=-1, keepdims=True)
                pvj = jnp.dot(p.astype(BF16), vs[j], preferred_element_type=F32)
                l = lj if l is None else l + lj
                pv = pvj if pv is None else pv + pvj
            outs.append(pv / l)
        o_ref[:, ls] = jnp.where(first, outs[0], outs[1]).astype(BF16)


def _attn_bias_tile(rel_bias, qb):
    r = np.arange(qb)[:, None]
    c = np.arange(ATT_PREV + qb)[None, :]
    idx = np.clip(r + ATT_PREV - c, -REL_CLIP, REL_CLIP) + REL_CLIP
    band = (c // CHUNK >= r // CHUNK) & (c // CHUNK <= r // CHUNK + LEFT_CHUNKS)
    return jnp.where(band[None], rel_bias[:, idx].astype(F32), NEG_INF)


def _attention(qkv, rel_bias):
    b, s, _ = qkv.shape
    qb = min(ATT_ROWS, s)
    assert ATT_PREV % qb == 0 and s % qb == 0
    n_prev = ATT_PREV // qb
    n_kblocks = n_prev + 1
    bias = _attn_bias_tile(rel_bias, qb)

    def blk(col, back):
        return pl.BlockSpec((None, qb, A_WIDTH), lambda bi, i: (bi, jnp.maximum(i - back, 0), col))

    in_specs = ([blk(0, 0)] + [blk(1, n_prev - j) for j in range(n_kblocks)]
                + [blk(2, n_prev - j) for j in range(n_kblocks)] + [_const_spec(bias.shape)])
    return pl.pallas_call(
        functools.partial(_attn_body, n_kblocks=n_kblocks),
        out_shape=jax.ShapeDtypeStruct((b, s, A_WIDTH), BF16),
        grid=(b, s // qb),
        in_specs=in_specs,
        out_specs=pl.BlockSpec((None, qb, A_WIDTH), lambda bi, i: (bi, i, 0)),
        compiler_params=_params("parallel", "parallel"),
        name="attention",
    )(*([qkv] * (1 + 2 * n_kblocks)), bias)


def _split3(x):
    h1 = x.astype(BF16)
    r1 = x - h1.astype(F32)
    h2 = r1.astype(BF16)
    h3 = (r1 - h2.astype(F32)).astype(BF16)
    return h1, h2, h3


def _rwkv_body(*refs, has_vmix):
    if has_vmix:
        (pb_ref, vf_ref, mu_ref, w0_ref, wup_ref, a0_ref, aup_ref, gup_ref, kk_ref, ka_ref, rk_ref,
         lnw_ref, lnb_ref, v0_ref, vdn_ref, vup_ref, o_ref, state_ref, carry_ref) = refs
        vraw_ref = None
    else:
        (pb_ref, mu_ref, w0_ref, wup_ref, a0_ref, aup_ref, gup_ref, kk_ref, ka_ref, rk_ref,
         lnw_ref, lnb_ref, o_ref, vraw_ref, state_ref, carry_ref) = refs
    c = pb_ref.shape[0]
    n = B_HEAD_SIZE

    @pl.when(pl.program_id(1) == 0)
    def _():
        state_ref[...] = jnp.zeros_like(state_ref)
        carry_ref[...] = jnp.zeros_like(carry_ref)

    p = pb_ref[...]
    row = lax.broadcasted_iota(jnp.int32, (c, 1), 0)
    prev = jnp.where(row == 0, carry_ref[0:1, :], pltpu.roll(p, 1, axis=0))
    carry_ref[0:1, :] = p[c - 1:c, :]
    ps = p + (prev - p) * mu_ref[...]
    r = ps[:, 0:B_WIDTH]
    k = ps[:, B_WIDTH:2 * B_WIDTH]
    v = ps[:, 2 * B_WIDTH:3 * B_WIDTH]
    wad = ps[:, 3 * B_WIDTH:3 * B_WIDTH + W_LORA + A_LORA]
    gd = ps[:, 3 * B_WIDTH + W_LORA + A_LORA:]
    if has_vmix:
        mix = _sigmoid(v0_ref[...] + _mm(_mm(v, vdn_ref[...]), vup_ref[...]))
        v = v + (vf_ref[...] - v) * mix
    else:
        vraw_ref[...] = v
    wlog = -DECAY_SCALE * _sigmoid(w0_ref[...] + _mm(jnp.tanh(wad), wup_ref[...]))
    a = _sigmoid(a0_ref[...] + _mm(wad, aup_ref[...]))
    g = _mm(_sigmoid(gd), gup_ref[...])
    kkf = k * kk_ref[...]
    kmod = k * (1.0 + (a - 1.0) * ka_ref[...])
    rkr = r * kmod * rk_ref[...]

    ti = lax.broadcasted_iota(jnp.int32, (c, 3 * c), 0)
    tj = lax.broadcasted_iota(jnp.int32, (c, 3 * c), 1) % c
    tri3 = jnp.where(tj <= ti, 1.0, 0.0).astype(BF16)
    lcum = jnp.dot(tri3, jnp.concatenate(_split3(wlog), axis=0), preferred_element_type=F32)

    ri = lax.broadcasted_iota(jnp.int32, (c, c), 0)
    ci = lax.broadcasted_iota(jnp.int32, (c, c), 1)
    strict = ci < ri
    incl = ci <= ri

    outs = []
    for h in range(B_HEADS):
        hs = slice(h * n, (h + 1) * n)
        lh = lcum[:, hs]
        wl = wlog[:, hs]
        lend = lh[c - 1:c, :]
        e_neg = jnp.exp(-lh)
        e_pos = jnp.exp(lh)
        e_prev = jnp.exp(lh - wl)
        e_end = jnp.exp(lend - lh)
        kkh = kkf[:, hs]
        kkh = kkh * lax.rsqrt(jnp.maximum(jnp.sum(kkh * kkh, axis=-1, keepdims=True), 1e-24))
        kka = kkh * a[:, hs]
        kh = kmod[:, hs]
        rh = r[:, hs]
        vh = v[:, hs]
        s0 = state_ref[h]
        x = jnp.concatenate([-kkh * e_prev, rh * e_pos], axis=0)
        y = jnp.concatenate([kka * e_neg, kh * e_neg], axis=0)
        gm = _mm_nt(x, y)
        xs = _mm_nt(x, s0)
        m_ab = jnp.where(strict, gm[:c, :c], 0.0)
        m_ak = jnp.where(strict, gm[:c, c:], 0.0)
        m_rb = jnp.where(incl, gm[c:, :c], 0.0)
        m_rk = jnp.where(incl, gm[c:, c:], 0.0)
        u = xs[:c] + _mm(m_ak, vh)
        pw = m_ab
        span = 1
        while True:
            u = u + _mm(pw, u)
            span *= 2
            if span >= c:
                break
            pw = _mm(pw, pw)
        uv = jnp.concatenate([u, vh], axis=0)
        o = xs[c:] + _mm(jnp.concatenate([m_rb, m_rk], axis=1), uv)
        bk_end = jnp.concatenate([kka * e_end, kh * e_end], axis=0)
        state_ref[h] = s0 * jnp.exp(lend) + _mm_tn(uv, bk_end)
        mean = jnp.mean(o, axis=-1, keepdims=True)
        var = jnp.mean(jnp.square(o - mean), axis=-1, keepdims=True)
        o = (o - mean) * lax.rsqrt(var + GN_EPS)
        o = o * lnw_ref[:, hs] + lnb_ref[:, hs]
        o = o + jnp.sum(rkr[:, hs], axis=-1, keepdims=True) * vh
        outs.append(o * g[:, hs])
    o_ref[...] = jnp.concatenate(outs, axis=-1).astype(BF16)


def _rwkv(pb, v_first, prm, vmix):
    b, s, ncol = pb.shape
    c = min(RWKV_CHUNK, s)
    has_vmix = vmix is not None
    tok = lambda w: pl.BlockSpec((None, c, w), lambda bi, i: (bi, i, 0))
    vec = lambda a: a.reshape(1, -1).astype(F32)
    lw = W_LORA + A_LORA
    wup = jnp.zeros((lw, B_WIDTH), F32).at[:W_LORA].set(prm["w_up"]).astype(BF16)
    aup = jnp.zeros((lw, B_WIDTH), F32).at[W_LORA:].set(prm["a_up"]).astype(BF16)
    common = [vec(prm["mu"]), vec(prm["w0"]), wup, vec(prm["a0"]), aup, prm["g_up"].astype(BF16),
              vec(prm["k_k"]), vec(prm["k_a"]), vec(prm["r_k"]), vec(prm["ln_w"]), vec(prm["ln_b"])]
    if has_vmix:
        v0, v_down, v_up = vmix
        vdn = jnp.zeros((B_WIDTH, V7X_LANES), F32).at[:, :V_LORA].set(v_down).astype(BF16)
        vup = jnp.zeros((V7X_LANES, B_WIDTH), F32).at[:V_LORA].set(v_up).astype(BF16)
        args = [pb, v_first] + common + [vec(v0), vdn, vup]
        in_specs = [tok(ncol), tok(B_WIDTH)] + [_const_spec(a.shape) for a in args[2:]]
        out_shape = jax.ShapeDtypeStruct((b, s, B_WIDTH), BF16)
        out_specs = tok(B_WIDTH)
    else:
        args = [pb] + common
        in_specs = [tok(ncol)] + [_const_spec(a.shape) for a in args[1:]]
        out_shape = (jax.ShapeDtypeStruct((b, s, B_WIDTH), BF16), jax.ShapeDtypeStruct((b, s, B_WIDTH), F32))
        out_specs = (tok(B_WIDTH), tok(B_WIDTH))
    return pl.pallas_call(
        functools.partial(_rwkv_body, has_vmix=has_vmix),
        out_shape=out_shape,
        grid=(b, s // c),
        in_specs=in_specs,
        out_specs=out_specs,
        scratch_shapes=[pltpu.VMEM((B_HEADS, B_HEAD_SIZE, B_HEAD_SIZE), F32),
                        pltpu.VMEM((V7X_SUBLANES, ncol), F32)],
        compiler_params=_params("parallel", "arbitrary"),
        name="rwkv_mix" if has_vmix else "rwkv",
    )(*args)


def _gelu_tanh(x):
    return 0.5 * x * (1.0 + jnp.tanh(np.sqrt(2.0 / np.pi).astype(np.float32) * (x + 0.044715 * (x * x * x))))


def _rglru_body(x_ref, g_ref, win_ref, cw_ref, cb_ref, wax_ref, ba_ref, bx_ref, lam_ref, wout_ref,
                o_ref, xprev_ref, hprev_ref):
    tr = x_ref.shape[0]

    @pl.when(pl.program_id(1) == 0)
    def _():
        xprev_ref[...] = jnp.zeros_like(xprev_ref)
        hprev_ref[...] = jnp.zeros_like(hprev_ref)

    x = x_ref[...]
    h = _rms(x, g_ref[...]).astype(BF16)
    gate = _gelu_tanh(jnp.dot(h, win_ref[:, :D_RNN], preferred_element_type=F32))
    xb = jnp.dot(h, win_ref[:, D_RNN:], preferred_element_type=F32)
    ext = jnp.concatenate([xprev_ref[...], xb], axis=0)
    xprev_ref[...] = xb[tr - V7X_SUBLANES:, :]
    xc = xb * cw_ref[CONV_W - 1:CONV_W, :] + cb_ref[...]
    for sft in range(1, CONV_W):
        shifted = pltpu.roll(ext, sft, axis=0)[V7X_SUBLANES:, :]
        xc = xc + shifted * cw_ref[CONV_W - 1 - sft:CONV_W - sft, :]
    xcb = xc.astype(BF16)
    pw = 2 * C_BLOCK_W
    ra, rx = [], []
    for q in range(C_BLOCKS // 2):
        t = jnp.dot(xcb[:, q * pw:(q + 1) * pw], wax_ref[q], preferred_element_type=F32)
        ra.append(t[:, :pw])
        rx.append(t[:, pw:])
    rgate = _sigmoid(jnp.concatenate(ra, axis=-1) + ba_ref[...])
    igate = _sigmoid(jnp.concatenate(rx, axis=-1) + bx_ref[...])
    z = -lam_ref[...]
    softplus = jnp.maximum(z, 0.0) + jnp.log(1.0 + jnp.exp(-jnp.abs(z)))
    log_a = -RG_C * rgate * softplus
    a = jnp.exp(log_a)
    mult = jnp.sqrt(jnp.maximum(1.0 - jnp.exp(2.0 * log_a), 0.0))
    bb = mult * igate * xc
    row = lax.broadcasted_iota(jnp.int32, (tr, 1), 0)
    acum, bcum = a, bb
    sft = 1
    while sft < tr:
        keep = row >= sft
        a_sh = jnp.where(keep, pltpu.roll(acum, sft, axis=0), 1.0)
        b_sh = jnp.where(keep, pltpu.roll(bcum, sft, axis=0), 0.0)
        bcum = acum * b_sh + bcum
        acum = acum * a_sh
        sft *= 2
    hs = acum * hprev_ref[0:1, :] + bcum
    hprev_ref[0:1, :] = hs[tr - 1:tr, :]
    y = (hs * gate).astype(BF16)
    o_ref[...] = x + jnp.dot(y, wout_ref[...], preferred_element_type=F32)


def _rglru(x, g, w_in, conv_w, conv_b, wa, ba, wx, bx, lam, w_out):
    b, s, d = x.shape
    tr = min(RGLRU_ROWS, s)
    pw = 2 * C_BLOCK_W
    wax = jnp.zeros((C_BLOCKS // 2, pw, 2 * pw), F32)
    for half in range(2):
        sl = slice(half * C_BLOCK_W, (half + 1) * C_BLOCK_W)
        wax = wax.at[:, sl, sl].set(wa[half::2])
        wax = wax.at[:, sl, pw + half * C_BLOCK_W:pw + (half + 1) * C_BLOCK_W].set(wx[half::2])
    wax = wax.astype(BF16)
    vec = lambda a: a.reshape(1, -1).astype(F32)
    args = [x, vec(g), w_in, conv_w.astype(F32), vec(conv_b), wax, vec(ba), vec(bx), vec(lam), w_out]
    tok = pl.BlockSpec((None, tr, d), lambda bi, i: (bi, i, 0))
    return pl.pallas_call(
        _rglru_body,
        out_shape=jax.ShapeDtypeStruct((b, s, d), F32),
        grid=(b, s // tr),
        in_specs=[tok] + [_const_spec(a.shape) for a in args[1:]],
        out_specs=tok,
        scratch_shapes=[pltpu.VMEM((V7X_SUBLANES, D_RNN), F32), pltpu.VMEM((V7X_SUBLANES, D_RNN), F32)],
        compiler_params=_params("parallel", "arbitrary"),
        name="rglru",
    )(*args)


def kernel(x, norm_ffn, ffn_w_gate, ffn_w_up, ffn_w_down, norm_mix, ab_w_in, ab_w_out, att_rel_bias, rwkv_mu, rwkv_w0, rwkv_w_up, rwkv_a0, rwkv_a_up, rwkv_g_up, rwkv_k_k, rwkv_k_a, rwkv_r_k, rwkv_ln_w, rwkv_ln_b, rwkv_v0, rwkv_v_down, rwkv_v_up, c_w_in, c_conv_w, c_conv_b, c_wa, c_ba, c_wx, c_bx, c_lambda, c_w_out, norm_final):
    b, s, d = x.shape
    t = b * s
    depth = norm_mix.shape[0]
    wg, wu, wd = ffn_w_gate.astype(BF16), ffn_w_up.astype(BF16), ffn_w_down.astype(BF16)
    w_in, w_out = ab_w_in.astype(BF16), ab_w_out.astype(BF16)
    cw_in, cw_out = c_w_in.astype(BF16), c_w_out.astype(BF16)
    gfin = norm_final.reshape(1, d).astype(F32)
    x = x.astype(F32).reshape(t, d)
    v_first = None
    for l in range(depth):
        x = _ffn(x, norm_ffn[l, 0].reshape(1, d), wg[l, 0], wu[l, 0], wd[l, 0], gfin, False)
        if l % 2 == 0:
            i = l // 2
            qkv, pb = _ab_in(x, norm_mix[l].reshape(1, d), w_in[i])
            oa = _attention(qkv.reshape(b, s, -1), att_rel_bias[i])
            prm = dict(mu=rwkv_mu[i], w0=rwkv_w0[i], w_up=rwkv_w_up[i], a0=rwkv_a0[i], a_up=rwkv_a_up[i],
                       g_up=rwkv_g_up[i], k_k=rwkv_k_k[i], k_a=rwkv_k_a[i], r_k=rwkv_r_k[i],
                       ln_w=rwkv_ln_w[i], ln_b=rwkv_ln_b[i])
            if i == 0:
                ob, v_first = _rwkv(pb.reshape(b, s, -1), None, prm, None)
            else:
                ob = _rwkv(pb.reshape(b, s, -1), v_first, prm,
                           (rwkv_v0[i - 1], rwkv_v_down[i - 1], rwkv_v_up[i - 1]))
            x = _ab_out(oa.reshape(t, -1), ob.reshape(t, -1), x, w_out[i])
        else:
            j = l // 2
            x = _rglru(x.reshape(b, s, d), norm_mix[l], cw_in[j], c_conv_w[j], c_conv_b[j], c_wa[j], c_ba[j],
                       c_wx[j], c_bx[j], c_lambda[j], cw_out[j]).reshape(t, d)
        x = _ffn(x, norm_ffn[l, 1].reshape(1, d), wg[l, 1], wu[l, 1], wd[l, 1], gfin, l == depth - 1)
    return x.reshape(b, s, d)
```

```python
import functools

import jax
import jax.numpy as jnp
import numpy as np
from jax import lax
from jax.experimental import pallas as pl
from jax.experimental.pallas import tpu as pltpu

F32 = jnp.float32
BF16 = jnp.bfloat16

D_MODEL = 1024
DEPTH = 4
CHUNK = 64
LEFT_CHUNKS = 8
A_HEADS = 8
A_HEAD_DIM = 64
A_WIDTH = A_HEADS * A_HEAD_DIM
REL_CLIP = 128
B_HEADS = 8
B_HEAD_SIZE = 64
B_WIDTH = B_HEADS * B_HEAD_SIZE
W_LORA = 64
A_LORA = 64
G_LORA = 128
V_LORA = 32
DECAY_SCALE = 0.606531
GN_EPS = 64e-5
B_COLS = 3 * B_WIDTH + W_LORA + A_LORA + G_LORA
D_RNN = 1280
C_BLOCKS = 10
C_BLOCK_W = D_RNN // C_BLOCKS
CONV_W = 4
RG_C = 8.0
D_FF = 2816
NORM_EPS = 1e-6
NEG_INF = -1e30

V7X_LANES = 128
V7X_SUBLANES = 8
V7X_MXU_DIM = 256
V7X_VMEM_BYTES = 64 * 1024 * 1024

FFN_ROWS = 512
FFN_CHUNKS = (0, 768, 1536, 2304, 2816)
PROJ_ROWS = 512
ATT_ROWS = 256
ATT_PREV = LEFT_CHUNKS * CHUNK
RWKV_CHUNK = 64
RGLRU_ROWS = 256
VMEM_LIMIT = 56 * 1024 * 1024


def _params(*sem):
    return pltpu.CompilerParams(dimension_semantics=sem, vmem_limit_bytes=VMEM_LIMIT)


def _const_spec(shape):
    nd = len(shape)
    return pl.BlockSpec(shape, lambda *_: (0,) * nd, pipeline_mode=pl.Buffered(1))


def _mm(a, b):
    return jnp.dot(a.astype(BF16), b.astype(BF16), preferred_element_type=F32)


def _mm_nt(a, b):
    return lax.dot_general(a.astype(BF16), b.astype(BF16), (((1,), (1,)), ((), ())),
                           preferred_element_type=F32)


def _mm_tn(a, b):
    return lax.dot_general(a.astype(BF16), b.astype(BF16), (((0,), (0,)), ((), ())),
                           preferred_element_type=F32)


def _rms(x, g):
    ms = jnp.mean(x * x, axis=-1, keepdims=True)
    return x * lax.rsqrt(ms + NORM_EPS) * g


def _sigmoid(x):
    return 1.0 / (1.0 + jnp.exp(-x))


def _ffn_body(x_ref, g_ref, wg_ref, wu_ref, wd_ref, gf_ref, o_ref, *, final_norm):
    x = x_ref[...]
    h = _rms(x, g_ref[...]).astype(BF16)
    acc = None
    for lo, hi in zip(FFN_CHUNKS[:-1], FFN_CHUNKS[1:]):
        gt = jnp.dot(h, wg_ref[:, lo:hi], preferred_element_type=F32)
        up = jnp.dot(h, wu_ref[:, lo:hi], preferred_element_type=F32)
        act = (gt * _sigmoid(gt) * up).astype(BF16)
        d = jnp.dot(act, wd_ref[lo:hi, :], preferred_element_type=F32)
        acc = d if acc is None else acc + d
    y = x + 0.5 * acc
    if final_norm:
        y = _rms(y, gf_ref[...])
    o_ref[...] = y


def _ffn(x, g, wg, wu, wd, gf, final_norm):
    t, d = x.shape
    f = wg.shape[1]
    tm = min(FFN_ROWS, t)
    row = pl.BlockSpec((tm, d), lambda i: (i, 0))
    return pl.pallas_call(
        functools.partial(_ffn_body, final_norm=final_norm),
        out_shape=jax.ShapeDtypeStruct((t, d), F32),
        grid=(t // tm,),
        in_specs=[row, _const_spec((1, d)), _const_spec((d, f)), _const_spec((d, f)),
                  _const_spec((f, d)), _const_spec((1, d))],
        out_specs=row,
        compiler_params=_params("parallel"),
        name="ffn_final" if final_norm else "ffn",
    )(x, g, wg, wu, wd, gf)


def _ab_in_body(x_ref, g_ref, w_ref, qkv_ref, pb_ref):
    h = _rms(x_ref[...], g_ref[...]).astype(BF16)
    na = qkv_ref.shape[-1]
    qkv_ref[...] = jnp.dot(h, w_ref[:, :na], preferred_element_type=F32).astype(BF16)
    pb_ref[...] = jnp.dot(h, w_ref[:, na:], preferred_element_type=F32)


def _ab_in(x, g, w):
    t, d = x.shape
    n = w.shape[1]
    na = 3 * A_WIDTH
    tm = min(PROJ_ROWS, t)
    return pl.pallas_call(
        _ab_in_body,
        out_shape=(jax.ShapeDtypeStruct((t, na), BF16), jax.ShapeDtypeStruct((t, n - na), F32)),
        grid=(t // tm,),
        in_specs=[pl.BlockSpec((tm, d), lambda i: (i, 0)), _const_spec((1, d)), _const_spec((d, n))],
        out_specs=(pl.BlockSpec((tm, na), lambda i: (i, 0)), pl.BlockSpec((tm, n - na), lambda i: (i, 0))),
        compiler_params=_params("parallel"),
        name="ab_in",
    )(x, g, w)


def _ab_out_body(oa_ref, ob_ref, x_ref, w_ref, o_ref):
    acc = jnp.dot(oa_ref[...], w_ref[:A_WIDTH, :], preferred_element_type=F32)
    acc = acc + jnp.dot(ob_ref[...], w_ref[A_WIDTH:, :], preferred_element_type=F32)
    o_ref[...] = x_ref[...] + acc


def _ab_out(oa, ob, x, w):
    t, d = x.shape
    tm = min(PROJ_ROWS, t)
    return pl.pallas_call(
        _ab_out_body,
        out_shape=jax.ShapeDtypeStruct((t, d), F32),
        grid=(t // tm,),
        in_specs=[pl.BlockSpec((tm, A_WIDTH), lambda i: (i, 0)), pl.BlockSpec((tm, B_WIDTH), lambda i: (i, 0)),
                  pl.BlockSpec((tm, d), lambda i: (i, 0)), _const_spec(w.shape)],
        out_specs=pl.BlockSpec((tm, d), lambda i: (i, 0)),
        compiler_params=_params("parallel"),
        name="ab_out",
    )(oa, ob, x, w)


def _attn_body(*refs, n_kblocks):
    q_ref = refs[0]
    k_refs = refs[1:1 + n_kblocks]
    v_refs = refs[1 + n_kblocks:1 + 2 * n_kblocks]
    bias_ref = refs[1 + 2 * n_kblocks]
    o_ref = refs[2 + 2 * n_kblocks]
    qb = q_ref.shape[0]
    i = pl.program_id(1)
    n_prev = n_kblocks - 1
    pens = [jnp.where(i >= n_prev - j, 0.0, NEG_INF).astype(F32) for j in range(n_prev)] + [None]
    lane = lax.broadcasted_iota(jnp.int32, (1, 2 * A_HEAD_DIM), 1)
    first = lane < A_HEAD_DIM
    scale = A_HEAD_DIM ** -0.5
    for pr in range(A_HEADS // 2):
        ls = slice(2 * A_HEAD_DIM * pr, 2 * A_HEAD_DIM * (pr + 1))
        q_pair = q_ref[:, ls]
        ks = [r[:, ls] for r in k_refs]
        vs = [r[:, ls] for r in v_refs]
        outs = []
        for hh in range(2):
            head = 2 * pr + hh
            sel = first if hh == 0 else jnp.logical_not(first)
            qm = jnp.where(sel, scale, 0.0).astype(BF16)
            qh = q_pair * qm
            s = []
            for j in range(n_kblocks):
                sj = lax.dot_general(qh, ks[j], (((1,), (1,)), ((), ())), preferred_element_type=F32)
                sj = sj + bias_ref[head, :, j * qb:(j + 1) * qb]
                if pens[j] is not None:
                    sj = sj + pens[j]
                s.append(sj)
            m = s[0].max(axis=-1, keepdims=True)
            for sj in s[1:]:
                m = jnp.maximum(m, sj.max(axis=-1, keepdims=True))
            l = None
            pv = None
            for j in range(n_kblocks):
                p = jnp.exp(s[j] - m)
                lj = p.sum(axis=-1, keepdims=True)
                pvj = jnp.dot(p.astype(BF16), vs[j], preferred_element_type=F32)
                l = lj if l is None else l + lj
                pv = pvj if pv is None else pv + pvj
            outs.append(pv / l)
        o_ref[:, ls] = jnp.where(first, outs[0], outs[1]).astype(BF16)


def _attn_bias_tile(rel_bias, qb):
    nh = rel_bias.shape[0]
    w = ATT_PREV + qb
    period = w + qb
    near = 2 * REL_CLIP + 1
    assert ATT_PREV >= REL_CLIP and qb >= REL_CLIP
    far_past = jnp.broadcast_to(rel_bias[:, near - 1:], (nh, ATT_PREV - REL_CLIP))
    far_future = jnp.broadcast_to(rel_bias[:, :1], (nh, qb - REL_CLIP))
    wrapped = jnp.broadcast_to(rel_bias[:, near - 1:], (nh, qb - 1))
    t = jnp.concatenate([far_past, rel_bias[:, ::-1], far_future, wrapped], axis=1).astype(F32)
    assert t.shape[1] == period
    toe = jnp.broadcast_to(t[:, None, :], (nh, qb, period)).reshape(nh, qb * period)
    toe = toe[:, :qb * (period - 1)].reshape(nh, qb, period - 1)[:, :, :w]
    r = np.arange(qb)[:, None]
    c = np.arange(w)[None, :]
    band = (c // CHUNK >= r // CHUNK) & (c // CHUNK <= r // CHUNK + LEFT_CHUNKS)
    return jnp.where(band[None], toe, NEG_INF)


def _attention(qkv, rel_bias):
    b, s, _ = qkv.shape
    qb = min(ATT_ROWS, s)
    assert ATT_PREV % qb == 0 and s % qb == 0
    n_prev = ATT_PREV // qb
    n_kblocks = n_prev + 1
    bias = _attn_bias_tile(rel_bias, qb)

    def blk(col, back):
        return pl.BlockSpec((None, qb, A_WIDTH), lambda bi, i: (bi, jnp.maximum(i - back, 0), col))

    in_specs = ([blk(0, 0)] + [blk(1, n_prev - j) for j in range(n_kblocks)]
                + [blk(2, n_prev - j) for j in range(n_kblocks)] + [_const_spec(bias.shape)])
    return pl.pallas_call(
        functools.partial(_attn_body, n_kblocks=n_kblocks),
        out_shape=jax.ShapeDtypeStruct((b, s, A_WIDTH), BF16),
        grid=(b, s // qb),
        in_specs=in_specs,
        out_specs=pl.BlockSpec((None, qb, A_WIDTH), lambda bi, i: (bi, i, 0)),
        compiler_params=_params("parallel", "parallel"),
        name="attention",
    )(*([qkv] * (1 + 2 * n_kblocks)), bias)


def _split3(x):
    h1 = x.astype(BF16)
    r1 = x - h1.astype(F32)
    h2 = r1.astype(BF16)
    h3 = (r1 - h2.astype(F32)).astype(BF16)
    return h1, h2, h3


def _rwkv_body(*refs, has_vmix):
    if has_vmix:
        (pb_ref, vf_ref, mu_ref, w0_ref, wup_ref, a0_ref, aup_ref, gup_ref, kk_ref, ka_ref, rk_ref,
         lnw_ref, lnb_ref, v0_ref, vdn_ref, vup_ref, o_ref, state_ref, carry_ref) = refs
        vraw_ref = None
    else:
        (pb_ref, mu_ref, w0_ref, wup_ref, a0_ref, aup_ref, gup_ref, kk_ref, ka_ref, rk_ref,
         lnw_ref, lnb_ref, o_ref, vraw_ref, state_ref, carry_ref) = refs
    nb, c, ncol = pb_ref.shape
    n = B_HEAD_SIZE
    rows = nb * c

    @pl.when(pl.program_id(0) == 0)
    def _():
        state_ref[...] = jnp.zeros_like(state_ref)
        carry_ref[...] = jnp.zeros_like(carry_ref)

    p = pb_ref[...].reshape(rows, ncol)
    row = lax.broadcasted_iota(jnp.int32, (rows, 1), 0)
    prev = pltpu.roll(p, 1, axis=0)
    for b in range(nb):
        prev = jnp.where(row == b * c, carry_ref[b, 0:1, :], prev)
    for b in range(nb):
        carry_ref[b, 0:1, :] = p[(b + 1) * c - 1:(b + 1) * c, :]
    ps = p + (prev - p) * mu_ref[...]
    r = ps[:, 0:B_WIDTH]
    k = ps[:, B_WIDTH:2 * B_WIDTH]
    v = ps[:, 2 * B_WIDTH:3 * B_WIDTH]
    wad = ps[:, 3 * B_WIDTH:3 * B_WIDTH + W_LORA + A_LORA]
    gd = ps[:, 3 * B_WIDTH + W_LORA + A_LORA:]
    if has_vmix:
        mix = _sigmoid(v0_ref[...] + _mm(_mm(v, vdn_ref[...]), vup_ref[...]))
        v = v + (vf_ref[...].reshape(rows, B_WIDTH) - v) * mix
    else:
        vraw_ref[...] = v.reshape(nb, c, B_WIDTH)
    wlog = -DECAY_SCALE * _sigmoid(w0_ref[...] + _mm(jnp.tanh(wad), wup_ref[...]))
    a = _sigmoid(a0_ref[...] + _mm(wad, aup_ref[...]))
    g = _mm(_sigmoid(gd), gup_ref[...])
    kkf = k * kk_ref[...]
    kka = kkf * a
    kmod = k * (1.0 + (a - 1.0) * ka_ref[...])
    rkr = r * kmod * rk_ref[...]

    ti = lax.broadcasted_iota(jnp.int32, (c, 3 * c), 0)
    tj = lax.broadcasted_iota(jnp.int32, (c, 3 * c), 1) % c
    tri3 = jnp.where(tj <= ti, 1.0, 0.0).astype(BF16)
    lcum, lend = [], []
    for b in range(nb):
        lb = jnp.dot(tri3, jnp.concatenate(_split3(wlog[b * c:(b + 1) * c]), axis=0),
                     preferred_element_type=F32)
        lcum.append(lb)
        lend.append(jnp.broadcast_to(lb[c - 1:c, :], lb.shape))
    lcum = jnp.concatenate(lcum, axis=0)
    lend = jnp.concatenate(lend, axis=0)
    e_neg = jnp.exp(-lcum)
    e_end = jnp.exp(lend - lcum)
    w_end = jnp.exp(lend)

    def heads(q):
        return jnp.stack([q[b * c:(b + 1) * c, h * n:(h + 1) * n]
                          for b in range(nb) for h in range(B_HEADS)], axis=0)

    kk_s = heads(kkf)
    rs = lax.rsqrt(jnp.maximum(jnp.sum(kk_s * kk_s, axis=-1, keepdims=True), 1e-24))
    x = jnp.concatenate([heads(-kkf * jnp.exp(lcum - wlog)) * rs, heads(r * jnp.exp(lcum))], axis=1)
    y = jnp.concatenate([heads(kka * e_neg) * rs, heads(kmod * e_neg)], axis=1)
    bk_end = jnp.concatenate([heads(kka * e_end) * rs, heads(kmod * e_end)], axis=1)
    vh = heads(v)
    s0 = state_ref[...]
    xb, yb = x.astype(BF16), y.astype(BF16)
    gm = jnp.einsum("gik,gjk->gij", xb, yb, preferred_element_type=F32)
    xs = jnp.einsum("gik,gvk->giv", xb, s0.astype(BF16), preferred_element_type=F32)
    ri = lax.broadcasted_iota(jnp.int32, (1, c, c), 1)
    ci = lax.broadcasted_iota(jnp.int32, (1, c, c), 2)
    strict = ci < ri
    incl = ci <= ri
    m_ab = jnp.where(strict, gm[:, :c, :c], 0.0)
    m_ak = jnp.where(strict, gm[:, :c, c:], 0.0)
    m_rb = jnp.where(incl, gm[:, c:, :c], 0.0)
    m_rk = jnp.where(incl, gm[:, c:, c:], 0.0)

    def bmm(lhs, rhs):
        return jnp.einsum("gij,gjv->giv", lhs.astype(BF16), rhs.astype(BF16), preferred_element_type=F32)

    u = xs[:, :c] + bmm(m_ak, vh)
    pw = m_ab
    span = 1
    while True:
        u = u + bmm(pw, u)
        span *= 2
        if span >= c:
            break
        pw = bmm(pw, pw)
    uv = jnp.concatenate([u, vh], axis=1)
    o = xs[:, c:] + bmm(jnp.concatenate([m_rb, m_rk], axis=2), uv)
    w_end_s = jnp.stack([w_end[b * c:b * c + 1, h * n:(h + 1) * n]
                         for b in range(nb) for h in range(B_HEADS)], axis=0)
    state_ref[...] = s0 * w_end_s + jnp.einsum("gjv,gjk->gvk", uv.astype(BF16), bk_end.astype(BF16),
                                               preferred_element_type=F32)
    mean = jnp.mean(o, axis=-1, keepdims=True)
    var = jnp.mean(jnp.square(o - mean), axis=-1, keepdims=True)
    o = (o - mean) * lax.rsqrt(var + GN_EPS)
    bonus = jnp.sum(heads(rkr), axis=-1, keepdims=True) * vh

    def unheads(q):
        return jnp.concatenate(
            [jnp.concatenate([q[b * B_HEADS + h] for h in range(B_HEADS)], axis=-1) for b in range(nb)], axis=0)

    out = (unheads(o) * lnw_ref[...] + lnb_ref[...] + unheads(bonus)) * g
    o_ref[...] = out.astype(BF16).reshape(nb, c, B_WIDTH)


def _rwkv(pb, v_first, prm, vmix):
    b, s, ncol = pb.shape
    c = min(RWKV_CHUNK, s)
    has_vmix = vmix is not None
    tok = lambda w: pl.BlockSpec((b, c, w), lambda i: (0, i, 0))
    vec = lambda a: a.reshape(1, -1).astype(F32)
    lw = W_LORA + A_LORA
    wup = jnp.zeros((lw, B_WIDTH), F32).at[:W_LORA].set(prm["w_up"]).astype(BF16)
    aup = jnp.zeros((lw, B_WIDTH), F32).at[W_LORA:].set(prm["a_up"]).astype(BF16)
    common = [vec(prm["mu"]), vec(prm["w0"]), wup, vec(prm["a0"]), aup, prm["g_up"].astype(BF16),
              vec(prm["k_k"]), vec(prm["k_a"]), vec(prm["r_k"]), vec(prm["ln_w"]), vec(prm["ln_b"])]
    if has_vmix:
        v0, v_down, v_up = vmix
        vdn = jnp.zeros((B_WIDTH, V7X_LANES), F32).at[:, :V_LORA].set(v_down).astype(BF16)
        vup = jnp.zeros((V7X_LANES, B_WIDTH), F32).at[:V_LORA].set(v_up).astype(BF16)
        args = [pb, v_first] + common + [vec(v0), vdn, vup]
        in_specs = [tok(ncol), tok(B_WIDTH)] + [_const_spec(a.shape) for a in args[2:]]
        out_shape = jax.ShapeDtypeStruct((b, s, B_WIDTH), BF16)
        out_specs = tok(B_WIDTH)
    else:
        args = [pb] + common
        in_specs = [tok(ncol)] + [_const_spec(a.shape) for a in args[1:]]
        out_shape = (jax.ShapeDtypeStruct((b, s, B_WIDTH), BF16), jax.ShapeDtypeStruct((b, s, B_WIDTH), F32))
        out_specs = (tok(B_WIDTH), tok(B_WIDTH))
    return pl.pallas_call(
        functools.partial(_rwkv_body, has_vmix=has_vmix),
        out_shape=out_shape,
        grid=(s // c,),
        in_specs=in_specs,
        out_specs=out_specs,
        scratch_shapes=[pltpu.VMEM((b * B_HEADS, B_HEAD_SIZE, B_HEAD_SIZE), F32),
                        pltpu.VMEM((b, V7X_SUBLANES, ncol), F32)],
        compiler_params=_params("arbitrary"),
        name="rwkv_mix" if has_vmix else "rwkv",
    )(*args)


def _gelu_tanh(x):
    return 0.5 * x * (1.0 + jnp.tanh(np.sqrt(2.0 / np.pi).astype(np.float32) * (x + 0.044715 * (x * x * x))))


def _rglru_body(x_ref, g_ref, win_ref, cw_ref, cb_ref, wax_ref, ba_ref, bx_ref, lam_ref, wout_ref,
                o_ref, xprev_ref, hprev_ref):
    tr = x_ref.shape[0]

    @pl.when(pl.program_id(1) == 0)
    def _():
        xprev_ref[...] = jnp.zeros_like(xprev_ref)
        hprev_ref[...] = jnp.zeros_like(hprev_ref)

    x = x_ref[...]
    h = _rms(x, g_ref[...]).astype(BF16)
    gate = _gelu_tanh(jnp.dot(h, win_ref[:, :D_RNN], preferred_element_type=F32))
    xb = jnp.dot(h, win_ref[:, D_RNN:], preferred_element_type=F32)
    ext = jnp.concatenate([xprev_ref[...], xb], axis=0)
    xprev_ref[...] = xb[tr - V7X_SUBLANES:, :]
    xc = xb * cw_ref[CONV_W - 1:CONV_W, :] + cb_ref[...]
    for sft in range(1, CONV_W):
        shifted = pltpu.roll(ext, sft, axis=0)[V7X_SUBLANES:, :]
        xc = xc + shifted * cw_ref[CONV_W - 1 - sft:CONV_W - sft, :]
    xcb = xc.astype(BF16)
    pw = 2 * C_BLOCK_W
    ra, rx = [], []
    for q in range(C_BLOCKS // 2):
        t = jnp.dot(xcb[:, q * pw:(q + 1) * pw], wax_ref[q], preferred_element_type=F32)
        ra.append(t[:, :pw])
        rx.append(t[:, pw:])
    rgate = _sigmoid(jnp.concatenate(ra, axis=-1) + ba_ref[...])
    igate = _sigmoid(jnp.concatenate(rx, axis=-1) + bx_ref[...])
    z = -lam_ref[...]
    softplus = jnp.maximum(z, 0.0) + jnp.log(1.0 + jnp.exp(-jnp.abs(z)))
    log_a = -RG_C * rgate * softplus
    a = jnp.exp(log_a)
    mult = jnp.sqrt(jnp.maximum(1.0 - jnp.exp(2.0 * log_a), 0.0))
    bb = mult * igate * xc
    row = lax.broadcasted_iota(jnp.int32, (tr, 1), 0)
    acum, bcum = a, bb
    sft = 1
    while sft < tr:
        keep = row >= sft
        a_sh = jnp.where(keep, pltpu.roll(acum, sft, axis=0), 1.0)
        b_sh = jnp.where(keep, pltpu.roll(bcum, sft, axis=0), 0.0)
        bcum = acum * b_sh + bcum
        acum = acum * a_sh
        sft *= 2
    hs = acum * hprev_ref[0:1, :] + bcum
    hprev_ref[0:1, :] = hs[tr - 1:tr, :]
    y = (hs * gate).astype(BF16)
    o_ref[...] = x + jnp.dot(y, wout_ref[...], preferred_element_type=F32)


def _rglru(x, g, w_in, conv_w, conv_b, wa, ba, wx, bx, lam, w_out):
    b, s, d = x.shape
    tr = min(RGLRU_ROWS, s)
    pw = 2 * C_BLOCK_W
    wax = jnp.zeros((C_BLOCKS // 2, pw, 2 * pw), F32)
    for half in range(2):
        sl = slice(half * C_BLOCK_W, (half + 1) * C_BLOCK_W)
        wax = wax.at[:, sl, sl].set(wa[half::2])
        wax = wax.at[:, sl, pw + half * C_BLOCK_W:pw + (half + 1) * C_BLOCK_W].set(wx[half::2])
    wax = wax.astype(BF16)
    vec = lambda a: a.reshape(1, -1).astype(F32)
    args = [x, vec(g), w_in, conv_w.astype(F32), vec(conv_b), wax, vec(ba), vec(bx), vec(lam), w_out]
    tok = pl.BlockSpec((None, tr, d), lambda bi, i: (bi, i, 0))
    return pl.pallas_call(
        _rglru_body,
        out_shape=jax.ShapeDtypeStruct((b, s, d), F32),
        grid=(b, s // tr),
        in_specs=[tok] + [_const_spec(a.shape) for a in args[1:]],
        out_specs=tok,
        scratch_shapes=[pltpu.VMEM((V7X_SUBLANES, D_RNN), F32), pltpu.VMEM((V7X_SUBLANES, D_RNN), F32)],
        compiler_params=_params("parallel", "arbitrary"),
        name="rglru",
    )(*args)


def kernel(x, norm_ffn, ffn_w_gate, ffn_w_up, ffn_w_down, norm_mix, ab_w_in, ab_w_out, att_rel_bias, rwkv_mu, rwkv_w0, rwkv_w_up, rwkv_a0, rwkv_a_up, rwkv_g_up, rwkv_k_k, rwkv_k_a, rwkv_r_k, rwkv_ln_w, rwkv_ln_b, rwkv_v0, rwkv_v_down, rwkv_v_up, c_w_in, c_conv_w, c_conv_b, c_wa, c_ba, c_wx, c_bx, c_lambda, c_w_out, norm_final):
    b, s, d = x.shape
    t = b * s
    depth = norm_mix.shape[0]
    wg, wu, wd = ffn_w_gate.astype(BF16), ffn_w_up.astype(BF16), ffn_w_down.astype(BF16)
    w_in, w_out = ab_w_in.astype(BF16), ab_w_out.astype(BF16)
    cw_in, cw_out = c_w_in.astype(BF16), c_w_out.astype(BF16)
    gfin = norm_final.reshape(1, d).astype(F32)
    x = x.astype(F32).reshape(t, d)
    v_first = None
    for l in range(depth):
        x = _ffn(x, norm_ffn[l, 0].reshape(1, d), wg[l, 0], wu[l, 0], wd[l, 0], gfin, False)
        if l % 2 == 0:
            i = l // 2
            qkv, pb = _ab_in(x, norm_mix[l].reshape(1, d), w_in[i])
            oa = _attention(qkv.reshape(b, s, -1), att_rel_bias[i])
            prm = dict(mu=rwkv_mu[i], w0=rwkv_w0[i], w_up=rwkv_w_up[i], a0=rwkv_a0[i], a_up=rwkv_a_up[i],
                       g_up=rwkv_g_up[i], k_k=rwkv_k_k[i], k_a=rwkv_k_a[i], r_k=rwkv_r_k[i],
                       ln_w=rwkv_ln_w[i], ln_b=rwkv_ln_b[i])
            if i == 0:
                ob, v_first = _rwkv(pb.reshape(b, s, -1), None, prm, None)
            else:
                ob = _rwkv(pb.reshape(b, s, -1), v_first, prm,
                           (rwkv_v0[i - 1], rwkv_v_down[i - 1], rwkv_v_up[i - 1]))
            x = _ab_out(oa.reshape(t, -1), ob.reshape(t, -1), x, w_out[i])
        else:
            j = l // 2
            x = _rglru(x.reshape(b, s, d), norm_mix[l], cw_in[j], c_conv_w[j], c_conv_b[j], c_wa[j], c_ba[j],
                       c_wx[j], c_bx[j], c_lambda[j], cw_out[j]).reshape(t, d)
        x = _ffn(x, norm_ffn[l, 1].reshape(1, d), wg[l, 1], wu[l, 1], wd[l, 1], gfin, l == depth - 1)
    return x.reshape(b, s, d)
```

```python
import functools

import jax
import jax.numpy as jnp
import numpy as np
from jax import lax
from jax.experimental import pallas as pl
from jax.experimental.pallas import tpu as pltpu

F32 = jnp.float32
BF16 = jnp.bfloat16

D_MODEL = 1024
DEPTH = 4
CHUNK = 64
LEFT_CHUNKS = 8
A_HEADS = 8
A_HEAD_DIM = 64
A_WIDTH = A_HEADS * A_HEAD_DIM
REL_CLIP = 128
B_HEADS = 8
B_HEAD_SIZE = 64
B_WIDTH = B_HEADS * B_HEAD_SIZE
W_LORA = 64
A_LORA = 64
G_LORA = 128
V_LORA = 32
DECAY_SCALE = 0.606531
GN_EPS = 64e-5
B_COLS = 3 * B_WIDTH + W_LORA + A_LORA + G_LORA
D_RNN = 1280
C_BLOCKS = 10
C_BLOCK_W = D_RNN // C_BLOCKS
CONV_W = 4
RG_C = 8.0
D_FF = 2816
NORM_EPS = 1e-6
NEG_INF = -1e30

V7X_LANES = 128
V7X_SUBLANES = 8
V7X_MXU_DIM = 256
V7X_VMEM_BYTES = 64 * 1024 * 1024

FFN_ROWS = 512
FFN_CHUNKS = (0, 768, 1536, 2304, 2816)
PROJ_ROWS = 512
ATT_ROWS = 256
ATT_PREV = LEFT_CHUNKS * CHUNK
RWKV_CHUNK = 64
RGLRU_ROWS = 256
VMEM_LIMIT = 56 * 1024 * 1024


def _params(*sem):
    return pltpu.CompilerParams(dimension_semantics=sem, vmem_limit_bytes=VMEM_LIMIT)


def _const_spec(shape):
    nd = len(shape)
    return pl.BlockSpec(shape, lambda *_: (0,) * nd, pipeline_mode=pl.Buffered(1))


def _pick_spec(shape, lead):
    tail = tuple(shape[len(lead):])
    index = tuple(lead) + (0,) * len(tail)
    return pl.BlockSpec((None,) * len(lead) + tail, lambda *_: index, pipeline_mode=pl.Buffered(1))


def _mm(a, b):
    return jnp.dot(a.astype(BF16), b.astype(BF16), preferred_element_type=F32)


def _mm_nt(a, b):
    return lax.dot_general(a.astype(BF16), b.astype(BF16), (((1,), (1,)), ((), ())),
                           preferred_element_type=F32)


def _mm_tn(a, b):
    return lax.dot_general(a.astype(BF16), b.astype(BF16), (((0,), (0,)), ((), ())),
                           preferred_element_type=F32)


def _rms(x, g):
    ms = jnp.mean(x * x, axis=-1, keepdims=True)
    return x * lax.rsqrt(ms + NORM_EPS) * g


def _sigmoid(x):
    return 0.5 * jnp.tanh(0.5 * x) + 0.5


def _ffn_body(x_ref, g_ref, wg_ref, wu_ref, wd_ref, gf_ref, o_ref, *, final_norm):
    x = x_ref[...]
    h = _rms(x, g_ref[...]).astype(BF16)
    acc = None
    for lo, hi in zip(FFN_CHUNKS[:-1], FFN_CHUNKS[1:]):
        gt = jnp.dot(h, wg_ref[:, lo:hi], preferred_element_type=F32)
        up = jnp.dot(h, wu_ref[:, lo:hi], preferred_element_type=F32)
        act = (gt * _sigmoid(gt) * up).astype(BF16)
        d = jnp.dot(act, wd_ref[lo:hi, :], preferred_element_type=F32)
        acc = d if acc is None else acc + d
    y = x + 0.5 * acc
    if final_norm:
        y = _rms(y, gf_ref[...])
    o_ref[...] = y


def _ffn(x, g, wg, wu, wd, lead, gf, final_norm):
    t, d = x.shape
    tm = min(FFN_ROWS, t)
    row = pl.BlockSpec((tm, d), lambda i: (i, 0))
    return pl.pallas_call(
        functools.partial(_ffn_body, final_norm=final_norm),
        out_shape=jax.ShapeDtypeStruct((t, d), F32),
        grid=(t // tm,),
        in_specs=[row, _const_spec((1, d)), _pick_spec(wg.shape, lead), _pick_spec(wu.shape, lead),
                  _pick_spec(wd.shape, lead), _const_spec((1, d))],
        out_specs=row,
        compiler_params=_params("parallel"),
        name="ffn_final" if final_norm else "ffn",
    )(x, g, wg, wu, wd, gf)


def _ab_in_body(x_ref, g_ref, w_ref, qkv_ref, pb_ref):
    h = _rms(x_ref[...], g_ref[...]).astype(BF16)
    na = qkv_ref.shape[-1]
    qkv_ref[...] = jnp.dot(h, w_ref[:, :na], preferred_element_type=F32).astype(BF16)
    pb_ref[...] = jnp.dot(h, w_ref[:, na:], preferred_element_type=F32)


def _ab_in(x, g, w, lead):
    t, d = x.shape
    n = w.shape[-1]
    na = 3 * A_WIDTH
    tm = min(PROJ_ROWS, t)
    return pl.pallas_call(
        _ab_in_body,
        out_shape=(jax.ShapeDtypeStruct((t, na), BF16), jax.ShapeDtypeStruct((t, n - na), F32)),
        grid=(t // tm,),
        in_specs=[pl.BlockSpec((tm, d), lambda i: (i, 0)), _const_spec((1, d)), _pick_spec(w.shape, lead)],
        out_specs=(pl.BlockSpec((tm, na), lambda i: (i, 0)), pl.BlockSpec((tm, n - na), lambda i: (i, 0))),
        compiler_params=_params("parallel"),
        name="ab_in",
    )(x, g, w)


def _ab_out_body(oa_ref, ob_ref, x_ref, w_ref, o_ref):
    acc = jnp.dot(oa_ref[...], w_ref[:A_WIDTH, :], preferred_element_type=F32)
    acc = acc + jnp.dot(ob_ref[...], w_ref[A_WIDTH:, :], preferred_element_type=F32)
    o_ref[...] = x_ref[...] + acc


def _ab_out(oa, ob, x, w, lead):
    t, d = x.shape
    tm = min(PROJ_ROWS, t)
    return pl.pallas_call(
        _ab_out_body,
        out_shape=jax.ShapeDtypeStruct((t, d), F32),
        grid=(t // tm,),
        in_specs=[pl.BlockSpec((tm, A_WIDTH), lambda i: (i, 0)), pl.BlockSpec((tm, B_WIDTH), lambda i: (i, 0)),
                  pl.BlockSpec((tm, d), lambda i: (i, 0)), _pick_spec(w.shape, lead)],
        out_specs=pl.BlockSpec((tm, d), lambda i: (i, 0)),
        compiler_params=_params("parallel"),
        name="ab_out",
    )(oa, ob, x, w)


def _attn_body(*refs, n_kblocks):
    q_ref = refs[0]
    k_refs = refs[1:1 + n_kblocks]
    v_refs = refs[1 + n_kblocks:1 + 2 * n_kblocks]
    bias_ref = refs[1 + 2 * n_kblocks]
    o_ref = refs[2 + 2 * n_kblocks]
    qb = q_ref.shape[0]
    i = pl.program_id(1)
    n_prev = n_kblocks - 1
    pens = [jnp.where(i >= n_prev - j, 0.0, NEG_INF).astype(F32) for j in range(n_prev)] + [None]
    lane = lax.broadcasted_iota(jnp.int32, (1, 2 * A_HEAD_DIM), 1)
    first = lane < A_HEAD_DIM
    scale = A_HEAD_DIM ** -0.5
    for pr in range(A_HEADS // 2):
        ls = slice(2 * A_HEAD_DIM * pr, 2 * A_HEAD_DIM * (pr + 1))
        q_pair = q_ref[:, ls]
        ks = [r[:, ls] for r in k_refs]
        vs = [r[:, ls] for r in v_refs]
        outs = []
        for hh in range(2):
            head = 2 * pr + hh
            sel = first if hh == 0 else jnp.logical_not(first)
            qm = jnp.where(sel, scale, 0.0).astype(BF16)
            qh = q_pair * qm
            s = []
            for j in range(n_kblocks):
                sj = lax.dot_general(qh, ks[j], (((1,), (1,)), ((), ())), preferred_element_type=F32)
                sj = sj + bias_ref[head, :, j * qb:(j + 1) * qb]
                if pens[j] is not None:
                    sj = sj + pens[j]
                s.append(sj)
            m = s[0].max(axis=-1, keepdims=True)
            for sj in s[1:]:
                m = jnp.maximum(m, sj.max(axis=-1, keepdims=True))
            l = None
            pv = None
            for j in range(n_kblocks):
                p = jnp.exp(s[j] - m)
                lj = p.sum(axis=-1, keepdims=True)
                pvj = jnp.dot(p.astype(BF16), vs[j], preferred_element_type=F32)
                l = lj if l is None else l + lj
                pv = pvj if pv is None else pv + pvj
            outs.append(pv / l)
        o_ref[:, ls] = jnp.where(first, outs[0], outs[1]).astype(BF16)


def _attn_bias_tile(rel_bias, qb):
    nh = rel_bias.shape[0]
    w = ATT_PREV + qb
    period = w + qb
    near = 2 * REL_CLIP + 1
    assert ATT_PREV >= REL_CLIP and qb >= REL_CLIP
    far_past = jnp.broadcast_to(rel_bias[:, near - 1:], (nh, ATT_PREV - REL_CLIP))
    far_future = jnp.broadcast_to(rel_bias[:, :1], (nh, qb - REL_CLIP))
    wrapped = jnp.broadcast_to(rel_bias[:, near - 1:], (nh, qb - 1))
    t = jnp.concatenate([far_past, rel_bias[:, ::-1], far_future, wrapped], axis=1).astype(F32)
    assert t.shape[1] == period
    toe = jnp.broadcast_to(t[:, None, :], (nh, qb, period)).reshape(nh, qb * period)
    toe = toe[:, :qb * (period - 1)].reshape(nh, qb, period - 1)[:, :, :w]
    r = np.arange(qb)[:, None]
    c = np.arange(w)[None, :]
    band = (c // CHUNK >= r // CHUNK) & (c // CHUNK <= r // CHUNK + LEFT_CHUNKS)
    return jnp.where(band[None], toe, NEG_INF)


def _attention(qkv, rel_bias):
    b, s, _ = qkv.shape
    qb = min(ATT_ROWS, s)
    assert ATT_PREV % qb == 0 and s % qb == 0
    n_prev = ATT_PREV // qb
    n_kblocks = n_prev + 1
    bias = _attn_bias_tile(rel_bias, qb)

    def blk(col, back):
        return pl.BlockSpec((None, qb, A_WIDTH), lambda bi, i: (bi, jnp.maximum(i - back, 0), col))

    in_specs = ([blk(0, 0)] + [blk(1, n_prev - j) for j in range(n_kblocks)]
                + [blk(2, n_prev - j) for j in range(n_kblocks)] + [_const_spec(bias.shape)])
    return pl.pallas_call(
        functools.partial(_attn_body, n_kblocks=n_kblocks),
        out_shape=jax.ShapeDtypeStruct((b, s, A_WIDTH), BF16),
        grid=(b, s // qb),
        in_specs=in_specs,
        out_specs=pl.BlockSpec((None, qb, A_WIDTH), lambda bi, i: (bi, i, 0)),
        compiler_params=_params("parallel", "parallel"),
        name="attention",
    )(*([qkv] * (1 + 2 * n_kblocks)), bias)


def _split3(x):
    h1 = x.astype(BF16)
    r1 = x - h1.astype(F32)
    h2 = r1.astype(BF16)
    h3 = (r1 - h2.astype(F32)).astype(BF16)
    return h1, h2, h3


def _rwkv_body(*refs, has_vmix):
    if has_vmix:
        (pb_ref, vf_ref, mu_ref, w0_ref, wup_ref, a0_ref, aup_ref, gup_ref, kk_ref, ka_ref, rk_ref,
         lnw_ref, lnb_ref, v0_ref, vdn_ref, vup_ref, o_ref, state_ref, carry_ref) = refs
        vraw_ref = None
    else:
        (pb_ref, mu_ref, w0_ref, wup_ref, a0_ref, aup_ref, gup_ref, kk_ref, ka_ref, rk_ref,
         lnw_ref, lnb_ref, o_ref, vraw_ref, state_ref, carry_ref) = refs
    nb, c, ncol = pb_ref.shape
    n = B_HEAD_SIZE
    rows = nb * c

    @pl.when(pl.program_id(0) == 0)
    def _():
        state_ref[...] = jnp.zeros_like(state_ref)
        carry_ref[...] = jnp.zeros_like(carry_ref)

    p = pb_ref[...].reshape(rows, ncol)
    row = lax.broadcasted_iota(jnp.int32, (rows, 1), 0)
    prev = pltpu.roll(p, 1, axis=0)
    for b in range(nb):
        prev = jnp.where(row == b * c, carry_ref[b, 0:1, :], prev)
    for b in range(nb):
        carry_ref[b, 0:1, :] = p[(b + 1) * c - 1:(b + 1) * c, :]
    ps = p + (prev - p) * mu_ref[...]
    r = ps[:, 0:B_WIDTH]
    k = ps[:, B_WIDTH:2 * B_WIDTH]
    v = ps[:, 2 * B_WIDTH:3 * B_WIDTH]
    wad = ps[:, 3 * B_WIDTH:3 * B_WIDTH + W_LORA + A_LORA]
    gd = ps[:, 3 * B_WIDTH + W_LORA + A_LORA:]
    if has_vmix:
        mix = _sigmoid(v0_ref[...] + _mm(_mm(v, vdn_ref[...]), vup_ref[...]))
        v = v + (vf_ref[...].reshape(rows, B_WIDTH) - v) * mix
    else:
        vraw_ref[...] = v.reshape(nb, c, B_WIDTH)
    wlog = -DECAY_SCALE * _sigmoid(w0_ref[...] + _mm(jnp.tanh(wad), wup_ref[...]))
    a = _sigmoid(a0_ref[...] + _mm(wad, aup_ref[...]))
    g = _mm(_sigmoid(gd), gup_ref[...])
    kkf = k * kk_ref[...]
    kka = kkf * a
    kmod = k * (1.0 + (a - 1.0) * ka_ref[...])
    rkr = r * kmod * rk_ref[...]

    ti = lax.broadcasted_iota(jnp.int32, (c, 3 * c), 0)
    tj = lax.broadcasted_iota(jnp.int32, (c, 3 * c), 1) % c
    tri3 = jnp.where(tj <= ti, 1.0, 0.0).astype(BF16)
    lcum, lend = [], []
    for b in range(nb):
        lb = jnp.dot(tri3, jnp.concatenate(_split3(wlog[b * c:(b + 1) * c]), axis=0),
                     preferred_element_type=F32)
        lcum.append(lb)
        lend.append(jnp.broadcast_to(lb[c - 1:c, :], lb.shape))
    lcum = jnp.concatenate(lcum, axis=0)
    lend = jnp.concatenate(lend, axis=0)
    e_neg = jnp.exp(-lcum)
    e_end = jnp.exp(lend - lcum)
    w_end = jnp.exp(lend)

    def heads(q):
        return jnp.stack([q[b * c:(b + 1) * c, h * n:(h + 1) * n]
                          for b in range(nb) for h in range(B_HEADS)], axis=0)

    kk_s = heads(kkf)
    rs = lax.rsqrt(jnp.maximum(jnp.sum(kk_s * kk_s, axis=-1, keepdims=True), 1e-24))
    x = jnp.concatenate([heads(-kkf * jnp.exp(lcum - wlog)) * rs, heads(r * jnp.exp(lcum))], axis=1)
    y = jnp.concatenate([heads(kka * e_neg) * rs, heads(kmod * e_neg)], axis=1)
    bk_end = jnp.concatenate([heads(kka * e_end) * rs, heads(kmod * e_end)], axis=1)
    vh = heads(v)
    s0 = state_ref[...]
    xb, yb = x.astype(BF16), y.astype(BF16)
    gm = jnp.einsum("gik,gjk->gij", xb, yb, preferred_element_type=F32)
    xs = jnp.einsum("gik,gvk->giv", xb, s0.astype(BF16), preferred_element_type=F32)
    ri = lax.broadcasted_iota(jnp.int32, (1, c, c), 1)
    ci = lax.broadcasted_iota(jnp.int32, (1, c, c), 2)
    strict = ci < ri
    incl = ci <= ri
    m_ab = jnp.where(strict, gm[:, :c, :c], 0.0)
    m_ak = jnp.where(strict, gm[:, :c, c:], 0.0)
    m_rb = jnp.where(incl, gm[:, c:, :c], 0.0)
    m_rk = jnp.where(incl, gm[:, c:, c:], 0.0)

    def bmm(lhs, rhs):
        return jnp.einsum("gij,gjv->giv", lhs.astype(BF16), rhs.astype(BF16), preferred_element_type=F32)

    u = xs[:, :c] + bmm(m_ak, vh)
    pw = m_ab
    span = 1
    while True:
        u = u + bmm(pw, u)
        span *= 2
        if span >= c:
            break
        pw = bmm(pw, pw)
    uv = jnp.concatenate([u, vh], axis=1)
    o = xs[:, c:] + bmm(jnp.concatenate([m_rb, m_rk], axis=2), uv)
    w_end_s = jnp.stack([w_end[b * c:b * c + 1, h * n:(h + 1) * n]
                         for b in range(nb) for h in range(B_HEADS)], axis=0)
    state_ref[...] = s0 * w_end_s + jnp.einsum("gjv,gjk->gvk", uv.astype(BF16), bk_end.astype(BF16),
                                               preferred_element_type=F32)
    mean = jnp.mean(o, axis=-1, keepdims=True)
    var = jnp.mean(jnp.square(o - mean), axis=-1, keepdims=True)
    o = (o - mean) * lax.rsqrt(var + GN_EPS)
    bonus = jnp.sum(heads(rkr), axis=-1, keepdims=True) * vh

    def unheads(q):
        return jnp.concatenate(
            [jnp.concatenate([q[b * B_HEADS + h] for h in range(B_HEADS)], axis=-1) for b in range(nb)], axis=0)

    out = (unheads(o) * lnw_ref[...] + lnb_ref[...] + unheads(bonus)) * g
    o_ref[...] = out.astype(BF16).reshape(nb, c, B_WIDTH)


def _rwkv(pb, v_first, prm, vmix):
    b, s, ncol = pb.shape
    c = min(RWKV_CHUNK, s)
    has_vmix = vmix is not None
    tok = lambda w: pl.BlockSpec((b, c, w), lambda i: (0, i, 0))
    vec = lambda a: a.reshape(1, -1).astype(F32)
    lw = W_LORA + A_LORA
    wup = jnp.zeros((lw, B_WIDTH), F32).at[:W_LORA].set(prm["w_up"]).astype(BF16)
    aup = jnp.zeros((lw, B_WIDTH), F32).at[W_LORA:].set(prm["a_up"]).astype(BF16)
    common = [vec(prm["mu"]), vec(prm["w0"]), wup, vec(prm["a0"]), aup, prm["g_up"].astype(BF16),
              vec(prm["k_k"]), vec(prm["k_a"]), vec(prm["r_k"]), vec(prm["ln_w"]), vec(prm["ln_b"])]
    if has_vmix:
        v0, v_down, v_up = vmix
        vdn = jnp.zeros((B_WIDTH, V7X_LANES), F32).at[:, :V_LORA].set(v_down).astype(BF16)
        vup = jnp.zeros((V7X_LANES, B_WIDTH), F32).at[:V_LORA].set(v_up).astype(BF16)
        args = [pb, v_first] + common + [vec(v0), vdn, vup]
        in_specs = [tok(ncol), tok(B_WIDTH)] + [_const_spec(a.shape) for a in args[2:]]
        out_shape = jax.ShapeDtypeStruct((b, s, B_WIDTH), BF16)
        out_specs = tok(B_WIDTH)
    else:
        args = [pb] + common
        in_specs = [tok(ncol)] + [_const_spec(a.shape) for a in args[1:]]
        out_shape = (jax.ShapeDtypeStruct((b, s, B_WIDTH), BF16), jax.ShapeDtypeStruct((b, s, B_WIDTH), F32))
        out_specs = (tok(B_WIDTH), tok(B_WIDTH))
    return pl.pallas_call(
        functools.partial(_rwkv_body, has_vmix=has_vmix),
        out_shape=out_shape,
        grid=(s // c,),
        in_specs=in_specs,
        out_specs=out_specs,
        scratch_shapes=[pltpu.VMEM((b * B_HEADS, B_HEAD_SIZE, B_HEAD_SIZE), F32),
                        pltpu.VMEM((b, V7X_SUBLANES, ncol), F32)],
        compiler_params=_params("arbitrary"),
        name="rwkv_mix" if has_vmix else "rwkv",
    )(*args)


def _gelu_tanh(x):
    return 0.5 * x * (1.0 + jnp.tanh(np.sqrt(2.0 / np.pi).astype(np.float32) * (x + 0.044715 * (x * x * x))))


def _rglru_body(x_ref, g_ref, win_ref, cw_ref, cb_ref, wax_ref, ba_ref, bx_ref, lam_ref, wout_ref,
                o_ref, xprev_ref, hprev_ref, ascan_ref, bscan_ref):
    tr = x_ref.shape[0]

    @pl.when(pl.program_id(1) == 0)
    def _():
        xprev_ref[...] = jnp.zeros_like(xprev_ref)
        hprev_ref[...] = jnp.zeros_like(hprev_ref)

    x = x_ref[...]
    h = _rms(x, g_ref[...]).astype(BF16)
    gate = _gelu_tanh(jnp.dot(h, win_ref[:, :D_RNN], preferred_element_type=F32))
    xb = jnp.dot(h, win_ref[:, D_RNN:], preferred_element_type=F32)
    ext = jnp.concatenate([xprev_ref[...], xb], axis=0)
    xprev_ref[...] = xb[tr - V7X_SUBLANES:, :]
    xc = xb * cw_ref[CONV_W - 1:CONV_W, :] + cb_ref[...]
    for sft in range(1, CONV_W):
        shifted = pltpu.roll(ext, sft, axis=0)[V7X_SUBLANES:, :]
        xc = xc + shifted * cw_ref[CONV_W - 1 - sft:CONV_W - sft, :]
    xcb = xc.astype(BF16)
    pw = 2 * C_BLOCK_W
    ra, rx = [], []
    for q in range(C_BLOCKS // 2):
        t = jnp.dot(xcb[:, q * pw:(q + 1) * pw], wax_ref[q], preferred_element_type=F32)
        ra.append(t[:, :pw])
        rx.append(t[:, pw:])
    rgate = _sigmoid(jnp.concatenate(ra, axis=-1) + ba_ref[...])
    igate = _sigmoid(jnp.concatenate(rx, axis=-1) + bx_ref[...])
    z = -lam_ref[...]
    softplus = jnp.maximum(z, 0.0) + jnp.log(1.0 + jnp.exp(-jnp.abs(z)))
    log_a = -RG_C * rgate * softplus
    a = jnp.exp(log_a)
    mult = jnp.sqrt(jnp.maximum(1.0 - a * a, 0.0))
    bb = mult * igate * xc
    sub = V7X_SUBLANES
    ng = tr // sub

    def combine(acum, bcum, pos, n, axis):
        sft = 1
        while sft < n:
            keep = pos >= sft
            a_sh = jnp.where(keep, pltpu.roll(acum, sft, axis=axis), 1.0)
            b_sh = jnp.where(keep, pltpu.roll(bcum, sft, axis=axis), 0.0)
            bcum = acum * b_sh + bcum
            acum = acum * a_sh
            sft *= 2
        return acum, bcum

    pos = lax.broadcasted_iota(jnp.int32, (1, sub, 1), 1)
    acum, bcum = combine(a.reshape(ng, sub, D_RNN), bb.reshape(ng, sub, D_RNN), pos, sub, 1)
    acum = acum.reshape(tr, D_RNN)
    bcum = bcum.reshape(tr, D_RNN)
    ga, gb = [], []
    for j in range(D_RNN // V7X_LANES):
        ls = slice(j * V7X_LANES, (j + 1) * V7X_LANES)
        ascan_ref[j] = acum[:, ls]
        bscan_ref[j] = bcum[:, ls]
        ga.append(ascan_ref[j, pl.ds(sub - 1, ng, stride=sub), :])
        gb.append(bscan_ref[j, pl.ds(sub - 1, ng, stride=sub), :])
    ga = jnp.concatenate(ga, axis=-1)
    gb = jnp.concatenate(gb, axis=-1)
    grow = lax.broadcasted_iota(jnp.int32, (ng, 1), 0)
    ga, gb = combine(ga, gb, grow, ng, 0)
    hprev = hprev_ref[0:1, :]
    hend = ga * hprev + gb
    hin = jnp.where(grow == 0, hprev, pltpu.roll(hend, 1, axis=0))
    hprev_ref[0:1, :] = hend[ng - 1:ng, :]
    hin = jnp.broadcast_to(hin[:, None, :], (ng, sub, D_RNN)).reshape(tr, D_RNN)
    hs = acum * hin + bcum
    y = (hs * gate).astype(BF16)
    o_ref[...] = x + jnp.dot(y, wout_ref[...], preferred_element_type=F32)


def _rglru(x, g, w_in, conv_w, conv_b, wa, ba, wx, bx, lam, w_out, lead):
    b, s, d = x.shape
    tr = min(RGLRU_ROWS, s)
    pw = 2 * C_BLOCK_W
    wax = jnp.zeros((C_BLOCKS // 2, pw, 2 * pw), F32)
    for half in range(2):
        sl = slice(half * C_BLOCK_W, (half + 1) * C_BLOCK_W)
        wax = wax.at[:, sl, sl].set(wa[half::2])
        wax = wax.at[:, sl, pw + half * C_BLOCK_W:pw + (half + 1) * C_BLOCK_W].set(wx[half::2])
    wax = wax.astype(BF16)
    vec = lambda a: a.reshape(1, -1).astype(F32)
    args = [x, vec(g), w_in, conv_w.astype(F32), vec(conv_b), wax, vec(ba), vec(bx), vec(lam), w_out]
    tok = pl.BlockSpec((None, tr, d), lambda bi, i: (bi, i, 0))
    return pl.pallas_call(
        _rglru_body,
        out_shape=jax.ShapeDtypeStruct((b, s, d), F32),
        grid=(b, s // tr),
        in_specs=([tok, _const_spec(args[1].shape), _pick_spec(w_in.shape, lead)]
                  + [_const_spec(a.shape) for a in args[3:-1]] + [_pick_spec(w_out.shape, lead)]),
        out_specs=tok,
        scratch_shapes=[pltpu.VMEM((V7X_SUBLANES, D_RNN), F32), pltpu.VMEM((V7X_SUBLANES, D_RNN), F32),
                        pltpu.VMEM((D_RNN // V7X_LANES, tr, V7X_LANES), F32),
                        pltpu.VMEM((D_RNN // V7X_LANES, tr, V7X_LANES), F32)],
        compiler_params=_params("parallel", "arbitrary"),
        name="rglru",
    )(*args)


def kernel(x, norm_ffn, ffn_w_gate, ffn_w_up, ffn_w_down, norm_mix, ab_w_in, ab_w_out, att_rel_bias, rwkv_mu, rwkv_w0, rwkv_w_up, rwkv_a0, rwkv_a_up, rwkv_g_up, rwkv_k_k, rwkv_k_a, rwkv_r_k, rwkv_ln_w, rwkv_ln_b, rwkv_v0, rwkv_v_down, rwkv_v_up, c_w_in, c_conv_w, c_conv_b, c_wa, c_ba, c_wx, c_bx, c_lambda, c_w_out, norm_final):
    b, s, d = x.shape
    t = b * s
    depth = norm_mix.shape[0]
    wg, wu, wd = ffn_w_gate.astype(BF16), ffn_w_up.astype(BF16), ffn_w_down.astype(BF16)
    w_in, w_out = ab_w_in.astype(BF16), ab_w_out.astype(BF16)
    cw_in, cw_out = c_w_in.astype(BF16), c_w_out.astype(BF16)
    gfin = norm_final.reshape(1, d).astype(F32)
    x = x.astype(F32).reshape(t, d)
    v_first = None
    for l in range(depth):
        x = _ffn(x, norm_ffn[l, 0].reshape(1, d), wg, wu, wd, (l, 0), gfin, False)
        if l % 2 == 0:
            i = l // 2
            qkv, pb = _ab_in(x, norm_mix[l].reshape(1, d), w_in, (i,))
            oa = _attention(qkv.reshape(b, s, -1), att_rel_bias[i])
            prm = dict(mu=rwkv_mu[i], w0=rwkv_w0[i], w_up=rwkv_w_up[i], a0=rwkv_a0[i], a_up=rwkv_a_up[i],
                       g_up=rwkv_g_up[i], k_k=rwkv_k_k[i], k_a=rwkv_k_a[i], r_k=rwkv_r_k[i],
                       ln_w=rwkv_ln_w[i], ln_b=rwkv_ln_b[i])
            if i == 0:
                ob, v_first = _rwkv(pb.reshape(b, s, -1), None, prm, None)
            else:
                ob = _rwkv(pb.reshape(b, s, -1), v_first, prm,
                           (rwkv_v0[i - 1], rwkv_v_down[i - 1], rwkv_v_up[i - 1]))
            x = _ab_out(oa.reshape(t, -1), ob.reshape(t, -1), x, w_out, (i,))
        else:
            j = l // 2
            x = _rglru(x.reshape(b, s, d), norm_mix[l], cw_in, c_conv_w[j], c_conv_b[j], c_wa[j], c_ba[j],
                       c_wx[j], c_bx[j], c_lambda[j], cw_out, (j,)).reshape(t, d)
        x = _ffn(x, norm_ffn[l, 1].reshape(1, d), wg, wu, wd, (l, 1), gfin, l == depth - 1)
    return x.reshape(b, s, d)
```

```python
import functools

import jax
import jax.numpy as jnp
import numpy as np
from jax import lax
from jax.experimental import pallas as pl
from jax.experimental.pallas import tpu as pltpu

F32 = jnp.float32
BF16 = jnp.bfloat16

D_MODEL = 1024
DEPTH = 4
CHUNK = 64
LEFT_CHUNKS = 8
A_HEADS = 8
A_HEAD_DIM = 64
A_WIDTH = A_HEADS * A_HEAD_DIM
REL_CLIP = 128
B_HEADS = 8
B_HEAD_SIZE = 64
B_WIDTH = B_HEADS * B_HEAD_SIZE
W_LORA = 64
A_LORA = 64
G_LORA = 128
V_LORA = 32
DECAY_SCALE = 0.606531
GN_EPS = 64e-5
B_COLS = 3 * B_WIDTH + W_LORA + A_LORA + G_LORA
D_RNN = 1280
C_BLOCKS = 10
C_BLOCK_W = D_RNN // C_BLOCKS
CONV_W = 4
RG_C = 8.0
D_FF = 2816
NORM_EPS = 1e-6
NEG_INF = -1e30

V7X_LANES = 128
V7X_SUBLANES = 8
V7X_MXU_DIM = 256
V7X_VMEM_BYTES = 64 * 1024 * 1024

FFN_ROWS = 512
FFN_CHUNKS = (0, 768, 1536, 2304, 2816)
PROJ_ROWS = 512
ATT_ROWS = 256
ATT_PREV = LEFT_CHUNKS * CHUNK
RWKV_CHUNK = 64
RGLRU_ROWS = 256
VMEM_LIMIT = 56 * 1024 * 1024


def _params(*sem):
    return pltpu.CompilerParams(dimension_semantics=sem, vmem_limit_bytes=VMEM_LIMIT)


def _const_spec(shape):
    nd = len(shape)
    return pl.BlockSpec(shape, lambda *_: (0,) * nd, pipeline_mode=pl.Buffered(1))


def _pick_spec(shape, lead):
    tail = tuple(shape[len(lead):])
    index = tuple(lead) + (0,) * len(tail)
    return pl.BlockSpec((None,) * len(lead) + tail, lambda *_: index, pipeline_mode=pl.Buffered(1))


def _mm(a, b):
    return jnp.dot(a.astype(BF16), b.astype(BF16), preferred_element_type=F32)


def _mm_nt(a, b):
    return lax.dot_general(a.astype(BF16), b.astype(BF16), (((1,), (1,)), ((), ())),
                           preferred_element_type=F32)


def _mm_tn(a, b):
    return lax.dot_general(a.astype(BF16), b.astype(BF16), (((0,), (0,)), ((), ())),
                           preferred_element_type=F32)


def _rms(x, g):
    ms = jnp.mean(x * x, axis=-1, keepdims=True)
    return x * lax.rsqrt(ms + NORM_EPS) * g


def _sigmoid(x):
    return 0.5 * jnp.tanh(0.5 * x) + 0.5


def _ffn_body(x_ref, g_ref, wg_ref, wu_ref, wd_ref, gf_ref, o_ref, *, final_norm):
    x = x_ref[...]
    h = _rms(x, g_ref[...]).astype(BF16)
    acc = None
    for lo, hi in zip(FFN_CHUNKS[:-1], FFN_CHUNKS[1:]):
        gt = jnp.dot(h, wg_ref[:, lo:hi], preferred_element_type=F32)
        up = jnp.dot(h, wu_ref[:, lo:hi], preferred_element_type=F32)
        act = (gt * _sigmoid(gt) * up).astype(BF16)
        d = jnp.dot(act, wd_ref[lo:hi, :], preferred_element_type=F32)
        acc = d if acc is None else acc + d
    y = x + 0.5 * acc
    if final_norm:
        y = _rms(y, gf_ref[...])
    o_ref[...] = y


def _ffn(x, g, wg, wu, wd, lead, gf, final_norm):
    t, d = x.shape
    tm = min(FFN_ROWS, t)
    row = pl.BlockSpec((tm, d), lambda i: (i, 0))
    return pl.pallas_call(
        functools.partial(_ffn_body, final_norm=final_norm),
        out_shape=jax.ShapeDtypeStruct((t, d), F32),
        grid=(t // tm,),
        in_specs=[row, _const_spec((1, d)), _pick_spec(wg.shape, lead), _pick_spec(wu.shape, lead),
                  _pick_spec(wd.shape, lead), _const_spec((1, d))],
        out_specs=row,
        compiler_params=_params("parallel"),
        name="ffn_final" if final_norm else "ffn",
    )(x, g, wg, wu, wd, gf)


def _ab_in_body(x_ref, g_ref, w_ref, qkv_ref, pb_ref):
    h = _rms(x_ref[...], g_ref[...]).astype(BF16)
    na = qkv_ref.shape[-1]
    qkv_ref[...] = jnp.dot(h, w_ref[:, :na], preferred_element_type=F32).astype(BF16)
    pb_ref[...] = jnp.dot(h, w_ref[:, na:], preferred_element_type=F32)


def _ab_in(x, g, w, lead):
    t, d = x.shape
    n = w.shape[-1]
    na = 3 * A_WIDTH
    tm = min(PROJ_ROWS, t)
    return pl.pallas_call(
        _ab_in_body,
        out_shape=(jax.ShapeDtypeStruct((t, na), BF16), jax.ShapeDtypeStruct((t, n - na), F32)),
        grid=(t // tm,),
        in_specs=[pl.BlockSpec((tm, d), lambda i: (i, 0)), _const_spec((1, d)), _pick_spec(w.shape, lead)],
        out_specs=(pl.BlockSpec((tm, na), lambda i: (i, 0)), pl.BlockSpec((tm, n - na), lambda i: (i, 0))),
        compiler_params=_params("parallel"),
        name="ab_in",
    )(x, g, w)


def _ab_out_body(oa_ref, ob_ref, x_ref, w_ref, o_ref):
    acc = jnp.dot(oa_ref[...], w_ref[:A_WIDTH, :], preferred_element_type=F32)
    acc = acc + jnp.dot(ob_ref[...], w_ref[A_WIDTH:, :], preferred_element_type=F32)
    o_ref[...] = x_ref[...] + acc


def _ab_out(oa, ob, x, w, lead):
    t, d = x.shape
    tm = min(PROJ_ROWS, t)
    return pl.pallas_call(
        _ab_out_body,
        out_shape=jax.ShapeDtypeStruct((t, d), F32),
        grid=(t // tm,),
        in_specs=[pl.BlockSpec((tm, A_WIDTH), lambda i: (i, 0)), pl.BlockSpec((tm, B_WIDTH), lambda i: (i, 0)),
                  pl.BlockSpec((tm, d), lambda i: (i, 0)), _pick_spec(w.shape, lead)],
        out_specs=pl.BlockSpec((tm, d), lambda i: (i, 0)),
        compiler_params=_params("parallel"),
        name="ab_out",
    )(oa, ob, x, w)


def _attn_body(*refs, n_kblocks):
    q_ref = refs[0]
    k_refs = refs[1:1 + n_kblocks]
    v_refs = refs[1 + n_kblocks:1 + 2 * n_kblocks]
    bias_ref = refs[1 + 2 * n_kblocks]
    o_ref = refs[2 + 2 * n_kblocks]
    qb = q_ref.shape[0]
    i = pl.program_id(1)
    n_prev = n_kblocks - 1
    pens = [jnp.where(i >= n_prev - j, 0.0, NEG_INF).astype(F32) for j in range(n_prev)] + [None]
    lane = lax.broadcasted_iota(jnp.int32, (1, 2 * A_HEAD_DIM), 1)
    first = lane < A_HEAD_DIM
    scale = A_HEAD_DIM ** -0.5
    for pr in range(A_HEADS // 2):
        ls = slice(2 * A_HEAD_DIM * pr, 2 * A_HEAD_DIM * (pr + 1))
        q_pair = q_ref[:, ls]
        ks = [r[:, ls] for r in k_refs]
        vs = [r[:, ls] for r in v_refs]
        outs = []
        for hh in range(2):
            head = 2 * pr + hh
            sel = first if hh == 0 else jnp.logical_not(first)
            qm = jnp.where(sel, scale, 0.0).astype(BF16)
            qh = q_pair * qm
            s = []
            for j in range(n_kblocks):
                sj = lax.dot_general(qh, ks[j], (((1,), (1,)), ((), ())), preferred_element_type=F32)
                sj = sj + bias_ref[head, :, j * qb:(j + 1) * qb]
                if pens[j] is not None:
                    sj = sj + pens[j]
                s.append(sj)
            m = s[0].max(axis=-1, keepdims=True)
            for sj in s[1:]:
                m = jnp.maximum(m, sj.max(axis=-1, keepdims=True))
            l = None
            pv = None
            for j in range(n_kblocks):
                p = jnp.exp(s[j] - m)
                lj = p.sum(axis=-1, keepdims=True)
                pvj = jnp.dot(p.astype(BF16), vs[j], preferred_element_type=F32)
                l = lj if l is None else l + lj
                pv = pvj if pv is None else pv + pvj
            outs.append(pv / l)
        o_ref[:, ls] = jnp.where(first, outs[0], outs[1]).astype(BF16)


def _attn_bias_tile(rel_bias, qb):
    nh = rel_bias.shape[0]
    w = ATT_PREV + qb
    period = w + qb
    near = 2 * REL_CLIP + 1
    assert ATT_PREV >= REL_CLIP and qb >= REL_CLIP
    far_past = jnp.broadcast_to(rel_bias[:, near - 1:], (nh, ATT_PREV - REL_CLIP))
    far_future = jnp.broadcast_to(rel_bias[:, :1], (nh, qb - REL_CLIP))
    wrapped = jnp.broadcast_to(rel_bias[:, near - 1:], (nh, qb - 1))
    t = jnp.concatenate([far_past, rel_bias[:, ::-1], far_future, wrapped], axis=1).astype(F32)
    assert t.shape[1] == period
    toe = jnp.broadcast_to(t[:, None, :], (nh, qb, period)).reshape(nh, qb * period)
    toe = toe[:, :qb * (period - 1)].reshape(nh, qb, period - 1)[:, :, :w]
    r = np.arange(qb)[:, None]
    c = np.arange(w)[None, :]
    band = (c // CHUNK >= r // CHUNK) & (c // CHUNK <= r // CHUNK + LEFT_CHUNKS)
    return jnp.where(band[None], toe, NEG_INF)


def _attention(qkv, rel_bias):
    b, s, _ = qkv.shape
    qb = min(ATT_ROWS, s)
    assert ATT_PREV % qb == 0 and s % qb == 0
    n_prev = ATT_PREV // qb
    n_kblocks = n_prev + 1
    bias = _attn_bias_tile(rel_bias, qb)

    def blk(col, back):
        return pl.BlockSpec((None, qb, A_WIDTH), lambda bi, i: (bi, jnp.maximum(i - back, 0), col))

    in_specs = ([blk(0, 0)] + [blk(1, n_prev - j) for j in range(n_kblocks)]
                + [blk(2, n_prev - j) for j in range(n_kblocks)] + [_const_spec(bias.shape)])
    return pl.pallas_call(
        functools.partial(_attn_body, n_kblocks=n_kblocks),
        out_shape=jax.ShapeDtypeStruct((b, s, A_WIDTH), BF16),
        grid=(b, s // qb),
        in_specs=in_specs,
        out_specs=pl.BlockSpec((None, qb, A_WIDTH), lambda bi, i: (bi, i, 0)),
        compiler_params=_params("parallel", "parallel"),
        name="attention",
    )(*([qkv] * (1 + 2 * n_kblocks)), bias)


def _split3(x):
    h1 = x.astype(BF16)
    r1 = x - h1.astype(F32)
    h2 = r1.astype(BF16)
    h3 = (r1 - h2.astype(F32)).astype(BF16)
    return h1, h2, h3


def _rwkv_body(*refs, has_vmix):
    if has_vmix:
        (pb_ref, vf_ref, mu_ref, w0_ref, wup_ref, a0_ref, aup_ref, gup_ref, kk_ref, ka_ref, rk_ref,
         lnw_ref, lnb_ref, v0_ref, vdn_ref, vup_ref, o_ref, state_ref, carry_ref) = refs
        vraw_ref = None
    else:
        (pb_ref, mu_ref, w0_ref, wup_ref, a0_ref, aup_ref, gup_ref, kk_ref, ka_ref, rk_ref,
         lnw_ref, lnb_ref, o_ref, vraw_ref, state_ref, carry_ref) = refs
    nb, c, ncol = pb_ref.shape
    n = B_HEAD_SIZE
    rows = nb * c

    @pl.when(pl.program_id(0) == 0)
    def _():
        state_ref[...] = jnp.zeros_like(state_ref)
        carry_ref[...] = jnp.zeros_like(carry_ref)

    p = pb_ref[...].reshape(rows, ncol)
    row = lax.broadcasted_iota(jnp.int32, (rows, 1), 0)
    prev = pltpu.roll(p, 1, axis=0)
    for b in range(nb):
        prev = jnp.where(row == b * c, carry_ref[b, 0:1, :], prev)
    for b in range(nb):
        carry_ref[b, 0:1, :] = p[(b + 1) * c - 1:(b + 1) * c, :]
    ps = p + (prev - p) * mu_ref[...]
    r = ps[:, 0:B_WIDTH]
    k = ps[:, B_WIDTH:2 * B_WIDTH]
    v = ps[:, 2 * B_WIDTH:3 * B_WIDTH]
    wad = ps[:, 3 * B_WIDTH:3 * B_WIDTH + W_LORA + A_LORA]
    gd = ps[:, 3 * B_WIDTH + W_LORA + A_LORA:]
    if has_vmix:
        mix = _sigmoid(v0_ref[...] + _mm(_mm(v, vdn_ref[...]), vup_ref[...]))
        v = v + (vf_ref[...].reshape(rows, B_WIDTH) - v) * mix
    else:
        vraw_ref[...] = v.reshape(nb, c, B_WIDTH)
    wlog = -DECAY_SCALE * _sigmoid(w0_ref[...] + _mm(jnp.tanh(wad), wup_ref[...]))
    a = _sigmoid(a0_ref[...] + _mm(wad, aup_ref[...]))
    g = _mm(_sigmoid(gd), gup_ref[...])
    kkf = k * kk_ref[...]
    kka = kkf * a
    kmod = k * (1.0 + (a - 1.0) * ka_ref[...])
    rkr = r * kmod * rk_ref[...]

    ti = lax.broadcasted_iota(jnp.int32, (c, 3 * c), 0)
    tj = lax.broadcasted_iota(jnp.int32, (c, 3 * c), 1) % c
    tri3 = jnp.where(tj <= ti, 1.0, 0.0).astype(BF16)
    lcum, lend = [], []
    for b in range(nb):
        lb = jnp.dot(tri3, jnp.concatenate(_split3(wlog[b * c:(b + 1) * c]), axis=0),
                     preferred_element_type=F32)
        lcum.append(lb)
        lend.append(jnp.broadcast_to(lb[c - 1:c, :], lb.shape))
    lcum = jnp.concatenate(lcum, axis=0)
    lend = jnp.concatenate(lend, axis=0)
    e_neg = jnp.exp(-lcum)
    e_end = jnp.exp(lend - lcum)
    w_end = jnp.exp(lend)

    pair = 2 * n
    npair = B_HEADS // 2
    lo = lax.broadcasted_iota(jnp.int32, (1, 1, pair), 2) < n

    def slabs(q):
        return jnp.stack([q[b * c:(b + 1) * c, p * pair:(p + 1) * pair]
                          for b in range(nb) for p in range(npair)], axis=0)

    def unslabs(q):
        return jnp.concatenate(
            [jnp.concatenate([q[b * npair + p] for p in range(npair)], axis=-1) for b in range(nb)], axis=0)

    def head_sum(q):
        s_lo = jnp.sum(jnp.where(lo, q, 0.0), axis=-1, keepdims=True)
        s_hi = jnp.sum(jnp.where(lo, 0.0, q), axis=-1, keepdims=True)
        return jnp.where(lo, s_lo, s_hi)

    def bd(q):
        return jnp.concatenate([jnp.where(lo, q, 0.0), jnp.where(lo, 0.0, q)], axis=1)

    def bmm(lhs, rhs):
        return jnp.einsum("gij,gjv->giv", lhs.astype(BF16), rhs.astype(BF16), preferred_element_type=F32)

    kk_s = slabs(kkf)
    rs = lax.rsqrt(jnp.maximum(head_sum(kk_s * kk_s), 1e-24))
    x = jnp.concatenate([slabs(-kkf * jnp.exp(lcum - wlog)) * rs, slabs(r * jnp.exp(lcum))], axis=1)
    bt = slabs(kka * e_neg) * rs
    kt = slabs(kmod * e_neg)
    bk_end = jnp.concatenate([slabs(kka * e_end) * rs, slabs(kmod * e_end)], axis=1)
    vh = slabs(v)
    s0 = state_ref[...]
    xb = x.astype(BF16)
    ybd = jnp.concatenate([bd(bt), bd(kt)], axis=1).astype(BF16)
    gm = jnp.einsum("gik,gjk->gij", xb, ybd, preferred_element_type=F32)
    xs = jnp.einsum("gik,gvk->giv", xb, s0.astype(BF16), preferred_element_type=F32)
    ri = lax.broadcasted_iota(jnp.int32, (1, c, 2 * c), 1)
    ci = lax.broadcasted_iota(jnp.int32, (1, c, 2 * c), 2) % c
    strict = ci < ri
    incl = ci <= ri
    m_ab = jnp.where(strict, gm[:, :c, :2 * c], 0.0)
    m_ak = jnp.where(strict, gm[:, :c, 2 * c:], 0.0)
    m_rb = jnp.where(incl, gm[:, c:, :2 * c], 0.0)
    m_rk = jnp.where(incl, gm[:, c:, 2 * c:], 0.0)
    bdv = bd(vh)
    u = xs[:, :c] + bmm(m_ak, bdv)
    pw = m_ab
    span = 1
    while True:
        u = u + bmm(pw, bd(u))
        span *= 2
        if span >= c:
            break
        pw = bmm(pw, bd(pw))
    o = xs[:, c:] + bmm(jnp.concatenate([m_rb, m_rk], axis=2), jnp.concatenate([bd(u), bdv], axis=1))
    w_end_s = jnp.stack([w_end[b * c:b * c + 1, p * pair:(p + 1) * pair]
                         for b in range(nb) for p in range(npair)], axis=0)
    uv = jnp.concatenate([u, vh], axis=1)
    upd = jnp.einsum("gjv,gjk->gvk", uv.astype(BF16), bk_end.astype(BF16), preferred_element_type=F32)
    same_head = (lax.broadcasted_iota(jnp.int32, (1, pair, pair), 1) < n) == lo
    state_ref[...] = s0 * w_end_s + jnp.where(same_head, upd, 0.0)
    mean = head_sum(o) * (1.0 / n)
    d = o - mean
    var = head_sum(d * d) * (1.0 / n)
    o = d * lax.rsqrt(var + GN_EPS)
    bonus = head_sum(slabs(rkr)) * vh
    out = (unslabs(o) * lnw_ref[...] + lnb_ref[...] + unslabs(bonus)) * g
    o_ref[...] = out.astype(BF16).reshape(nb, c, B_WIDTH)


def _rwkv(pb, v_first, prm, vmix):
    b, s, ncol = pb.shape
    c = min(RWKV_CHUNK, s)
    has_vmix = vmix is not None
    tok = lambda w: pl.BlockSpec((b, c, w), lambda i: (0, i, 0))
    vec = lambda a: a.reshape(1, -1).astype(F32)
    lw = W_LORA + A_LORA
    wup = jnp.zeros((lw, B_WIDTH), F32).at[:W_LORA].set(prm["w_up"]).astype(BF16)
    aup = jnp.zeros((lw, B_WIDTH), F32).at[W_LORA:].set(prm["a_up"]).astype(BF16)
    common = [vec(prm["mu"]), vec(prm["w0"]), wup, vec(prm["a0"]), aup, prm["g_up"].astype(BF16),
              vec(prm["k_k"]), vec(prm["k_a"]), vec(prm["r_k"]), vec(prm["ln_w"]), vec(prm["ln_b"])]
    if has_vmix:
        v0, v_down, v_up = vmix
        vdn = jnp.zeros((B_WIDTH, V7X_LANES), F32).at[:, :V_LORA].set(v_down).astype(BF16)
        vup = jnp.zeros((V7X_LANES, B_WIDTH), F32).at[:V_LORA].set(v_up).astype(BF16)
        args = [pb, v_first] + common + [vec(v0), vdn, vup]
        in_specs = [tok(ncol), tok(B_WIDTH)] + [_const_spec(a.shape) for a in args[2:]]
        out_shape = jax.ShapeDtypeStruct((b, s, B_WIDTH), BF16)
        out_specs = tok(B_WIDTH)
    else:
        args = [pb] + common
        in_specs = [tok(ncol)] + [_const_spec(a.shape) for a in args[1:]]
        out_shape = (jax.ShapeDtypeStruct((b, s, B_WIDTH), BF16), jax.ShapeDtypeStruct((b, s, B_WIDTH), F32))
        out_specs = (tok(B_WIDTH), tok(B_WIDTH))
    return pl.pallas_call(
        functools.partial(_rwkv_body, has_vmix=has_vmix),
        out_shape=out_shape,
        grid=(s // c,),
        in_specs=in_specs,
        out_specs=out_specs,
        scratch_shapes=[pltpu.VMEM((b * B_HEADS // 2, 2 * B_HEAD_SIZE, 2 * B_HEAD_SIZE), F32),
                        pltpu.VMEM((b, V7X_SUBLANES, ncol), F32)],
        compiler_params=_params("arbitrary"),
        name="rwkv_mix" if has_vmix else "rwkv",
    )(*args)


def _gelu_tanh(x):
    return 0.5 * x * (1.0 + jnp.tanh(np.sqrt(2.0 / np.pi).astype(np.float32) * (x + 0.044715 * (x * x * x))))


def _rglru_body(x_ref, g_ref, win_ref, cw_ref, cb_ref, wax_ref, ba_ref, bx_ref, lam_ref, wout_ref,
                o_ref, xprev_ref, hprev_ref, ascan_ref, bscan_ref):
    tr = x_ref.shape[0]

    @pl.when(pl.program_id(1) == 0)
    def _():
        xprev_ref[...] = jnp.zeros_like(xprev_ref)
        hprev_ref[...] = jnp.zeros_like(hprev_ref)

    x = x_ref[...]
    h = _rms(x, g_ref[...]).astype(BF16)
    gate = _gelu_tanh(jnp.dot(h, win_ref[:, :D_RNN], preferred_element_type=F32))
    xb = jnp.dot(h, win_ref[:, D_RNN:], preferred_element_type=F32)
    ext = jnp.concatenate([xprev_ref[...], xb], axis=0)
    xprev_ref[...] = xb[tr - V7X_SUBLANES:, :]
    xc = xb * cw_ref[CONV_W - 1:CONV_W, :] + cb_ref[...]
    for sft in range(1, CONV_W):
        shifted = pltpu.roll(ext, sft, axis=0)[V7X_SUBLANES:, :]
        xc = xc + shifted * cw_ref[CONV_W - 1 - sft:CONV_W - sft, :]
    xcb = xc.astype(BF16)
    pw = 2 * C_BLOCK_W
    ra, rx = [], []
    for q in range(C_BLOCKS // 2):
        t = jnp.dot(xcb[:, q * pw:(q + 1) * pw], wax_ref[q], preferred_element_type=F32)
        ra.append(t[:, :pw])
        rx.append(t[:, pw:])
    rgate = _sigmoid(jnp.concatenate(ra, axis=-1) + ba_ref[...])
    igate = _sigmoid(jnp.concatenate(rx, axis=-1) + bx_ref[...])
    z = -lam_ref[...]
    softplus = jnp.maximum(z, 0.0) + jnp.log(1.0 + jnp.exp(-jnp.abs(z)))
    log_a = -RG_C * rgate * softplus
    a = jnp.exp(log_a)
    mult = jnp.sqrt(jnp.maximum(1.0 - a * a, 0.0))
    bb = mult * igate * xc
    sub = V7X_SUBLANES
    ng = tr // sub

    def combine(acum, bcum, pos, n, axis):
        sft = 1
        while sft < n:
            keep = pos >= sft
            a_sh = jnp.where(keep, pltpu.roll(acum, sft, axis=axis), 1.0)
            b_sh = jnp.where(keep, pltpu.roll(bcum, sft, axis=axis), 0.0)
            bcum = acum * b_sh + bcum
            acum = acum * a_sh
            sft *= 2
        return acum, bcum

    pos = lax.broadcasted_iota(jnp.int32, (1, sub, 1), 1)
    acum, bcum = combine(a.reshape(ng, sub, D_RNN), bb.reshape(ng, sub, D_RNN), pos, sub, 1)
    acum = acum.reshape(tr, D_RNN)
    bcum = bcum.reshape(tr, D_RNN)
    ga, gb = [], []
    for j in range(D_RNN // V7X_LANES):
        ls = slice(j * V7X_LANES, (j + 1) * V7X_LANES)
        ascan_ref[j] = acum[:, ls]
        bscan_ref[j] = bcum[:, ls]
        ga.append(ascan_ref[j, pl.ds(sub - 1, ng, stride=sub), :])
        gb.append(bscan_ref[j, pl.ds(sub - 1, ng, stride=sub), :])
    ga = jnp.concatenate(ga, axis=-1)
    gb = jnp.concatenate(gb, axis=-1)
    grow = lax.broadcasted_iota(jnp.int32, (ng, 1), 0)
    ga, gb = combine(ga, gb, grow, ng, 0)
    hprev = hprev_ref[0:1, :]
    hend = ga * hprev + gb
    hin = jnp.where(grow == 0, hprev, pltpu.roll(hend, 1, axis=0))
    hprev_ref[0:1, :] = hend[ng - 1:ng, :]
    hin = jnp.broadcast_to(hin[:, None, :], (ng, sub, D_RNN)).reshape(tr, D_RNN)
    hs = acum * hin + bcum
    y = (hs * gate).astype(BF16)
    o_ref[...] = x + jnp.dot(y, wout_ref[...], preferred_element_type=F32)


def _rglru(x, g, w_in, conv_w, conv_b, wa, ba, wx, bx, lam, w_out, lead):
    b, s, d = x.shape
    tr = min(RGLRU_ROWS, s)
    pw = 2 * C_BLOCK_W
    wax = jnp.zeros((C_BLOCKS // 2, pw, 2 * pw), F32)
    for half in range(2):
        sl = slice(half * C_BLOCK_W, (half + 1) * C_BLOCK_W)
        wax = wax.at[:, sl, sl].set(wa[half::2])
        wax = wax.at[:, sl, pw + half * C_BLOCK_W:pw + (half + 1) * C_BLOCK_W].set(wx[half::2])
    wax = wax.astype(BF16)
    vec = lambda a: a.reshape(1, -1).astype(F32)
    args = [x, vec(g), w_in, conv_w.astype(F32), vec(conv_b), wax, vec(ba), vec(bx), vec(lam), w_out]
    tok = pl.BlockSpec((None, tr, d), lambda bi, i: (bi, i, 0))
    return pl.pallas_call(
        _rglru_body,
        out_shape=jax.ShapeDtypeStruct((b, s, d), F32),
        grid=(b, s // tr),
        in_specs=([tok, _const_spec(args[1].shape), _pick_spec(w_in.shape, lead)]
                  + [_const_spec(a.shape) for a in args[3:-1]] + [_pick_spec(w_out.shape, lead)]),
        out_specs=tok,
        scratch_shapes=[pltpu.VMEM((V7X_SUBLANES, D_RNN), F32), pltpu.VMEM((V7X_SUBLANES, D_RNN), F32),
                        pltpu.VMEM((D_RNN // V7X_LANES, tr, V7X_LANES), F32),
                        pltpu.VMEM((D_RNN // V7X_LANES, tr, V7X_LANES), F32)],
        compiler_params=_params("parallel", "arbitrary"),
        name="rglru",
    )(*args)


def kernel(x, norm_ffn, ffn_w_gate, ffn_w_up, ffn_w_down, norm_mix, ab_w_in, ab_w_out, att_rel_bias, rwkv_mu, rwkv_w0, rwkv_w_up, rwkv_a0, rwkv_a_up, rwkv_g_up, rwkv_k_k, rwkv_k_a, rwkv_r_k, rwkv_ln_w, rwkv_ln_b, rwkv_v0, rwkv_v_down, rwkv_v_up, c_w_in, c_conv_w, c_conv_b, c_wa, c_ba, c_wx, c_bx, c_lambda, c_w_out, norm_final):
    b, s, d = x.shape
    t = b * s
    depth = norm_mix.shape[0]
    wg, wu, wd = ffn_w_gate.astype(BF16), ffn_w_up.astype(BF16), ffn_w_down.astype(BF16)
    w_in, w_out = ab_w_in.astype(BF16), ab_w_out.astype(BF16)
    cw_in, cw_out = c_w_in.astype(BF16), c_w_out.astype(BF16)
    gfin = norm_final.reshape(1, d).astype(F32)
    x = x.astype(F32).reshape(t, d)
    v_first = None
    for l in range(depth):
        x = _ffn(x, norm_ffn[l, 0].reshape(1, d), wg, wu, wd, (l, 0), gfin, False)
        if l % 2 == 0:
            i = l // 2
            qkv, pb = _ab_in(x, norm_mix[l].reshape(1, d), w_in, (i,))
            oa = _attention(qkv.reshape(b, s, -1), att_rel_bias[i])
            prm = dict(mu=rwkv_mu[i], w0=rwkv_w0[i], w_up=rwkv_w_up[i], a0=rwkv_a0[i], a_up=rwkv_a_up[i],
                       g_up=rwkv_g_up[i], k_k=rwkv_k_k[i], k_a=rwkv_k_a[i], r_k=rwkv_r_k[i],
                       ln_w=rwkv_ln_w[i], ln_b=rwkv_ln_b[i])
            if i == 0:
                ob, v_first = _rwkv(pb.reshape(b, s, -1), None, prm, None)
            else:
                ob = _rwkv(pb.reshape(b, s, -1), v_first, prm,
                           (rwkv_v0[i - 1], rwkv_v_down[i - 1], rwkv_v_up[i - 1]))
            x = _ab_out(oa.reshape(t, -1), ob.reshape(t, -1), x, w_out, (i,))
        else:
            j = l // 2
            x = _rglru(x.reshape(b, s, d), norm_mix[l], cw_in, c_conv_w[j], c_conv_b[j], c_wa[j], c_ba[j],
                       c_wx[j], c_bx[j], c_lambda[j], cw_out, (j,)).reshape(t, d)
        x = _ffn(x, norm_ffn[l, 1].reshape(1, d), wg, wu, wd, (l, 1), gfin, l == depth - 1)
    return x.reshape(b, s, d)
```

```python
import functools

import jax
import jax.numpy as jnp
import numpy as np
from jax import lax
from jax.experimental import pallas as pl
from jax.experimental.pallas import tpu as pltpu

F32 = jnp.float32
BF16 = jnp.bfloat16

D_MODEL = 1024
DEPTH = 4
CHUNK = 64
LEFT_CHUNKS = 8
A_HEADS = 8
A_HEAD_DIM = 64
A_WIDTH = A_HEADS * A_HEAD_DIM
REL_CLIP = 128
B_HEADS = 8
B_HEAD_SIZE = 64
B_WIDTH = B_HEADS * B_HEAD_SIZE
W_LORA = 64
A_LORA = 64
G_LORA = 128
V_LORA = 32
DECAY_SCALE = 0.606531
GN_EPS = 64e-5
B_COLS = 3 * B_WIDTH + W_LORA + A_LORA + G_LORA
D_RNN = 1280
C_BLOCKS = 10
C_BLOCK_W = D_RNN // C_BLOCKS
CONV_W = 4
RG_C = 8.0
D_FF = 2816
NORM_EPS = 1e-6
NEG_INF = -1e30

V7X_LANES = 128
V7X_SUBLANES = 8
V7X_MXU_DIM = 256
V7X_VMEM_BYTES = 64 * 1024 * 1024

FFN_ROWS = 1024
FFN_CHUNKS = (0, 768, 1536, 2304, 2816)
PROJ_ROWS = 512
ATT_ROWS = 256
ATT_PREV = LEFT_CHUNKS * CHUNK
RWKV_CHUNK = 64
RGLRU_ROWS = 256
VMEM_LIMIT = 56 * 1024 * 1024


def _params(*sem):
    return pltpu.CompilerParams(dimension_semantics=sem, vmem_limit_bytes=VMEM_LIMIT)


def _const_spec(shape):
    nd = len(shape)
    return pl.BlockSpec(shape, lambda *_: (0,) * nd, pipeline_mode=pl.Buffered(1))


def _pick_spec(shape, lead):
    tail = tuple(shape[len(lead):])
    index = tuple(lead) + (0,) * len(tail)
    return pl.BlockSpec((None,) * len(lead) + tail, lambda *_: index, pipeline_mode=pl.Buffered(1))


def _mm(a, b):
    return jnp.dot(a.astype(BF16), b.astype(BF16), preferred_element_type=F32)


def _mm_nt(a, b):
    return lax.dot_general(a.astype(BF16), b.astype(BF16), (((1,), (1,)), ((), ())),
                           preferred_element_type=F32)


def _mm_tn(a, b):
    return lax.dot_general(a.astype(BF16), b.astype(BF16), (((0,), (0,)), ((), ())),
                           preferred_element_type=F32)


def _rms(x, g):
    ms = jnp.mean(x * x, axis=-1, keepdims=True)
    return x * lax.rsqrt(ms + NORM_EPS) * g


def _sigmoid(x):
    return 0.5 * jnp.tanh(0.5 * x) + 0.5


def _ffn_body(*refs, final_norm, mixer_out):
    if mixer_out:
        oa_ref, ob_ref, wo_ref, x_ref, g_ref, wg_ref, wu_ref, wd_ref, gf_ref, o_ref = refs
        x = x_ref[...] + jnp.dot(oa_ref[...], wo_ref[:A_WIDTH, :], preferred_element_type=F32)
        x = x + jnp.dot(ob_ref[...], wo_ref[A_WIDTH:, :], preferred_element_type=F32)
    else:
        x_ref, g_ref, wg_ref, wu_ref, wd_ref, gf_ref, o_ref = refs
        x = x_ref[...]
    h = _rms(x, g_ref[...]).astype(BF16)
    acc = None
    for lo, hi in zip(FFN_CHUNKS[:-1], FFN_CHUNKS[1:]):
        gt = jnp.dot(h, wg_ref[:, lo:hi], preferred_element_type=F32)
        up = jnp.dot(h, wu_ref[:, lo:hi], preferred_element_type=F32)
        act = (gt * _sigmoid(gt) * up).astype(BF16)
        d = jnp.dot(act, wd_ref[lo:hi, :], preferred_element_type=F32)
        acc = d if acc is None else acc + d
    y = x + 0.5 * acc
    if final_norm:
        y = _rms(y, gf_ref[...])
    o_ref[...] = y


def _ffn(x, g, wg, wu, wd, lead, gf, final_norm, mixer=None):
    t, d = x.shape
    tm = min(FFN_ROWS, t)
    row = pl.BlockSpec((tm, d), lambda i: (i, 0))
    args = [x, g, wg, wu, wd, gf]
    in_specs = [row, _const_spec((1, d)), _pick_spec(wg.shape, lead), _pick_spec(wu.shape, lead),
                _pick_spec(wd.shape, lead), _const_spec((1, d))]
    if mixer is not None:
        oa, ob, wo, lead_out = mixer
        args = [oa, ob, wo] + args
        in_specs = [pl.BlockSpec((tm, oa.shape[1]), lambda i: (i, 0)), pl.BlockSpec((tm, ob.shape[1]), lambda i: (i, 0)),
                    _pick_spec(wo.shape, lead_out)] + in_specs
    return pl.pallas_call(
        functools.partial(_ffn_body, final_norm=final_norm, mixer_out=mixer is not None),
        out_shape=jax.ShapeDtypeStruct((t, d), F32),
        grid=(t // tm,),
        in_specs=in_specs,
        out_specs=row,
        compiler_params=_params("parallel"),
        name="ffn_final" if final_norm else ("ffn_mix" if mixer is not None else "ffn"),
    )(*args)


def _ab_in_body(x_ref, g_ref, w_ref, qkv_ref, pb_ref):
    h = _rms(x_ref[...], g_ref[...]).astype(BF16)
    na = qkv_ref.shape[-1]
    qkv_ref[...] = jnp.dot(h, w_ref[:, :na], preferred_element_type=F32).astype(BF16)
    pb_ref[...] = jnp.dot(h, w_ref[:, na:], preferred_element_type=F32)


def _ab_in(x, g, w, lead):
    t, d = x.shape
    n = w.shape[-1]
    na = 3 * A_WIDTH
    tm = min(PROJ_ROWS, t)
    return pl.pallas_call(
        _ab_in_body,
        out_shape=(jax.ShapeDtypeStruct((t, na), BF16), jax.ShapeDtypeStruct((t, n - na), F32)),
        grid=(t // tm,),
        in_specs=[pl.BlockSpec((tm, d), lambda i: (i, 0)), _const_spec((1, d)), _pick_spec(w.shape, lead)],
        out_specs=(pl.BlockSpec((tm, na), lambda i: (i, 0)), pl.BlockSpec((tm, n - na), lambda i: (i, 0))),
        compiler_params=_params("parallel"),
        name="ab_in",
    )(x, g, w)


def _attn_body(*refs, n_kblocks):
    q_ref = refs[0]
    k_refs = refs[1:1 + n_kblocks]
    v_refs = refs[1 + n_kblocks:1 + 2 * n_kblocks]
    bias_ref = refs[1 + 2 * n_kblocks]
    o_ref = refs[2 + 2 * n_kblocks]
    qb = q_ref.shape[0]
    i = pl.program_id(1)
    n_prev = n_kblocks - 1
    pens = [jnp.where(i >= n_prev - j, 0.0, NEG_INF).astype(F32) for j in range(n_prev)] + [None]
    lane = lax.broadcasted_iota(jnp.int32, (1, 2 * A_HEAD_DIM), 1)
    first = lane < A_HEAD_DIM
    scale = A_HEAD_DIM ** -0.5
    for pr in range(A_HEADS // 2):
        ls = slice(2 * A_HEAD_DIM * pr, 2 * A_HEAD_DIM * (pr + 1))
        q_pair = q_ref[:, ls]
        ks = [r[:, ls] for r in k_refs]
        vs = [r[:, ls] for r in v_refs]
        outs = []
        for hh in range(2):
            head = 2 * pr + hh
            sel = first if hh == 0 else jnp.logical_not(first)
            qm = jnp.where(sel, scale, 0.0).astype(BF16)
            qh = q_pair * qm
            s = []
            for j in range(n_kblocks):
                sj = lax.dot_general(qh, ks[j], (((1,), (1,)), ((), ())), preferred_element_type=F32)
                sj = sj + bias_ref[head, :, j * qb:(j + 1) * qb]
                if pens[j] is not None:
                    sj = sj + pens[j]
                s.append(sj)
            m = s[0].max(axis=-1, keepdims=True)
            for sj in s[1:]:
                m = jnp.maximum(m, sj.max(axis=-1, keepdims=True))
            l = None
            pv = None
            for j in range(n_kblocks):
                p = jnp.exp(s[j] - m)
                lj = p.sum(axis=-1, keepdims=True)
                pvj = jnp.dot(p.astype(BF16), vs[j], preferred_element_type=F32)
                l = lj if l is None else l + lj
                pv = pvj if pv is None else pv + pvj
            outs.append(pv / l)
        o_ref[:, ls] = jnp.where(first, outs[0], outs[1]).astype(BF16)


def _attn_bias_tile(rel_bias, qb):
    nh = rel_bias.shape[0]
    w = ATT_PREV + qb
    period = w + qb
    near = 2 * REL_CLIP + 1
    assert ATT_PREV >= REL_CLIP and qb >= REL_CLIP
    far_past = jnp.broadcast_to(rel_bias[:, near - 1:], (nh, ATT_PREV - REL_CLIP))
    far_future = jnp.broadcast_to(rel_bias[:, :1], (nh, qb - REL_CLIP))
    wrapped = jnp.broadcast_to(rel_bias[:, near - 1:], (nh, qb - 1))
    t = jnp.concatenate([far_past, rel_bias[:, ::-1], far_future, wrapped], axis=1).astype(F32)
    assert t.shape[1] == period
    toe = jnp.broadcast_to(t[:, None, :], (nh, qb, period)).reshape(nh, qb * period)
    toe = toe[:, :qb * (period - 1)].reshape(nh, qb, period - 1)[:, :, :w]
    r = np.arange(qb)[:, None]
    c = np.arange(w)[None, :]
    band = (c // CHUNK >= r // CHUNK) & (c // CHUNK <= r // CHUNK + LEFT_CHUNKS)
    return jnp.where(band[None], toe, NEG_INF)


def _attention(qkv, rel_bias):
    b, s, _ = qkv.shape
    qb = min(ATT_ROWS, s)
    assert ATT_PREV % qb == 0 and s % qb == 0
    n_prev = ATT_PREV // qb
    n_kblocks = n_prev + 1
    bias = _attn_bias_tile(rel_bias, qb)

    def blk(col, back):
        return pl.BlockSpec((None, qb, A_WIDTH), lambda bi, i: (bi, jnp.maximum(i - back, 0), col))

    in_specs = ([blk(0, 0)] + [blk(1, n_prev - j) for j in range(n_kblocks)]
                + [blk(2, n_prev - j) for j in range(n_kblocks)] + [_const_spec(bias.shape)])
    return pl.pallas_call(
        functools.partial(_attn_body, n_kblocks=n_kblocks),
        out_shape=jax.ShapeDtypeStruct((b, s, A_WIDTH), BF16),
        grid=(b, s // qb),
        in_specs=in_specs,
        out_specs=pl.BlockSpec((None, qb, A_WIDTH), lambda bi, i: (bi, i, 0)),
        compiler_params=_params("parallel", "parallel"),
        name="attention",
    )(*([qkv] * (1 + 2 * n_kblocks)), bias)


def _split3(x):
    h1 = x.astype(BF16)
    r1 = x - h1.astype(F32)
    h2 = r1.astype(BF16)
    h3 = (r1 - h2.astype(F32)).astype(BF16)
    return h1, h2, h3


def _rwkv_body(*refs, has_vmix):
    if has_vmix:
        (pb_ref, vf_ref, mu_ref, w0_ref, wup_ref, a0_ref, aup_ref, gup_ref, kk_ref, ka_ref, rk_ref,
         lnw_ref, lnb_ref, v0_ref, vdn_ref, vup_ref, o_ref, state_ref, carry_ref) = refs
        vraw_ref = None
    else:
        (pb_ref, mu_ref, w0_ref, wup_ref, a0_ref, aup_ref, gup_ref, kk_ref, ka_ref, rk_ref,
         lnw_ref, lnb_ref, o_ref, vraw_ref, state_ref, carry_ref) = refs
    nb, c, ncol = pb_ref.shape
    n = B_HEAD_SIZE
    rows = nb * c

    @pl.when(pl.program_id(0) == 0)
    def _():
        state_ref[...] = jnp.zeros_like(state_ref)
        carry_ref[...] = jnp.zeros_like(carry_ref)

    p = pb_ref[...].reshape(rows, ncol)
    row = lax.broadcasted_iota(jnp.int32, (rows, 1), 0)
    prev = pltpu.roll(p, 1, axis=0)
    for b in range(nb):
        prev = jnp.where(row == b * c, carry_ref[b, 0:1, :], prev)
    for b in range(nb):
        carry_ref[b, 0:1, :] = p[(b + 1) * c - 1:(b + 1) * c, :]
    ps = p + (prev - p) * mu_ref[...]
    r = ps[:, 0:B_WIDTH]
    k = ps[:, B_WIDTH:2 * B_WIDTH]
    v = ps[:, 2 * B_WIDTH:3 * B_WIDTH]
    wad = ps[:, 3 * B_WIDTH:3 * B_WIDTH + W_LORA + A_LORA]
    gd = ps[:, 3 * B_WIDTH + W_LORA + A_LORA:]
    if has_vmix:
        mix = _sigmoid(v0_ref[...] + _mm(_mm(v, vdn_ref[...]), vup_ref[...]))
        v = v + (vf_ref[...].reshape(rows, B_WIDTH) - v) * mix
    else:
        vraw_ref[...] = v.reshape(nb, c, B_WIDTH)
    wlog = -DECAY_SCALE * _sigmoid(w0_ref[...] + _mm(jnp.tanh(wad), wup_ref[...]))
    a = _sigmoid(a0_ref[...] + _mm(wad, aup_ref[...]))
    g = _mm(_sigmoid(gd), gup_ref[...])
    kkf = k * kk_ref[...]
    kka = kkf * a
    kmod = k * (1.0 + (a - 1.0) * ka_ref[...])
    rkr = r * kmod * rk_ref[...]

    ti = lax.broadcasted_iota(jnp.int32, (c, 3 * c), 0)
    tj = lax.broadcasted_iota(jnp.int32, (c, 3 * c), 1) % c
    tri3 = jnp.where(tj <= ti, 1.0, 0.0).astype(BF16)
    lcum, lend = [], []
    for b in range(nb):
        lb = jnp.dot(tri3, jnp.concatenate(_split3(wlog[b * c:(b + 1) * c]), axis=0),
                     preferred_element_type=F32)
        lcum.append(lb)
        lend.append(jnp.broadcast_to(lb[c - 1:c, :], lb.shape))
    lcum = jnp.concatenate(lcum, axis=0)
    lend = jnp.concatenate(lend, axis=0)
    e_neg = jnp.exp(-lcum)
    e_end = jnp.exp(lend - lcum)
    w_end = jnp.exp(lend)

    pair = 2 * n
    npair = B_HEADS // 2
    lo = lax.broadcasted_iota(jnp.int32, (1, 1, pair), 2) < n

    def slabs(q):
        return jnp.stack([q[b * c:(b + 1) * c, p * pair:(p + 1) * pair]
                          for b in range(nb) for p in range(npair)], axis=0)

    def unslabs(q):
        return jnp.concatenate(
            [jnp.concatenate([q[b * npair + p] for p in range(npair)], axis=-1) for b in range(nb)], axis=0)

    def head_sum(q):
        s_lo = jnp.sum(jnp.where(lo, q, 0.0), axis=-1, keepdims=True)
        s_hi = jnp.sum(jnp.where(lo, 0.0, q), axis=-1, keepdims=True)
        return jnp.where(lo, s_lo, s_hi)

    def bd(q):
        return jnp.concatenate([jnp.where(lo, q, 0.0), jnp.where(lo, 0.0, q)], axis=1)

    def bmm(lhs, rhs):
        return jnp.einsum("gij,gjv->giv", lhs.astype(BF16), rhs.astype(BF16), preferred_element_type=F32)

    kk_s = slabs(kkf)
    rs = lax.rsqrt(jnp.maximum(head_sum(kk_s * kk_s), 1e-24))
    x = jnp.concatenate([slabs(-kkf * jnp.exp(lcum - wlog)) * rs, slabs(r * jnp.exp(lcum))], axis=1)
    bt = slabs(kka * e_neg) * rs
    kt = slabs(kmod * e_neg)
    bk_end = jnp.concatenate([slabs(kka * e_end) * rs, slabs(kmod * e_end)], axis=1)
    vh = slabs(v)
    s0 = state_ref[...]
    xb = x.astype(BF16)
    ybd = jnp.concatenate([bd(bt), bd(kt)], axis=1).astype(BF16)
    gm = jnp.einsum("gik,gjk->gij", xb, ybd, preferred_element_type=F32)
    xs = jnp.einsum("gik,gvk->giv", xb, s0.astype(BF16), preferred_element_type=F32)
    ri = lax.broadcasted_iota(jnp.int32, (1, c, 2 * c), 1)
    ci = lax.broadcasted_iota(jnp.int32, (1, c, 2 * c), 2) % c
    strict = ci < ri
    incl = ci <= ri
    m_ab = jnp.where(strict, gm[:, :c, :2 * c], 0.0)
    m_ak = jnp.where(strict, gm[:, :c, 2 * c:], 0.0)
    m_rb = jnp.where(incl, gm[:, c:, :2 * c], 0.0)
    m_rk = jnp.where(incl, gm[:, c:, 2 * c:], 0.0)
    bdv = bd(vh)
    u = xs[:, :c] + bmm(m_ak, bdv)
    pw = m_ab
    span = 1
    while True:
        u = u + bmm(pw, bd(u))
        span *= 2
        if span >= c:
            break
        pw = bmm(pw, bd(pw))
    o = xs[:, c:] + bmm(jnp.concatenate([m_rb, m_rk], axis=2), jnp.concatenate([bd(u), bdv], axis=1))
    w_end_s = jnp.stack([w_end[b * c:b * c + 1, p * pair:(p + 1) * pair]
                         for b in range(nb) for p in range(npair)], axis=0)
    uv = jnp.concatenate([u, vh], axis=1)
    upd = jnp.einsum("gjv,gjk->gvk", uv.astype(BF16), bk_end.astype(BF16), preferred_element_type=F32)
    same_head = (lax.broadcasted_iota(jnp.int32, (1, pair, pair), 1) < n) == lo
    state_ref[...] = s0 * w_end_s + jnp.where(same_head, upd, 0.0)
    mean = head_sum(o) * (1.0 / n)
    d = o - mean
    var = head_sum(d * d) * (1.0 / n)
    o = d * lax.rsqrt(var + GN_EPS)
    bonus = head_sum(slabs(rkr)) * vh
    out = (unslabs(o) * lnw_ref[...] + lnb_ref[...] + unslabs(bonus)) * g
    o_ref[...] = out.astype(BF16).reshape(nb, c, B_WIDTH)


def _rwkv(pb, v_first, prm, vmix):
    b, s, ncol = pb.shape
    c = min(RWKV_CHUNK, s)
    has_vmix = vmix is not None
    tok = lambda w: pl.BlockSpec((b, c, w), lambda i: (0, i, 0))
    vec = lambda a: a.reshape(1, -1).astype(F32)
    lw = W_LORA + A_LORA
    wup = jnp.zeros((lw, B_WIDTH), F32).at[:W_LORA].set(prm["w_up"]).astype(BF16)
    aup = jnp.zeros((lw, B_WIDTH), F32).at[W_LORA:].set(prm["a_up"]).astype(BF16)
    common = [vec(prm["mu"]), vec(prm["w0"]), wup, vec(prm["a0"]), aup, prm["g_up"].astype(BF16),
              vec(prm["k_k"]), vec(prm["k_a"]), vec(prm["r_k"]), vec(prm["ln_w"]), vec(prm["ln_b"])]
    if has_vmix:
        v0, v_down, v_up = vmix
        vdn = jnp.zeros((B_WIDTH, V7X_LANES), F32).at[:, :V_LORA].set(v_down).astype(BF16)
        vup = jnp.zeros((V7X_LANES, B_WIDTH), F32).at[:V_LORA].set(v_up).astype(BF16)
        args = [pb, v_first] + common + [vec(v0), vdn, vup]
        in_specs = [tok(ncol), tok(B_WIDTH)] + [_const_spec(a.shape) for a in args[2:]]
        out_shape = jax.ShapeDtypeStruct((b, s, B_WIDTH), BF16)
        out_specs = tok(B_WIDTH)
    else:
        args = [pb] + common
        in_specs = [tok(ncol)] + [_const_spec(a.shape) for a in args[1:]]
        out_shape = (jax.ShapeDtypeStruct((b, s, B_WIDTH), BF16), jax.ShapeDtypeStruct((b, s, B_WIDTH), F32))
        out_specs = (tok(B_WIDTH), tok(B_WIDTH))
    return pl.pallas_call(
        functools.partial(_rwkv_body, has_vmix=has_vmix),
        out_shape=out_shape,
        grid=(s // c,),
        in_specs=in_specs,
        out_specs=out_specs,
        scratch_shapes=[pltpu.VMEM((b * B_HEADS // 2, 2 * B_HEAD_SIZE, 2 * B_HEAD_SIZE), F32),
                        pltpu.VMEM((b, V7X_SUBLANES, ncol), F32)],
        compiler_params=_params("arbitrary"),
        name="rwkv_mix" if has_vmix else "rwkv",
    )(*args)


def _gelu_tanh(x):
    return 0.5 * x * (1.0 + jnp.tanh(np.sqrt(2.0 / np.pi).astype(np.float32) * (x + 0.044715 * (x * x * x))))


def _rglru_body(x_ref, g_ref, win_ref, cw_ref, cb_ref, wax_ref, ba_ref, bx_ref, lam_ref, wout_ref,
                o_ref, xprev_ref, hprev_ref, ascan_ref, bscan_ref):
    tr = x_ref.shape[0]

    @pl.when(pl.program_id(1) == 0)
    def _():
        xprev_ref[...] = jnp.zeros_like(xprev_ref)
        hprev_ref[...] = jnp.zeros_like(hprev_ref)

    x = x_ref[...]
    h = _rms(x, g_ref[...]).astype(BF16)
    gate = _gelu_tanh(jnp.dot(h, win_ref[:, :D_RNN], preferred_element_type=F32))
    xb = jnp.dot(h, win_ref[:, D_RNN:], preferred_element_type=F32)
    ext = jnp.concatenate([xprev_ref[...], xb], axis=0)
    xprev_ref[...] = xb[tr - V7X_SUBLANES:, :]
    xc = xb * cw_ref[CONV_W - 1:CONV_W, :] + cb_ref[...]
    for sft in range(1, CONV_W):
        shifted = pltpu.roll(ext, sft, axis=0)[V7X_SUBLANES:, :]
        xc = xc + shifted * cw_ref[CONV_W - 1 - sft:CONV_W - sft, :]
    xcb = xc.astype(BF16)
    pw = 2 * C_BLOCK_W
    ra, rx = [], []
    for q in range(C_BLOCKS // 2):
        t = jnp.dot(xcb[:, q * pw:(q + 1) * pw], wax_ref[q], preferred_element_type=F32)
        ra.append(t[:, :pw])
        rx.append(t[:, pw:])
    rgate = _sigmoid(jnp.concatenate(ra, axis=-1) + ba_ref[...])
    igate = _sigmoid(jnp.concatenate(rx, axis=-1) + bx_ref[...])
    z = -lam_ref[...]
    softplus = jnp.maximum(z, 0.0) + jnp.log(1.0 + jnp.exp(-jnp.abs(z)))
    log_a = -RG_C * rgate * softplus
    a = jnp.exp(log_a)
    mult = jnp.sqrt(jnp.maximum(1.0 - a * a, 0.0))
    bb = mult * igate * xc
    sub = V7X_SUBLANES
    ng = tr // sub

    def combine(acum, bcum, pos, n, axis):
        sft = 1
        while sft < n:
            keep = pos >= sft
            a_sh = jnp.where(keep, pltpu.roll(acum, sft, axis=axis), 1.0)
            b_sh = jnp.where(keep, pltpu.roll(bcum, sft, axis=axis), 0.0)
            bcum = acum * b_sh + bcum
            acum = acum * a_sh
            sft *= 2
        return acum, bcum

    pos = lax.broadcasted_iota(jnp.int32, (1, sub, 1), 1)
    acum, bcum = combine(a.reshape(ng, sub, D_RNN), bb.reshape(ng, sub, D_RNN), pos, sub, 1)
    acum = acum.reshape(tr, D_RNN)
    bcum = bcum.reshape(tr, D_RNN)
    ga, gb = [], []
    for j in range(D_RNN // V7X_LANES):
        ls = slice(j * V7X_LANES, (j + 1) * V7X_LANES)
        ascan_ref[j] = acum[:, ls]
        bscan_ref[j] = bcum[:, ls]
        ga.append(ascan_ref[j, pl.ds(sub - 1, ng, stride=sub), :])
        gb.append(bscan_ref[j, pl.ds(sub - 1, ng, stride=sub), :])
    ga = jnp.concatenate(ga, axis=-1)
    gb = jnp.concatenate(gb, axis=-1)
    grow = lax.broadcasted_iota(jnp.int32, (ng, 1), 0)
    ga, gb = combine(ga, gb, grow, ng, 0)
    hprev = hprev_ref[0:1, :]
    hend = ga * hprev + gb
    hin = jnp.where(grow == 0, hprev, pltpu.roll(hend, 1, axis=0))
    hprev_ref[0:1, :] = hend[ng - 1:ng, :]
    hin = jnp.broadcast_to(hin[:, None, :], (ng, sub, D_RNN)).reshape(tr, D_RNN)
    hs = acum * hin + bcum
    y = (hs * gate).astype(BF16)
    o_ref[...] = x + jnp.dot(y, wout_ref[...], preferred_element_type=F32)


def _rglru(x, g, w_in, conv_w, conv_b, wa, ba, wx, bx, lam, w_out, lead):
    b, s, d = x.shape
    tr = min(RGLRU_ROWS, s)
    pw = 2 * C_BLOCK_W
    wax = jnp.zeros((C_BLOCKS // 2, pw, 2 * pw), F32)
    for half in range(2):
        sl = slice(half * C_BLOCK_W, (half + 1) * C_BLOCK_W)
        wax = wax.at[:, sl, sl].set(wa[half::2])
        wax = wax.at[:, sl, pw + half * C_BLOCK_W:pw + (half + 1) * C_BLOCK_W].set(wx[half::2])
    wax = wax.astype(BF16)
    vec = lambda a: a.reshape(1, -1).astype(F32)
    args = [x, vec(g), w_in, conv_w.astype(F32), vec(conv_b), wax, vec(ba), vec(bx), vec(lam), w_out]
    tok = pl.BlockSpec((None, tr, d), lambda bi, i: (bi, i, 0))
    return pl.pallas_call(
        _rglru_body,
        out_shape=jax.ShapeDtypeStruct((b, s, d), F32),
        grid=(b, s // tr),
        in_specs=([tok, _const_spec(args[1].shape), _pick_spec(w_in.shape, lead)]
                  + [_const_spec(a.shape) for a in args[3:-1]] + [_pick_spec(w_out.shape, lead)]),
        out_specs=tok,
        scratch_shapes=[pltpu.VMEM((V7X_SUBLANES, D_RNN), F32), pltpu.VMEM((V7X_SUBLANES, D_RNN), F32),
                        pltpu.VMEM((D_RNN // V7X_LANES, tr, V7X_LANES), F32),
                        pltpu.VMEM((D_RNN // V7X_LANES, tr, V7X_LANES), F32)],
        compiler_params=_params("parallel", "arbitrary"),
        name="rglru",
    )(*args)


def kernel(x, norm_ffn, ffn_w_gate, ffn_w_up, ffn_w_down, norm_mix, ab_w_in, ab_w_out, att_rel_bias, rwkv_mu, rwkv_w0, rwkv_w_up, rwkv_a0, rwkv_a_up, rwkv_g_up, rwkv_k_k, rwkv_k_a, rwkv_r_k, rwkv_ln_w, rwkv_ln_b, rwkv_v0, rwkv_v_down, rwkv_v_up, c_w_in, c_conv_w, c_conv_b, c_wa, c_ba, c_wx, c_bx, c_lambda, c_w_out, norm_final):
    b, s, d = x.shape
    t = b * s
    depth = norm_mix.shape[0]
    wg, wu, wd = ffn_w_gate.astype(BF16), ffn_w_up.astype(BF16), ffn_w_down.astype(BF16)
    w_in, w_out = ab_w_in.astype(BF16), ab_w_out.astype(BF16)
    cw_in, cw_out = c_w_in.astype(BF16), c_w_out.astype(BF16)
    gfin = norm_final.reshape(1, d).astype(F32)
    x = x.astype(F32).reshape(t, d)
    v_first = None
    for l in range(depth):
        x = _ffn(x, norm_ffn[l, 0].reshape(1, d), wg, wu, wd, (l, 0), gfin, False)
        mixer = None
        if l % 2 == 0:
            i = l // 2
            qkv, pb = _ab_in(x, norm_mix[l].reshape(1, d), w_in, (i,))
            oa = _attention(qkv.reshape(b, s, -1), att_rel_bias[i])
            prm = dict(mu=rwkv_mu[i], w0=rwkv_w0[i], w_up=rwkv_w_up[i], a0=rwkv_a0[i], a_up=rwkv_a_up[i],
                       g_up=rwkv_g_up[i], k_k=rwkv_k_k[i], k_a=rwkv_k_a[i], r_k=rwkv_r_k[i],
                       ln_w=rwkv_ln_w[i], ln_b=rwkv_ln_b[i])
            if i == 0:
                ob, v_first = _rwkv(pb.reshape(b, s, -1), None, prm, None)
            else:
                ob = _rwkv(pb.reshape(b, s, -1), v_first, prm,
                           (rwkv_v0[i - 1], rwkv_v_down[i - 1], rwkv_v_up[i - 1]))
            mixer = (oa.reshape(t, -1), ob.reshape(t, -1), w_out, (i,))
        else:
            j = l // 2
            x = _rglru(x.reshape(b, s, d), norm_mix[l], cw_in, c_conv_w[j], c_conv_b[j], c_wa[j], c_ba[j],
                       c_wx[j], c_bx[j], c_lambda[j], cw_out, (j,)).reshape(t, d)
        x = _ffn(x, norm_ffn[l, 1].reshape(1, d), wg, wu, wd, (l, 1), gfin, l == depth - 1, mixer)
    return x.reshape(b, s, d)
```

```python
import functools

import jax
import jax.numpy as jnp
import numpy as np
from jax import lax
from jax.experimental import pallas as pl
from jax.experimental.pallas import tpu as pltpu

F32 = jnp.float32
BF16 = jnp.bfloat16

D_MODEL = 1024
DEPTH = 4
CHUNK = 64
LEFT_CHUNKS = 8
A_HEADS = 8
A_HEAD_DIM = 64
A_WIDTH = A_HEADS * A_HEAD_DIM
REL_CLIP = 128
B_HEADS = 8
B_HEAD_SIZE = 64
B_WIDTH = B_HEADS * B_HEAD_SIZE
W_LORA = 64
A_LORA = 64
G_LORA = 128
V_LORA = 32
DECAY_SCALE = 0.606531
GN_EPS = 64e-5
B_COLS = 3 * B_WIDTH + W_LORA + A_LORA + G_LORA
D_RNN = 1280
C_BLOCKS = 10
C_BLOCK_W = D_RNN // C_BLOCKS
CONV_W = 4
RG_C = 8.0
D_FF = 2816
NORM_EPS = 1e-6
NEG_INF = -1e30

V7X_LANES = 128
V7X_SUBLANES = 8
V7X_MXU_DIM = 256
V7X_VMEM_BYTES = 64 * 1024 * 1024

FFN_ROWS = 1024
FFN_CHUNKS = (0, 768, 1536, 2304, 2816)
PROJ_ROWS = 512
ATT_ROWS = 256
ATT_PREV = LEFT_CHUNKS * CHUNK
RWKV_CHUNK = 64
RGLRU_ROWS = 256
RGLRU_BATCH = 1
VMEM_LIMIT = 56 * 1024 * 1024


def _params(*sem):
    return pltpu.CompilerParams(dimension_semantics=sem, vmem_limit_bytes=VMEM_LIMIT)


def _const_spec(shape):
    nd = len(shape)
    return pl.BlockSpec(shape, lambda *_: (0,) * nd, pipeline_mode=pl.Buffered(1))


def _pick_spec(shape, lead):
    tail = tuple(shape[len(lead):])
    index = tuple(lead) + (0,) * len(tail)
    return pl.BlockSpec((None,) * len(lead) + tail, lambda *_: index, pipeline_mode=pl.Buffered(1))


def _mm(a, b):
    return jnp.dot(a.astype(BF16), b.astype(BF16), preferred_element_type=F32)


def _mm_nt(a, b):
    return lax.dot_general(a.astype(BF16), b.astype(BF16), (((1,), (1,)), ((), ())),
                           preferred_element_type=F32)


def _mm_tn(a, b):
    return lax.dot_general(a.astype(BF16), b.astype(BF16), (((0,), (0,)), ((), ())),
                           preferred_element_type=F32)


def _rms(x, g):
    ms = jnp.mean(x * x, axis=-1, keepdims=True)
    return x * lax.rsqrt(ms + NORM_EPS) * g


def _sigmoid(x):
    return 0.5 * jnp.tanh(0.5 * x) + 0.5


def _ffn_body(*refs, final_norm, mixer_out):
    if mixer_out:
        oa_ref, ob_ref, wo_ref, x_ref, g_ref, wg_ref, wu_ref, wd_ref, gf_ref, o_ref = refs
        x = x_ref[...] + jnp.dot(oa_ref[...], wo_ref[:A_WIDTH, :], preferred_element_type=F32)
        x = x + jnp.dot(ob_ref[...], wo_ref[A_WIDTH:, :], preferred_element_type=F32)
    else:
        x_ref, g_ref, wg_ref, wu_ref, wd_ref, gf_ref, o_ref = refs
        x = x_ref[...]
    h = _rms(x, g_ref[...]).astype(BF16)
    acc = None
    for lo, hi in zip(FFN_CHUNKS[:-1], FFN_CHUNKS[1:]):
        gt = jnp.dot(h, wg_ref[:, lo:hi], preferred_element_type=F32)
        up = jnp.dot(h, wu_ref[:, lo:hi], preferred_element_type=F32)
        act = (gt * _sigmoid(gt) * up).astype(BF16)
        d = jnp.dot(act, wd_ref[lo:hi, :], preferred_element_type=F32)
        acc = d if acc is None else acc + d
    y = x + 0.5 * acc
    if final_norm:
        y = _rms(y, gf_ref[...])
    o_ref[...] = y


def _ffn(x, g, wg, wu, wd, lead, gf, final_norm, mixer=None):
    t, d = x.shape
    tm = min(FFN_ROWS, t)
    row = pl.BlockSpec((tm, d), lambda i: (i, 0))
    args = [x, g, wg, wu, wd, gf]
    in_specs = [row, _const_spec((1, d)), _pick_spec(wg.shape, lead), _pick_spec(wu.shape, lead),
                _pick_spec(wd.shape, lead), _const_spec((1, d))]
    if mixer is not None:
        oa, ob, wo, lead_out = mixer
        args = [oa, ob, wo] + args
        in_specs = [pl.BlockSpec((tm, oa.shape[1]), lambda i: (i, 0)), pl.BlockSpec((tm, ob.shape[1]), lambda i: (i, 0)),
                    _pick_spec(wo.shape, lead_out)] + in_specs
    return pl.pallas_call(
        functools.partial(_ffn_body, final_norm=final_norm, mixer_out=mixer is not None),
        out_shape=jax.ShapeDtypeStruct((t, d), F32),
        grid=(t // tm,),
        in_specs=in_specs,
        out_specs=row,
        compiler_params=_params("parallel"),
        name="ffn_final" if final_norm else ("ffn_mix" if mixer is not None else "ffn"),
    )(*args)


def _ab_in_body(x_ref, g_ref, w_ref, qkv_ref, pb_ref):
    h = _rms(x_ref[...], g_ref[...]).astype(BF16)
    na = qkv_ref.shape[-1]
    qkv_ref[...] = jnp.dot(h, w_ref[:, :na], preferred_element_type=F32).astype(BF16)
    pb_ref[...] = jnp.dot(h, w_ref[:, na:], preferred_element_type=F32)


def _ab_in(x, g, w, lead):
    t, d = x.shape
    n = w.shape[-1]
    na = 3 * A_WIDTH
    tm = min(PROJ_ROWS, t)
    return pl.pallas_call(
        _ab_in_body,
        out_shape=(jax.ShapeDtypeStruct((t, na), BF16), jax.ShapeDtypeStruct((t, n - na), F32)),
        grid=(t // tm,),
        in_specs=[pl.BlockSpec((tm, d), lambda i: (i, 0)), _const_spec((1, d)), _pick_spec(w.shape, lead)],
        out_specs=(pl.BlockSpec((tm, na), lambda i: (i, 0)), pl.BlockSpec((tm, n - na), lambda i: (i, 0))),
        compiler_params=_params("parallel"),
        name="ab_in",
    )(x, g, w)


def _attn_body(*refs, n_kblocks):
    q_ref = refs[0]
    k_refs = refs[1:1 + n_kblocks]
    v_refs = refs[1 + n_kblocks:1 + 2 * n_kblocks]
    bias_ref = refs[1 + 2 * n_kblocks]
    o_ref = refs[2 + 2 * n_kblocks]
    qb = q_ref.shape[0]
    i = pl.program_id(1)
    n_prev = n_kblocks - 1
    pens = [jnp.where(i >= n_prev - j, 0.0, NEG_INF).astype(F32) for j in range(n_prev)] + [None]
    lane = lax.broadcasted_iota(jnp.int32, (1, 2 * A_HEAD_DIM), 1)
    first = lane < A_HEAD_DIM
    scale = A_HEAD_DIM ** -0.5
    for pr in range(A_HEADS // 2):
        ls = slice(2 * A_HEAD_DIM * pr, 2 * A_HEAD_DIM * (pr + 1))
        q_pair = q_ref[:, ls]
        ks = [r[:, ls] for r in k_refs]
        vs = [r[:, ls] for r in v_refs]
        outs = []
        for hh in range(2):
            head = 2 * pr + hh
            sel = first if hh == 0 else jnp.logical_not(first)
            qm = jnp.where(sel, scale, 0.0).astype(BF16)
            qh = q_pair * qm
            s = []
            for j in range(n_kblocks):
                sj = lax.dot_general(qh, ks[j], (((1,), (1,)), ((), ())), preferred_element_type=F32)
                sj = sj + bias_ref[head, :, j * qb:(j + 1) * qb]
                if pens[j] is not None:
                    sj = sj + pens[j]
                s.append(sj)
            m = s[0].max(axis=-1, keepdims=True)
            for sj in s[1:]:
                m = jnp.maximum(m, sj.max(axis=-1, keepdims=True))
            l = None
            pv = None
            for j in range(n_kblocks):
                p = jnp.exp(s[j] - m)
                lj = p.sum(axis=-1, keepdims=True)
                pvj = jnp.dot(p.astype(BF16), vs[j], preferred_element_type=F32)
                l = lj if l is None else l + lj
                pv = pvj if pv is None else pv + pvj
            outs.append(pv / l)
        o_ref[:, ls] = jnp.where(first, outs[0], outs[1]).astype(BF16)


def _attn_bias_tile(rel_bias, qb):
    nh = rel_bias.shape[0]
    w = ATT_PREV + qb
    period = w + qb
    near = 2 * REL_CLIP + 1
    assert ATT_PREV >= REL_CLIP and qb >= REL_CLIP
    far_past = jnp.broadcast_to(rel_bias[:, near - 1:], (nh, ATT_PREV - REL_CLIP))
    far_future = jnp.broadcast_to(rel_bias[:, :1], (nh, qb - REL_CLIP))
    wrapped = jnp.broadcast_to(rel_bias[:, near - 1:], (nh, qb - 1))
    t = jnp.concatenate([far_past, rel_bias[:, ::-1], far_future, wrapped], axis=1).astype(F32)
    assert t.shape[1] == period
    toe = jnp.broadcast_to(t[:, None, :], (nh, qb, period)).reshape(nh, qb * period)
    toe = toe[:, :qb * (period - 1)].reshape(nh, qb, period - 1)[:, :, :w]
    r = np.arange(qb)[:, None]
    c = np.arange(w)[None, :]
    band = (c // CHUNK >= r // CHUNK) & (c // CHUNK <= r // CHUNK + LEFT_CHUNKS)
    return jnp.where(band[None], toe, NEG_INF)


def _attention(qkv, rel_bias):
    b, s, _ = qkv.shape
    qb = min(ATT_ROWS, s)
    assert ATT_PREV % qb == 0 and s % qb == 0
    n_prev = ATT_PREV // qb
    n_kblocks = n_prev + 1
    bias = _attn_bias_tile(rel_bias, qb)

    def blk(col, back):
        return pl.BlockSpec((None, qb, A_WIDTH), lambda bi, i: (bi, jnp.maximum(i - back, 0), col))

    in_specs = ([blk(0, 0)] + [blk(1, n_prev - j) for j in range(n_kblocks)]
                + [blk(2, n_prev - j) for j in range(n_kblocks)] + [_const_spec(bias.shape)])
    return pl.pallas_call(
        functools.partial(_attn_body, n_kblocks=n_kblocks),
        out_shape=jax.ShapeDtypeStruct((b, s, A_WIDTH), BF16),
        grid=(b, s // qb),
        in_specs=in_specs,
        out_specs=pl.BlockSpec((None, qb, A_WIDTH), lambda bi, i: (bi, i, 0)),
        compiler_params=_params("parallel", "parallel"),
        name="attention",
    )(*([qkv] * (1 + 2 * n_kblocks)), bias)


def _split3(x):
    h1 = x.astype(BF16)
    r1 = x - h1.astype(F32)
    h2 = r1.astype(BF16)
    h3 = (r1 - h2.astype(F32)).astype(BF16)
    return h1, h2, h3


def _rwkv_body(*refs, has_vmix):
    if has_vmix:
        (pb_ref, vf_ref, mu_ref, w0_ref, wup_ref, a0_ref, aup_ref, gup_ref, kk_ref, ka_ref, rk_ref,
         lnw_ref, lnb_ref, v0_ref, vdn_ref, vup_ref, o_ref, state_ref, carry_ref) = refs
        vraw_ref = None
    else:
        (pb_ref, mu_ref, w0_ref, wup_ref, a0_ref, aup_ref, gup_ref, kk_ref, ka_ref, rk_ref,
         lnw_ref, lnb_ref, o_ref, vraw_ref, state_ref, carry_ref) = refs
    nb, c, ncol = pb_ref.shape
    n = B_HEAD_SIZE
    rows = nb * c

    @pl.when(pl.program_id(0) == 0)
    def _():
        state_ref[...] = jnp.zeros_like(state_ref)
        carry_ref[...] = jnp.zeros_like(carry_ref)

    p = pb_ref[...].reshape(rows, ncol)
    row = lax.broadcasted_iota(jnp.int32, (rows, 1), 0)
    prev = pltpu.roll(p, 1, axis=0)
    for b in range(nb):
        prev = jnp.where(row == b * c, carry_ref[b, 0:1, :], prev)
    for b in range(nb):
        carry_ref[b, 0:1, :] = p[(b + 1) * c - 1:(b + 1) * c, :]
    ps = p + (prev - p) * mu_ref[...]
    r = ps[:, 0:B_WIDTH]
    k = ps[:, B_WIDTH:2 * B_WIDTH]
    v = ps[:, 2 * B_WIDTH:3 * B_WIDTH]
    wad = ps[:, 3 * B_WIDTH:3 * B_WIDTH + W_LORA + A_LORA]
    gd = ps[:, 3 * B_WIDTH + W_LORA + A_LORA:]
    if has_vmix:
        mix = _sigmoid(v0_ref[...] + _mm(_mm(v, vdn_ref[...]), vup_ref[...]))
        v = v + (vf_ref[...].reshape(rows, B_WIDTH) - v) * mix
    else:
        vraw_ref[...] = v.reshape(nb, c, B_WIDTH)
    wlog = -DECAY_SCALE * _sigmoid(w0_ref[...] + _mm(jnp.tanh(wad), wup_ref[...]))
    a = _sigmoid(a0_ref[...] + _mm(wad, aup_ref[...]))
    g = _mm(_sigmoid(gd), gup_ref[...])
    kkf = k * kk_ref[...]
    kka = kkf * a
    kmod = k * (1.0 + (a - 1.0) * ka_ref[...])
    rkr = r * kmod * rk_ref[...]

    ti = lax.broadcasted_iota(jnp.int32, (c, 3 * c), 0)
    tj = lax.broadcasted_iota(jnp.int32, (c, 3 * c), 1) % c
    tri3 = jnp.where(tj <= ti, 1.0, 0.0).astype(BF16)
    lcum, lend = [], []
    for b in range(nb):
        lb = jnp.dot(tri3, jnp.concatenate(_split3(wlog[b * c:(b + 1) * c]), axis=0),
                     preferred_element_type=F32)
        lcum.append(lb)
        lend.append(jnp.broadcast_to(lb[c - 1:c, :], lb.shape))
    lcum = jnp.concatenate(lcum, axis=0)
    lend = jnp.concatenate(lend, axis=0)
    e_neg = jnp.exp(-lcum)
    e_end = jnp.exp(lend - lcum)
    w_end = jnp.exp(lend)

    pair = 2 * n
    npair = B_HEADS // 2
    lo = lax.broadcasted_iota(jnp.int32, (1, 1, pair), 2) < n

    def slabs(q):
        return jnp.stack([q[b * c:(b + 1) * c, p * pair:(p + 1) * pair]
                          for b in range(nb) for p in range(npair)], axis=0)

    def unslabs(q):
        return jnp.concatenate(
            [jnp.concatenate([q[b * npair + p] for p in range(npair)], axis=-1) for b in range(nb)], axis=0)

    def head_sum(q):
        s_lo = jnp.sum(jnp.where(lo, q, 0.0), axis=-1, keepdims=True)
        s_hi = jnp.sum(jnp.where(lo, 0.0, q), axis=-1, keepdims=True)
        return jnp.where(lo, s_lo, s_hi)

    def bd(q):
        return jnp.concatenate([jnp.where(lo, q, 0.0), jnp.where(lo, 0.0, q)], axis=1)

    def bmm(lhs, rhs):
        return jnp.einsum("gij,gjv->giv", lhs.astype(BF16), rhs.astype(BF16), preferred_element_type=F32)

    kk_s = slabs(kkf)
    rs = lax.rsqrt(jnp.maximum(head_sum(kk_s * kk_s), 1e-24))
    x = jnp.concatenate([slabs(-kkf * jnp.exp(lcum - wlog)) * rs, slabs(r * jnp.exp(lcum))], axis=1)
    bt = slabs(kka * e_neg) * rs
    kt = slabs(kmod * e_neg)
    bk_end = jnp.concatenate([slabs(kka * e_end) * rs, slabs(kmod * e_end)], axis=1)
    vh = slabs(v)
    s0 = state_ref[...]
    xb = x.astype(BF16)
    ybd = jnp.concatenate([bd(bt), bd(kt)], axis=1).astype(BF16)
    gm = jnp.einsum("gik,gjk->gij", xb, ybd, preferred_element_type=F32)
    xs = jnp.einsum("gik,gvk->giv", xb, s0.astype(BF16), preferred_element_type=F32)
    ri = lax.broadcasted_iota(jnp.int32, (1, c, 2 * c), 1)
    ci = lax.broadcasted_iota(jnp.int32, (1, c, 2 * c), 2) % c
    strict = ci < ri
    incl = ci <= ri
    m_ab = jnp.where(strict, gm[:, :c, :2 * c], 0.0)
    m_ak = jnp.where(strict, gm[:, :c, 2 * c:], 0.0)
    m_rb = jnp.where(incl, gm[:, c:, :2 * c], 0.0)
    m_rk = jnp.where(incl, gm[:, c:, 2 * c:], 0.0)
    bdv = bd(vh)
    u = xs[:, :c] + bmm(m_ak, bdv)
    pw = m_ab
    span = 1
    while True:
        u = u + bmm(pw, bd(u))
        span *= 2
        if span >= c:
            break
        pw = bmm(pw, bd(pw))
    o = xs[:, c:] + bmm(jnp.concatenate([m_rb, m_rk], axis=2), jnp.concatenate([bd(u), bdv], axis=1))
    w_end_s = jnp.stack([w_end[b * c:b * c + 1, p * pair:(p + 1) * pair]
                         for b in range(nb) for p in range(npair)], axis=0)
    uv = jnp.concatenate([u, vh], axis=1)
    upd = jnp.einsum("gjv,gjk->gvk", uv.astype(BF16), bk_end.astype(BF16), preferred_element_type=F32)
    same_head = (lax.broadcasted_iota(jnp.int32, (1, pair, pair), 1) < n) == lo
    state_ref[...] = s0 * w_end_s + jnp.where(same_head, upd, 0.0)
    mean = head_sum(o) * (1.0 / n)
    d = o - mean
    var = head_sum(d * d) * (1.0 / n)
    o = d * lax.rsqrt(var + GN_EPS)
    bonus = head_sum(slabs(rkr)) * vh
    out = (unslabs(o) * lnw_ref[...] + lnb_ref[...] + unslabs(bonus)) * g
    o_ref[...] = out.astype(BF16).reshape(nb, c, B_WIDTH)


def _rwkv(pb, v_first, prm, vmix):
    b, s, ncol = pb.shape
    c = min(RWKV_CHUNK, s)
    has_vmix = vmix is not None
    tok = lambda w: pl.BlockSpec((b, c, w), lambda i: (0, i, 0))
    vec = lambda a: a.reshape(1, -1).astype(F32)
    lw = W_LORA + A_LORA
    wup = jnp.zeros((lw, B_WIDTH), F32).at[:W_LORA].set(prm["w_up"]).astype(BF16)
    aup = jnp.zeros((lw, B_WIDTH), F32).at[W_LORA:].set(prm["a_up"]).astype(BF16)
    common = [vec(prm["mu"]), vec(prm["w0"]), wup, vec(prm["a0"]), aup, prm["g_up"].astype(BF16),
              vec(prm["k_k"]), vec(prm["k_a"]), vec(prm["r_k"]), vec(prm["ln_w"]), vec(prm["ln_b"])]
    if has_vmix:
        v0, v_down, v_up = vmix
        vdn = jnp.zeros((B_WIDTH, V7X_LANES), F32).at[:, :V_LORA].set(v_down).astype(BF16)
        vup = jnp.zeros((V7X_LANES, B_WIDTH), F32).at[:V_LORA].set(v_up).astype(BF16)
        args = [pb, v_first] + common + [vec(v0), vdn, vup]
        in_specs = [tok(ncol), tok(B_WIDTH)] + [_const_spec(a.shape) for a in args[2:]]
        out_shape = jax.ShapeDtypeStruct((b, s, B_WIDTH), BF16)
        out_specs = tok(B_WIDTH)
    else:
        args = [pb] + common
        in_specs = [tok(ncol)] + [_const_spec(a.shape) for a in args[1:]]
        out_shape = (jax.ShapeDtypeStruct((b, s, B_WIDTH), BF16), jax.ShapeDtypeStruct((b, s, B_WIDTH), F32))
        out_specs = (tok(B_WIDTH), tok(B_WIDTH))
    return pl.pallas_call(
        functools.partial(_rwkv_body, has_vmix=has_vmix),
        out_shape=out_shape,
        grid=(s // c,),
        in_specs=in_specs,
        out_specs=out_specs,
        scratch_shapes=[pltpu.VMEM((b * B_HEADS // 2, 2 * B_HEAD_SIZE, 2 * B_HEAD_SIZE), F32),
                        pltpu.VMEM((b, V7X_SUBLANES, ncol), F32)],
        compiler_params=_params("arbitrary"),
        name="rwkv_mix" if has_vmix else "rwkv",
    )(*args)


def _gelu_tanh(x):
    return 0.5 * x * (1.0 + jnp.tanh(np.sqrt(2.0 / np.pi).astype(np.float32) * (x + 0.044715 * (x * x * x))))


def _rglru_tile(bi, x_ref, g_ref, win_ref, cw_ref, cb_ref, wax_ref, ba_ref, bx_ref, lam_ref, wout_ref,
                o_ref, xprev_ref, hprev_ref):
    tr = x_ref.shape[1]
    sub = V7X_SUBLANES
    nrow = tr // sub
    ntail = CONV_W - 1
    x = x_ref[bi]
    h = _rms(x, g_ref[...]).astype(BF16)
    i0 = lax.broadcasted_iota(jnp.int32, (tr, tr), 0)
    i1 = lax.broadcasted_iota(jnp.int32, (tr, tr), 1)
    perm = jnp.where(i1 == (i0 % sub) * nrow + i0 // sub, 1.0, 0.0).astype(BF16)
    unperm = jnp.where(i0 == (i1 % sub) * nrow + i1 // sub, 1.0, 0.0).astype(BF16)
    hp = jnp.dot(perm, h, preferred_element_type=F32).astype(BF16)
    gate = _gelu_tanh(jnp.dot(hp, win_ref[:, :D_RNN], preferred_element_type=F32))
    xb3 = jnp.dot(hp, win_ref[:, D_RNN:], preferred_element_type=F32).reshape(nrow, sub, D_RNN)
    seg = lax.broadcasted_iota(jnp.int32, (1, sub, 1), 1)
    prev3 = xprev_ref[bi]
    tails = [jnp.where(seg == 0, pltpu.roll(prev3[q:q + 1], 1, axis=1),
                       pltpu.roll(xb3[nrow - ntail + q:nrow - ntail + q + 1], 1, axis=1)) for q in range(ntail)]
    xprev_ref[bi] = xb3[nrow - ntail:]
    xc3 = xb3 * cw_ref[CONV_W - 1:CONV_W, :] + cb_ref[...]
    for sft in range(1, CONV_W):
        shifted = jnp.concatenate(tails[ntail - sft:] + [xb3[:nrow - sft]], axis=0)
        xc3 = xc3 + shifted * cw_ref[CONV_W - 1 - sft:CONV_W - sft, :]
    xc = xc3.reshape(tr, D_RNN)
    xcb = xc.astype(BF16)
    pw = 2 * C_BLOCK_W
    ra, rx = [], []
    for q in range(C_BLOCKS // 2):
        t = jnp.dot(xcb[:, q * pw:(q + 1) * pw], wax_ref[q], preferred_element_type=F32)
        ra.append(t[:, :pw])
        rx.append(t[:, pw:])
    rgate = _sigmoid(jnp.concatenate(ra, axis=-1) + ba_ref[...])
    igate = _sigmoid(jnp.concatenate(rx, axis=-1) + bx_ref[...])
    z = -lam_ref[...]
    softplus = jnp.maximum(z, 0.0) + jnp.log(1.0 + jnp.exp(-jnp.abs(z)))
    a = jnp.exp(-RG_C * rgate * softplus)
    mult = jnp.sqrt(jnp.maximum(1.0 - a * a, 0.0))
    bb = mult * igate * xc
    a3 = a.reshape(nrow, sub, D_RNN)
    b3 = bb.reshape(nrow, sub, D_RNN)
    acs, bcs = [a3[0]], [b3[0]]
    for p in range(1, nrow):
        bcs.append(a3[p] * bcs[-1] + b3[p])
        acs.append(a3[p] * acs[-1])
    ga, gb = acs[-1], bcs[-1]
    srow = lax.broadcasted_iota(jnp.int32, (sub, 1), 0)
    sft = 1
    while sft < sub:
        keep = srow >= sft
        a_sh = jnp.where(keep, pltpu.roll(ga, sft, axis=0), 1.0)
        b_sh = jnp.where(keep, pltpu.roll(gb, sft, axis=0), 0.0)
        gb = ga * b_sh + gb
        ga = ga * a_sh
        sft *= 2
    hprev = hprev_ref[bi, 0:1, :]
    hend = ga * hprev + gb
    hin = jnp.where(srow == 0, hprev, pltpu.roll(hend, 1, axis=0))
    hprev_ref[bi, 0:1, :] = hend[sub - 1:sub, :]
    hs = jnp.stack([acs[p] * hin + bcs[p] for p in range(nrow)], axis=0).reshape(tr, D_RNN)
    yp = (hs * gate).astype(BF16)
    y = jnp.dot(unperm, yp, preferred_element_type=F32).astype(BF16)
    o_ref[bi] = x + jnp.dot(y, wout_ref[...], preferred_element_type=F32)


def _rglru_body(x_ref, *rest):
    xprev_ref, hprev_ref = rest[-2:]

    @pl.when(pl.program_id(1) == 0)
    def _():
        xprev_ref[...] = jnp.zeros_like(xprev_ref)
        hprev_ref[...] = jnp.zeros_like(hprev_ref)

    for bi in range(x_ref.shape[0]):
        _rglru_tile(bi, x_ref, *rest)


def _rglru(x, g, w_in, conv_w, conv_b, wa, ba, wx, bx, lam, w_out, lead):
    b, s, d = x.shape
    tr = min(RGLRU_ROWS, s)
    pw = 2 * C_BLOCK_W
    wax = jnp.zeros((C_BLOCKS // 2, pw, 2 * pw), F32)
    for half in range(2):
        sl = slice(half * C_BLOCK_W, (half + 1) * C_BLOCK_W)
        wax = wax.at[:, sl, sl].set(wa[half::2])
        wax = wax.at[:, sl, pw + half * C_BLOCK_W:pw + (half + 1) * C_BLOCK_W].set(wx[half::2])
    wax = wax.astype(BF16)
    vec = lambda a: a.reshape(1, -1).astype(F32)
    args = [x, vec(g), w_in, conv_w.astype(F32), vec(conv_b), wax, vec(ba), vec(bx), vec(lam), w_out]
    nb = min(RGLRU_BATCH, b)
    tok = pl.BlockSpec((nb, tr, d), lambda bi, i: (bi, i, 0))
    return pl.pallas_call(
        _rglru_body,
        out_shape=jax.ShapeDtypeStruct((b, s, d), F32),
        grid=(b // nb, s // tr),
        in_specs=([tok, _const_spec(args[1].shape), _pick_spec(w_in.shape, lead)]
                  + [_const_spec(a.shape) for a in args[3:-1]] + [_pick_spec(w_out.shape, lead)]),
        out_specs=tok,
        scratch_shapes=[pltpu.VMEM((nb, CONV_W - 1, V7X_SUBLANES, D_RNN), F32),
                        pltpu.VMEM((nb, V7X_SUBLANES, D_RNN), F32)],
        compiler_params=_params("parallel", "arbitrary"),
        name="rglru",
    )(*args)


def kernel(x, norm_ffn, ffn_w_gate, ffn_w_up, ffn_w_down, norm_mix, ab_w_in, ab_w_out, att_rel_bias, rwkv_mu, rwkv_w0, rwkv_w_up, rwkv_a0, rwkv_a_up, rwkv_g_up, rwkv_k_k, rwkv_k_a, rwkv_r_k, rwkv_ln_w, rwkv_ln_b, rwkv_v0, rwkv_v_down, rwkv_v_up, c_w_in, c_conv_w, c_conv_b, c_wa, c_ba, c_wx, c_bx, c_lambda, c_w_out, norm_final):
    b, s, d = x.shape
    t = b * s
    depth = norm_mix.shape[0]
    wg, wu, wd = ffn_w_gate.astype(BF16), ffn_w_up.astype(BF16), ffn_w_down.astype(BF16)
    w_in, w_out = ab_w_in.astype(BF16), ab_w_out.astype(BF16)
    cw_in, cw_out = c_w_in.astype(BF16), c_w_out.astype(BF16)
    gfin = norm_final.reshape(1, d).astype(F32)
    x = x.astype(F32).reshape(t, d)
    v_first = None
    for l in range(depth):
        x = _ffn(x, norm_ffn[l, 0].reshape(1, d), wg, wu, wd, (l, 0), gfin, False)
        mixer = None
        if l % 2 == 0:
            i = l // 2
            qkv, pb = _ab_in(x, norm_mix[l].reshape(1, d), w_in, (i,))
            oa = _attention(qkv.reshape(b, s, -1), att_rel_bias[i])
            prm = dict(mu=rwkv_mu[i], w0=rwkv_w0[i], w_up=rwkv_w_up[i], a0=rwkv_a0[i], a_up=rwkv_a_up[i],
                       g_up=rwkv_g_up[i], k_k=rwkv_k_k[i], k_a=rwkv_k_a[i], r_k=rwkv_r_k[i],
                       ln_w=rwkv_ln_w[i], ln_b=rwkv_ln_b[i])
            if i == 0:
                ob, v_first = _rwkv(pb.reshape(b, s, -1), None, prm, None)
            else:
                ob = _rwkv(pb.reshape(b, s, -1), v_first, prm,
                           (rwkv_v0[i - 1], rwkv_v_down[i - 1], rwkv_v_up[i - 1]))
            mixer = (oa.reshape(t, -1), ob.reshape(t, -1), w_out, (i,))
        else:
            j = l // 2
            x = _rglru(x.reshape(b, s, d), norm_mix[l], cw_in, c_conv_w[j], c_conv_b[j], c_wa[j], c_ba[j],
                       c_wx[j], c_bx[j], c_lambda[j], cw_out, (j,)).reshape(t, d)
        x = _ffn(x, norm_ffn[l, 1].reshape(1, d), wg, wu, wd, (l, 1), gfin, l == depth - 1, mixer)
    return x.reshape(b, s, d)
```

```python
import functools

import jax
import jax.numpy as jnp
import numpy as np
from jax import lax
from jax.experimental import pallas as pl
from jax.experimental.pallas import tpu as pltpu

F32 = jnp.float32
BF16 = jnp.bfloat16

D_MODEL = 1024
DEPTH = 4
CHUNK = 64
LEFT_CHUNKS = 8
A_HEADS = 8
A_HEAD_DIM = 64
A_WIDTH = A_HEADS * A_HEAD_DIM
REL_CLIP = 128
B_HEADS = 8
B_HEAD_SIZE = 64
B_WIDTH = B_HEADS * B_HEAD_SIZE
W_LORA = 64
A_LORA = 64
G_LORA = 128
V_LORA = 32
DECAY_SCALE = 0.606531
GN_EPS = 64e-5
B_COLS = 3 * B_WIDTH + W_LORA + A_LORA + G_LORA
D_RNN = 1280
C_BLOCKS = 10
C_BLOCK_W = D_RNN // C_BLOCKS
CONV_W = 4
RG_C = 8.0
D_FF = 2816
NORM_EPS = 1e-6
NEG_INF = -1e30

V7X_LANES = 128
V7X_SUBLANES = 8
V7X_MXU_DIM = 256
V7X_VMEM_BYTES = 64 * 1024 * 1024

FFN_ROWS = 1024
FFN_CHUNKS = (0, 768, 1536, 2304, 2816)
FFN_STAGE_IN_ROWS = 64
FFN_STAGE_OUT_ROWS = 176
PROJ_ROWS = 512
ATT_ROWS = 256
ATT_PREV = LEFT_CHUNKS * CHUNK
RWKV_CHUNK = 64
RGLRU_ROWS = 256
RGLRU_BATCH = 1
VMEM_LIMIT = 56 * 1024 * 1024


def _params(*sem):
    return pltpu.CompilerParams(dimension_semantics=sem, vmem_limit_bytes=VMEM_LIMIT)


def _const_spec(shape):
    nd = len(shape)
    return pl.BlockSpec(shape, lambda *_: (0,) * nd, pipeline_mode=pl.Buffered(1))


def _pick_spec(shape, lead):
    tail = tuple(shape[len(lead):])
    index = tuple(lead) + (0,) * len(tail)
    return pl.BlockSpec((None,) * len(lead) + tail, lambda *_: index, pipeline_mode=pl.Buffered(1))


def _mm(a, b):
    return jnp.dot(a.astype(BF16), b.astype(BF16), preferred_element_type=F32)


def _mm_nt(a, b):
    return lax.dot_general(a.astype(BF16), b.astype(BF16), (((1,), (1,)), ((), ())),
                           preferred_element_type=F32)


def _mm_tn(a, b):
    return lax.dot_general(a.astype(BF16), b.astype(BF16), (((0,), (0,)), ((), ())),
                           preferred_element_type=F32)


def _rms(x, g):
    ms = jnp.mean(x * x, axis=-1, keepdims=True)
    return x * lax.rsqrt(ms + NORM_EPS) * g


def _sigmoid(x):
    return 0.5 * jnp.tanh(0.5 * x) + 0.5


def _stage_copy(src, k, rows, buf, sem, slot):
    return pltpu.make_async_copy(src.at[pl.ds(k * rows, rows)], buf.at[slot], sem.at[slot])


def _stage_weight(src, dst, buf, sem):
    rows = buf.shape[1]
    n = src.shape[0] // rows
    _stage_copy(src, 0, rows, buf, sem, 0).start()
    for k in range(n):
        if k + 1 < n:
            _stage_copy(src, k + 1, rows, buf, sem, (k + 1) % 2).start()
        _stage_copy(src, k, rows, buf, sem, k % 2).wait()
        dst[pl.ds(k * rows, rows), :] = buf[k % 2].astype(BF16)


def _ffn_body(*refs, final_norm, mixer_out, lead):
    (wg_hbm, wu_hbm, wd_hbm, gf_ref, o_ref, wg_ref, wu_ref, wd_ref, stage_in, stage_out, sem_in, sem_out) = refs[-12:]

    @pl.when(pl.program_id(0) == 0)
    def _():
        _stage_weight(wg_hbm.at[lead[0], lead[1]], wg_ref, stage_in, sem_in)
        _stage_weight(wu_hbm.at[lead[0], lead[1]], wu_ref, stage_in, sem_in)
        _stage_weight(wd_hbm.at[lead[0], lead[1]], wd_ref, stage_out, sem_out)

    if mixer_out:
        oa_ref, ob_ref, wo_ref, x_ref, g_ref = refs[:5]
        x = x_ref[...] + jnp.dot(oa_ref[...], wo_ref[:A_WIDTH, :], preferred_element_type=F32)
        x = x + jnp.dot(ob_ref[...], wo_ref[A_WIDTH:, :], preferred_element_type=F32)
    else:
        x_ref, g_ref = refs[:2]
        x = x_ref[...]
    h = _rms(x, g_ref[...]).astype(BF16)
    acc = None
    for lo, hi in zip(FFN_CHUNKS[:-1], FFN_CHUNKS[1:]):
        gt = jnp.dot(h, wg_ref[:, lo:hi], preferred_element_type=F32)
        up = jnp.dot(h, wu_ref[:, lo:hi], preferred_element_type=F32)
        act = (gt * _sigmoid(gt) * up).astype(BF16)
        d = jnp.dot(act, wd_ref[lo:hi, :], preferred_element_type=F32)
        acc = d if acc is None else acc + d
    y = x + 0.5 * acc
    if final_norm:
        y = _rms(y, gf_ref[...])
    o_ref[...] = y


def _ffn(x, g, wg, wu, wd, lead, gf, final_norm, mixer=None):
    t, d = x.shape
    f = wg.shape[-1]
    tm = min(FFN_ROWS, t)
    row = pl.BlockSpec((tm, d), lambda i: (i, 0))
    hbm = pl.BlockSpec(memory_space=pl.ANY)
    args = [x, g, wg, wu, wd, gf]
    in_specs = [row, _const_spec((1, d)), hbm, hbm, hbm, _const_spec((1, d))]
    if mixer is not None:
        oa, ob, wo, lead_out = mixer
        args = [oa, ob, wo] + args
        in_specs = [pl.BlockSpec((tm, oa.shape[1]), lambda i: (i, 0)), pl.BlockSpec((tm, ob.shape[1]), lambda i: (i, 0)),
                    _pick_spec(wo.shape, lead_out)] + in_specs
    assert d % FFN_STAGE_IN_ROWS == 0 and f % FFN_STAGE_OUT_ROWS == 0
    return pl.pallas_call(
        functools.partial(_ffn_body, final_norm=final_norm, mixer_out=mixer is not None, lead=lead),
        out_shape=jax.ShapeDtypeStruct((t, d), F32),
        grid=(t // tm,),
        in_specs=in_specs,
        out_specs=row,
        scratch_shapes=[pltpu.VMEM((d, f), BF16), pltpu.VMEM((d, f), BF16), pltpu.VMEM((f, d), BF16),
                        pltpu.VMEM((2, FFN_STAGE_IN_ROWS, f), F32), pltpu.VMEM((2, FFN_STAGE_OUT_ROWS, d), F32),
                        pltpu.SemaphoreType.DMA((2,)), pltpu.SemaphoreType.DMA((2,))],
        compiler_params=_params("arbitrary"),
        name="ffn_final" if final_norm else ("ffn_mix" if mixer is not None else "ffn"),
    )(*args)


def _ab_in_body(x_ref, g_ref, w_ref, qkv_ref, pb_ref):
    h = _rms(x_ref[...], g_ref[...]).astype(BF16)
    na = qkv_ref.shape[-1]
    qkv_ref[...] = jnp.dot(h, w_ref[:, :na], preferred_element_type=F32).astype(BF16)
    pb_ref[...] = jnp.dot(h, w_ref[:, na:], preferred_element_type=F32)


def _ab_in(x, g, w, lead):
    t, d = x.shape
    n = w.shape[-1]
    na = 3 * A_WIDTH
    tm = min(PROJ_ROWS, t)
    return pl.pallas_call(
        _ab_in_body,
        out_shape=(jax.ShapeDtypeStruct((t, na), BF16), jax.ShapeDtypeStruct((t, n - na), F32)),
        grid=(t // tm,),
        in_specs=[pl.BlockSpec((tm, d), lambda i: (i, 0)), _const_spec((1, d)), _pick_spec(w.shape, lead)],
        out_specs=(pl.BlockSpec((tm, na), lambda i: (i, 0)), pl.BlockSpec((tm, n - na), lambda i: (i, 0))),
        compiler_params=_params("parallel"),
        name="ab_in",
    )(x, g, w)


def _attn_body(*refs, n_kblocks):
    q_ref = refs[0]
    k_refs = refs[1:1 + n_kblocks]
    v_refs = refs[1 + n_kblocks:1 + 2 * n_kblocks]
    bias_ref = refs[1 + 2 * n_kblocks]
    o_ref = refs[2 + 2 * n_kblocks]
    qb = q_ref.shape[0]
    i = pl.program_id(1)
    n_prev = n_kblocks - 1
    pens = [jnp.where(i >= n_prev - j, 0.0, NEG_INF).astype(F32) for j in range(n_prev)] + [None]
    lane = lax.broadcasted_iota(jnp.int32, (1, 2 * A_HEAD_DIM), 1)
    first = lane < A_HEAD_DIM
    scale = A_HEAD_DIM ** -0.5
    for pr in range(A_HEADS // 2):
        ls = slice(2 * A_HEAD_DIM * pr, 2 * A_HEAD_DIM * (pr + 1))
        q_pair = q_ref[:, ls]
        ks = [r[:, ls] for r in k_refs]
        vs = [r[:, ls] for r in v_refs]
        outs = []
        for hh in range(2):
            head = 2 * pr + hh
            sel = first if hh == 0 else jnp.logical_not(first)
            qm = jnp.where(sel, scale, 0.0).astype(BF16)
            qh = q_pair * qm
            s = []
            for j in range(n_kblocks):
                sj = lax.dot_general(qh, ks[j], (((1,), (1,)), ((), ())), preferred_element_type=F32)
                sj = sj + bias_ref[head, :, j * qb:(j + 1) * qb]
                if pens[j] is not None:
                    sj = sj + pens[j]
                s.append(sj)
            m = s[0].max(axis=-1, keepdims=True)
            for sj in s[1:]:
                m = jnp.maximum(m, sj.max(axis=-1, keepdims=True))
            l = None
            pv = None
            for j in range(n_kblocks):
                p = jnp.exp(s[j] - m)
                lj = p.sum(axis=-1, keepdims=True)
                pvj = jnp.dot(p.astype(BF16), vs[j], preferred_element_type=F32)
                l = lj if l is None else l + lj
                pv = pvj if pv is None else pv + pvj
            outs.append(pv / l)
        o_ref[:, ls] = jnp.where(first, outs[0], outs[1]).astype(BF16)


def _attn_bias_tile(rel_bias, qb):
    nh = rel_bias.shape[0]
    w = ATT_PREV + qb
    period = w + qb
    near = 2 * REL_CLIP + 1
    assert ATT_PREV >= REL_CLIP and qb >= REL_CLIP
    far_past = jnp.broadcast_to(rel_bias[:, near - 1:], (nh, ATT_PREV - REL_CLIP))
    far_future = jnp.broadcast_to(rel_bias[:, :1], (nh, qb - REL_CLIP))
    wrapped = jnp.broadcast_to(rel_bias[:, near - 1:], (nh, qb - 1))
    t = jnp.concatenate([far_past, rel_bias[:, ::-1], far_future, wrapped], axis=1).astype(F32)
    assert t.shape[1] == period
    toe = jnp.broadcast_to(t[:, None, :], (nh, qb, period)).reshape(nh, qb * period)
    toe = toe[:, :qb * (period - 1)].reshape(nh, qb, period - 1)[:, :, :w]
    r = np.arange(qb)[:, None]
    c = np.arange(w)[None, :]
    band = (c // CHUNK >= r // CHUNK) & (c // CHUNK <= r // CHUNK + LEFT_CHUNKS)
    return jnp.where(band[None], toe, NEG_INF)


def _attention(qkv, rel_bias):
    b, s, _ = qkv.shape
    qb = min(ATT_ROWS, s)
    assert ATT_PREV % qb == 0 and s % qb == 0
    n_prev = ATT_PREV // qb
    n_kblocks = n_prev + 1
    bias = _attn_bias_tile(rel_bias, qb)

    def blk(col, back):
        return pl.BlockSpec((None, qb, A_WIDTH), lambda bi, i: (bi, jnp.maximum(i - back, 0), col))

    in_specs = ([blk(0, 0)] + [blk(1, n_prev - j) for j in range(n_kblocks)]
                + [blk(2, n_prev - j) for j in range(n_kblocks)] + [_const_spec(bias.shape)])
    return pl.pallas_call(
        functools.partial(_attn_body, n_kblocks=n_kblocks),
        out_shape=jax.ShapeDtypeStruct((b, s, A_WIDTH), BF16),
        grid=(b, s // qb),
        in_specs=in_specs,
        out_specs=pl.BlockSpec((None, qb, A_WIDTH), lambda bi, i: (bi, i, 0)),
        compiler_params=_params("parallel", "parallel"),
        name="attention",
    )(*([qkv] * (1 + 2 * n_kblocks)), bias)


def _split3(x):
    h1 = x.astype(BF16)
    r1 = x - h1.astype(F32)
    h2 = r1.astype(BF16)
    h3 = (r1 - h2.astype(F32)).astype(BF16)
    return h1, h2, h3


def _rwkv_body(*refs, has_vmix):
    if has_vmix:
        (pb_ref, vf_ref, mu_ref, w0_ref, wup_ref, a0_ref, aup_ref, gup_ref, kk_ref, ka_ref, rk_ref,
         lnw_ref, lnb_ref, v0_ref, vdn_ref, vup_ref, o_ref, state_ref, carry_ref) = refs
        vraw_ref = None
    else:
        (pb_ref, mu_ref, w0_ref, wup_ref, a0_ref, aup_ref, gup_ref, kk_ref, ka_ref, rk_ref,
         lnw_ref, lnb_ref, o_ref, vraw_ref, state_ref, carry_ref) = refs
    nb, c, ncol = pb_ref.shape
    n = B_HEAD_SIZE
    rows = nb * c

    @pl.when(pl.program_id(0) == 0)
    def _():
        state_ref[...] = jnp.zeros_like(state_ref)
        carry_ref[...] = jnp.zeros_like(carry_ref)

    p = pb_ref[...].reshape(rows, ncol)
    row = lax.broadcasted_iota(jnp.int32, (rows, 1), 0)
    prev = pltpu.roll(p, 1, axis=0)
    for b in range(nb):
        prev = jnp.where(row == b * c, carry_ref[b, 0:1, :], prev)
    for b in range(nb):
        carry_ref[b, 0:1, :] = p[(b + 1) * c - 1:(b + 1) * c, :]
    ps = p + (prev - p) * mu_ref[...]
    r = ps[:, 0:B_WIDTH]
    k = ps[:, B_WIDTH:2 * B_WIDTH]
    v = ps[:, 2 * B_WIDTH:3 * B_WIDTH]
    wad = ps[:, 3 * B_WIDTH:3 * B_WIDTH + W_LORA + A_LORA]
    gd = ps[:, 3 * B_WIDTH + W_LORA + A_LORA:]
    if has_vmix:
        mix = _sigmoid(v0_ref[...] + _mm(_mm(v, vdn_ref[...]), vup_ref[...]))
        v = v + (vf_ref[...].reshape(rows, B_WIDTH) - v) * mix
    else:
        vraw_ref[...] = v.reshape(nb, c, B_WIDTH)
    wlog = -DECAY_SCALE * _sigmoid(w0_ref[...] + _mm(jnp.tanh(wad), wup_ref[...]))
    a = _sigmoid(a0_ref[...] + _mm(wad, aup_ref[...]))
    g = _mm(_sigmoid(gd), gup_ref[...])
    kkf = k * kk_ref[...]
    kka = kkf * a
    kmod = k * (1.0 + (a - 1.0) * ka_ref[...])
    rkr = r * kmod * rk_ref[...]

    ti = lax.broadcasted_iota(jnp.int32, (c, 3 * c), 0)
    tj = lax.broadcasted_iota(jnp.int32, (c, 3 * c), 1) % c
    tri3 = jnp.where(tj <= ti, 1.0, 0.0).astype(BF16)
    lcum, lend = [], []
    for b in range(nb):
        lb = jnp.dot(tri3, jnp.concatenate(_split3(wlog[b * c:(b + 1) * c]), axis=0),
                     preferred_element_type=F32)
        lcum.append(lb)
        lend.append(jnp.broadcast_to(lb[c - 1:c, :], lb.shape))
    lcum = jnp.concatenate(lcum, axis=0)
    lend = jnp.concatenate(lend, axis=0)
    e_neg = jnp.exp(-lcum)
    e_end = jnp.exp(lend - lcum)
    w_end = jnp.exp(lend)

    pair = 2 * n
    npair = B_HEADS // 2
    lo = lax.broadcasted_iota(jnp.int32, (1, 1, pair), 2) < n

    def slabs(q):
        return jnp.stack([q[b * c:(b + 1) * c, p * pair:(p + 1) * pair]
                          for b in range(nb) for p in range(npair)], axis=0)

    def unslabs(q):
        return jnp.concatenate(
            [jnp.concatenate([q[b * npair + p] for p in range(npair)], axis=-1) for b in range(nb)], axis=0)

    def head_sum(q):
        s_lo = jnp.sum(jnp.where(lo, q, 0.0), axis=-1, keepdims=True)
        s_hi = jnp.sum(jnp.where(lo, 0.0, q), axis=-1, keepdims=True)
        return jnp.where(lo, s_lo, s_hi)

    def bd(q):
        return jnp.concatenate([jnp.where(lo, q, 0.0), jnp.where(lo, 0.0, q)], axis=1)

    def bmm(lhs, rhs):
        return jnp.einsum("gij,gjv->giv", lhs.astype(BF16), rhs.astype(BF16), preferred_element_type=F32)

    kk_s = slabs(kkf)
    rs = lax.rsqrt(jnp.maximum(head_sum(kk_s * kk_s), 1e-24))
    x = jnp.concatenate([slabs(-kkf * jnp.exp(lcum - wlog)) * rs, slabs(r * jnp.exp(lcum))], axis=1)
    bt = slabs(kka * e_neg) * rs
    kt = slabs(kmod * e_neg)
    bk_end = jnp.concatenate([slabs(kka * e_end) * rs, slabs(kmod * e_end)], axis=1)
    vh = slabs(v)
    s0 = state_ref[...]
    xb = x.astype(BF16)
    ybd = jnp.concatenate([bd(bt), bd(kt)], axis=1).astype(BF16)
    gm = jnp.einsum("gik,gjk->gij", xb, ybd, preferred_element_type=F32)
    xs = jnp.einsum("gik,gvk->giv", xb, s0.astype(BF16), preferred_element_type=F32)
    ri = lax.broadcasted_iota(jnp.int32, (1, c, 2 * c), 1)
    ci = lax.broadcasted_iota(jnp.int32, (1, c, 2 * c), 2) % c
    strict = ci < ri
    incl = ci <= ri
    m_ab = jnp.where(strict, gm[:, :c, :2 * c], 0.0)
    m_ak = jnp.where(strict, gm[:, :c, 2 * c:], 0.0)
    m_rb = jnp.where(incl, gm[:, c:, :2 * c], 0.0)
    m_rk = jnp.where(incl, gm[:, c:, 2 * c:], 0.0)
    bdv = bd(vh)
    u = xs[:, :c] + bmm(m_ak, bdv)
    pw = m_ab
    span = 1
    while True:
        u = u + bmm(pw, bd(u))
        span *= 2
        if span >= c:
            break
        pw = bmm(pw, bd(pw))
    o = xs[:, c:] + bmm(jnp.concatenate([m_rb, m_rk], axis=2), jnp.concatenate([bd(u), bdv], axis=1))
    w_end_s = jnp.stack([w_end[b * c:b * c + 1, p * pair:(p + 1) * pair]
                         for b in range(nb) for p in range(npair)], axis=0)
    uv = jnp.concatenate([u, vh], axis=1)
    upd = jnp.einsum("gjv,gjk->gvk", uv.astype(BF16), bk_end.astype(BF16), preferred_element_type=F32)
    same_head = (lax.broadcasted_iota(jnp.int32, (1, pair, pair), 1) < n) == lo
    state_ref[...] = s0 * w_end_s + jnp.where(same_head, upd, 0.0)
    mean = head_sum(o) * (1.0 / n)
    d = o - mean
    var = head_sum(d * d) * (1.0 / n)
    o = d * lax.rsqrt(var + GN_EPS)
    bonus = head_sum(slabs(rkr)) * vh
    out = (unslabs(o) * lnw_ref[...] + lnb_ref[...] + unslabs(bonus)) * g
    o_ref[...] = out.astype(BF16).reshape(nb, c, B_WIDTH)


def _rwkv(pb, v_first, prm, vmix):
    b, s, ncol = pb.shape
    c = min(RWKV_CHUNK, s)
    has_vmix = vmix is not None
    tok = lambda w: pl.BlockSpec((b, c, w), lambda i: (0, i, 0))
    vec = lambda a: a.reshape(1, -1).astype(F32)
    lw = W_LORA + A_LORA
    wup = jnp.zeros((lw, B_WIDTH), F32).at[:W_LORA].set(prm["w_up"]).astype(BF16)
    aup = jnp.zeros((lw, B_WIDTH), F32).at[W_LORA:].set(prm["a_up"]).astype(BF16)
    common = [vec(prm["mu"]), vec(prm["w0"]), wup, vec(prm["a0"]), aup, prm["g_up"].astype(BF16),
              vec(prm["k_k"]), vec(prm["k_a"]), vec(prm["r_k"]), vec(prm["ln_w"]), vec(prm["ln_b"])]
    if has_vmix:
        v0, v_down, v_up = vmix
        vdn = jnp.zeros((B_WIDTH, V7X_LANES), F32).at[:, :V_LORA].set(v_down).astype(BF16)
        vup = jnp.zeros((V7X_LANES, B_WIDTH), F32).at[:V_LORA].set(v_up).astype(BF16)
        args = [pb, v_first] + common + [vec(v0), vdn, vup]
        in_specs = [tok(ncol), tok(B_WIDTH)] + [_const_spec(a.shape) for a in args[2:]]
        out_shape = jax.ShapeDtypeStruct((b, s, B_WIDTH), BF16)
        out_specs = tok(B_WIDTH)
    else:
        args = [pb] + common
        in_specs = [tok(ncol)] + [_const_spec(a.shape) for a in args[1:]]
        out_shape = (jax.ShapeDtypeStruct((b, s, B_WIDTH), BF16), jax.ShapeDtypeStruct((b, s, B_WIDTH), F32))
        out_specs = (tok(B_WIDTH), tok(B_WIDTH))
    return pl.pallas_call(
        functools.partial(_rwkv_body, has_vmix=has_vmix),
        out_shape=out_shape,
        grid=(s // c,),
        in_specs=in_specs,
        out_specs=out_specs,
        scratch_shapes=[pltpu.VMEM((b * B_HEADS // 2, 2 * B_HEAD_SIZE, 2 * B_HEAD_SIZE), F32),
                        pltpu.VMEM((b, V7X_SUBLANES, ncol), F32)],
        compiler_params=_params("arbitrary"),
        name="rwkv_mix" if has_vmix else "rwkv",
    )(*args)


def _gelu_tanh(x):
    return 0.5 * x * (1.0 + jnp.tanh(np.sqrt(2.0 / np.pi).astype(np.float32) * (x + 0.044715 * (x * x * x))))


def _rglru_tile(bi, x_ref, g_ref, win_ref, cw_ref, cb_ref, wax_ref, ba_ref, bx_ref, lam_ref, wout_ref,
                o_ref, xprev_ref, hprev_ref):
    tr = x_ref.shape[1]
    sub = V7X_SUBLANES
    nrow = tr // sub
    ntail = CONV_W - 1
    x = x_ref[bi]
    h = _rms(x, g_ref[...]).astype(BF16)
    i0 = lax.broadcasted_iota(jnp.int32, (tr, tr), 0)
    i1 = lax.broadcasted_iota(jnp.int32, (tr, tr), 1)
    perm = jnp.where(i1 == (i0 % sub) * nrow + i0 // sub, 1.0, 0.0).astype(BF16)
    unperm = jnp.where(i0 == (i1 % sub) * nrow + i1 // sub, 1.0, 0.0).astype(BF16)
    hp = jnp.dot(perm, h, preferred_element_type=F32).astype(BF16)
    gate = _gelu_tanh(jnp.dot(hp, win_ref[:, :D_RNN], preferred_element_type=F32))
    xb3 = jnp.dot(hp, win_ref[:, D_RNN:], preferred_element_type=F32).reshape(nrow, sub, D_RNN)
    seg = lax.broadcasted_iota(jnp.int32, (1, sub, 1), 1)
    prev3 = xprev_ref[bi]
    tails = [jnp.where(seg == 0, pltpu.roll(prev3[q:q + 1], 1, axis=1),
                       pltpu.roll(xb3[nrow - ntail + q:nrow - ntail + q + 1], 1, axis=1)) for q in range(ntail)]
    xprev_ref[bi] = xb3[nrow - ntail:]
    xc3 = xb3 * cw_ref[CONV_W - 1:CONV_W, :] + cb_ref[...]
    for sft in range(1, CONV_W):
        shifted = jnp.concatenate(tails[ntail - sft:] + [xb3[:nrow - sft]], axis=0)
        xc3 = xc3 + shifted * cw_ref[CONV_W - 1 - sft:CONV_W - sft, :]
    xc = xc3.reshape(tr, D_RNN)
    xcb = xc.astype(BF16)
    pw = 2 * C_BLOCK_W
    ra, rx = [], []
    for q in range(C_BLOCKS // 2):
        t = jnp.dot(xcb[:, q * pw:(q + 1) * pw], wax_ref[q], preferred_element_type=F32)
        ra.append(t[:, :pw])
        rx.append(t[:, pw:])
    rgate = _sigmoid(jnp.concatenate(ra, axis=-1) + ba_ref[...])
    igate = _sigmoid(jnp.concatenate(rx, axis=-1) + bx_ref[...])
    z = -lam_ref[...]
    softplus = jnp.maximum(z, 0.0) + jnp.log(1.0 + jnp.exp(-jnp.abs(z)))
    a = jnp.exp(-RG_C * rgate * softplus)
    mult = jnp.sqrt(jnp.maximum(1.0 - a * a, 0.0))
    bb = mult * igate * xc
    a3 = a.reshape(nrow, sub, D_RNN)
    b3 = bb.reshape(nrow, sub, D_RNN)
    acs, bcs = [a3[0]], [b3[0]]
    for p in range(1, nrow):
        bcs.append(a3[p] * bcs[-1] + b3[p])
        acs.append(a3[p] * acs[-1])
    ga, gb = acs[-1], bcs[-1]
    srow = lax.broadcasted_iota(jnp.int32, (sub, 1), 0)
    sft = 1
    while sft < sub:
        keep = srow >= sft
        a_sh = jnp.where(keep, pltpu.roll(ga, sft, axis=0), 1.0)
        b_sh = jnp.where(keep, pltpu.roll(gb, sft, axis=0), 0.0)
        gb = ga * b_sh + gb
        ga = ga * a_sh
        sft *= 2
    hprev = hprev_ref[bi, 0:1, :]
    hend = ga * hprev + gb
    hin = jnp.where(srow == 0, hprev, pltpu.roll(hend, 1, axis=0))
    hprev_ref[bi, 0:1, :] = hend[sub - 1:sub, :]
    hs = jnp.stack([acs[p] * hin + bcs[p] for p in range(nrow)], axis=0).reshape(tr, D_RNN)
    yp = (hs * gate).astype(BF16)
    y = jnp.dot(unperm, yp, preferred_element_type=F32).astype(BF16)
    o_ref[bi] = x + jnp.dot(y, wout_ref[...], preferred_element_type=F32)


def _rglru_body(x_ref, *rest):
    xprev_ref, hprev_ref = rest[-2:]

    @pl.when(pl.program_id(1) == 0)
    def _():
        xprev_ref[...] = jnp.zeros_like(xprev_ref)
        hprev_ref[...] = jnp.zeros_like(hprev_ref)

    for bi in range(x_ref.shape[0]):
        _rglru_tile(bi, x_ref, *rest)


def _rglru(x, g, w_in, conv_w, conv_b, wa, ba, wx, bx, lam, w_out, lead):
    b, s, d = x.shape
    tr = min(RGLRU_ROWS, s)
    pw = 2 * C_BLOCK_W
    wax = jnp.zeros((C_BLOCKS // 2, pw, 2 * pw), F32)
    for half in range(2):
        sl = slice(half * C_BLOCK_W, (half + 1) * C_BLOCK_W)
        wax = wax.at[:, sl, sl].set(wa[half::2])
        wax = wax.at[:, sl, pw + half * C_BLOCK_W:pw + (half + 1) * C_BLOCK_W].set(wx[half::2])
    wax = wax.astype(BF16)
    vec = lambda a: a.reshape(1, -1).astype(F32)
    args = [x, vec(g), w_in, conv_w.astype(F32), vec(conv_b), wax, vec(ba), vec(bx), vec(lam), w_out]
    nb = min(RGLRU_BATCH, b)
    tok = pl.BlockSpec((nb, tr, d), lambda bi, i: (bi, i, 0))
    return pl.pallas_call(
        _rglru_body,
        out_shape=jax.ShapeDtypeStruct((b, s, d), F32),
        grid=(b // nb, s // tr),
        in_specs=([tok, _const_spec(args[1].shape), _pick_spec(w_in.shape, lead)]
                  + [_const_spec(a.shape) for a in args[3:-1]] + [_pick_spec(w_out.shape, lead)]),
        out_specs=tok,
        scratch_shapes=[pltpu.VMEM((nb, CONV_W - 1, V7X_SUBLANES, D_RNN), F32),
                        pltpu.VMEM((nb, V7X_SUBLANES, D_RNN), F32)],
        compiler_params=_params("parallel", "arbitrary"),
        name="rglru",
    )(*args)


def kernel(x, norm_ffn, ffn_w_gate, ffn_w_up, ffn_w_down, norm_mix, ab_w_in, ab_w_out, att_rel_bias, rwkv_mu, rwkv_w0, rwkv_w_up, rwkv_a0, rwkv_a_up, rwkv_g_up, rwkv_k_k, rwkv_k_a, rwkv_r_k, rwkv_ln_w, rwkv_ln_b, rwkv_v0, rwkv_v_down, rwkv_v_up, c_w_in, c_conv_w, c_conv_b, c_wa, c_ba, c_wx, c_bx, c_lambda, c_w_out, norm_final):
    b, s, d = x.shape
    t = b * s
    depth = norm_mix.shape[0]
    wg, wu, wd = ffn_w_gate.astype(F32), ffn_w_up.astype(F32), ffn_w_down.astype(F32)
    w_in, w_out = ab_w_in.astype(BF16), ab_w_out.astype(BF16)
    cw_in, cw_out = c_w_in.astype(BF16), c_w_out.astype(BF16)
    gfin = norm_final.reshape(1, d).astype(F32)
    x = x.astype(F32).reshape(t, d)
    v_first = None
    for l in range(depth):
        x = _ffn(x, norm_ffn[l, 0].reshape(1, d), wg, wu, wd, (l, 0), gfin, False)
        mixer = None
        if l % 2 == 0:
            i = l // 2
            qkv, pb = _ab_in(x, norm_mix[l].reshape(1, d), w_in, (i,))
            oa = _attention(qkv.reshape(b, s, -1), att_rel_bias[i])
            prm = dict(mu=rwkv_mu[i], w0=rwkv_w0[i], w_up=rwkv_w_up[i], a0=rwkv_a0[i], a_up=rwkv_a_up[i],
                       g_up=rwkv_g_up[i], k_k=rwkv_k_k[i], k_a=rwkv_k_a[i], r_k=rwkv_r_k[i],
                       ln_w=rwkv_ln_w[i], ln_b=rwkv_ln_b[i])
            if i == 0:
                ob, v_first = _rwkv(pb.reshape(b, s, -1), None, prm, None)
            else:
                ob = _rwkv(pb.reshape(b, s, -1), v_first, prm,
                           (rwkv_v0[i - 1], rwkv_v_down[i - 1], rwkv_v_up[i - 1]))
            mixer = (oa.reshape(t, -1), ob.reshape(t, -1), w_out, (i,))
        else:
            j = l // 2
            x = _rglru(x.reshape(b, s, d), norm_mix[l], cw_in, c_conv_w[j], c_conv_b[j], c_wa[j], c_ba[j],
                       c_wx[j], c_bx[j], c_lambda[j], cw_out, (j,)).reshape(t, d)
        x = _ffn(x, norm_ffn[l, 1].reshape(1, d), wg, wu, wd, (l, 1), gfin, l == depth - 1, mixer)
    return x.reshape(b, s, d)
```

```python
import functools

import jax
import jax.numpy as jnp
import numpy as np
from jax import lax
from jax.experimental import pallas as pl
from jax.experimental.pallas import tpu as pltpu

F32 = jnp.float32
BF16 = jnp.bfloat16

D_MODEL = 1024
DEPTH = 4
CHUNK = 64
LEFT_CHUNKS = 8
A_HEADS = 8
A_HEAD_DIM = 64
A_WIDTH = A_HEADS * A_HEAD_DIM
REL_CLIP = 128
B_HEADS = 8
B_HEAD_SIZE = 64
B_WIDTH = B_HEADS * B_HEAD_SIZE
W_LORA = 64
A_LORA = 64
G_LORA = 128
V_LORA = 32
DECAY_SCALE = 0.606531
GN_EPS = 64e-5
B_COLS = 3 * B_WIDTH + W_LORA + A_LORA + G_LORA
D_RNN = 1280
C_BLOCKS = 10
C_BLOCK_W = D_RNN // C_BLOCKS
CONV_W = 4
RG_C = 8.0
D_FF = 2816
NORM_EPS = 1e-6
NEG_INF = -1e30

V7X_LANES = 128
V7X_SUBLANES = 8
V7X_MXU_DIM = 256
V7X_VMEM_BYTES = 64 * 1024 * 1024

FFN_ROWS = 512
FFN_CHUNKS = (0, 768, 1536, 2304, 2816)
FFN_STAGE_IN_ROWS = 64
FFN_STAGE_OUT_ROWS = 176
FFN_STAGE_SLOTS = 8
PROJ_ROWS = 512
ATT_ROWS = 256
ATT_PREV = LEFT_CHUNKS * CHUNK
RWKV_CHUNK = 64
RGLRU_ROWS = 256
RGLRU_BATCH = 1
VMEM_LIMIT = 56 * 1024 * 1024


def _params(*sem):
    return pltpu.CompilerParams(dimension_semantics=sem, vmem_limit_bytes=VMEM_LIMIT)


def _const_spec(shape):
    nd = len(shape)
    return pl.BlockSpec(shape, lambda *_: (0,) * nd, pipeline_mode=pl.Buffered(1))


def _pick_spec(shape, lead):
    tail = tuple(shape[len(lead):])
    index = tuple(lead) + (0,) * len(tail)
    return pl.BlockSpec((None,) * len(lead) + tail, lambda *_: index, pipeline_mode=pl.Buffered(1))


def _mm(a, b):
    return jnp.dot(a.astype(BF16), b.astype(BF16), preferred_element_type=F32)


def _mm_nt(a, b):
    return lax.dot_general(a.astype(BF16), b.astype(BF16), (((1,), (1,)), ((), ())),
                           preferred_element_type=F32)


def _mm_tn(a, b):
    return lax.dot_general(a.astype(BF16), b.astype(BF16), (((0,), (0,)), ((), ())),
                           preferred_element_type=F32)


def _rms(x, g):
    ms = jnp.mean(x * x, axis=-1, keepdims=True)
    return x * lax.rsqrt(ms + NORM_EPS) * g


def _sigmoid(x):
    return 0.5 * jnp.tanh(0.5 * x) + 0.5


def _stage_copy(src, k, rows, buf, sem, slot):
    return pltpu.make_async_copy(src.at[pl.ds(k * rows, rows)], buf.at[slot], sem.at[slot])


def _stage_weight(src, dst, buf, sem):
    nslot, rows = buf.shape[0], buf.shape[1]
    n = src.shape[0] // rows
    for k in range(min(nslot - 1, n)):
        _stage_copy(src, k, rows, buf, sem, k % nslot).start()
    for k in range(n):
        ahead = k + nslot - 1
        if ahead < n:
            _stage_copy(src, ahead, rows, buf, sem, ahead % nslot).start()
        _stage_copy(src, k, rows, buf, sem, k % nslot).wait()
        dst[pl.ds(k * rows, rows), :] = buf[k % nslot].astype(BF16)


def _ffn_body(*refs, final_norm, mixer_out, lead):
    (wg_hbm, wu_hbm, wd_hbm, gf_ref, o_ref, wg_ref, wu_ref, wd_ref, stage_in, stage_out, sem_in, sem_out) = refs[-12:]

    @pl.when(pl.program_id(0) == 0)
    def _():
        _stage_weight(wg_hbm.at[lead[0], lead[1]], wg_ref, stage_in, sem_in)
        _stage_weight(wu_hbm.at[lead[0], lead[1]], wu_ref, stage_in, sem_in)
        _stage_weight(wd_hbm.at[lead[0], lead[1]], wd_ref, stage_out, sem_out)

    if mixer_out:
        oa_ref, ob_ref, wo_ref, x_ref, g_ref = refs[:5]
        x = x_ref[...] + jnp.dot(oa_ref[...], wo_ref[:A_WIDTH, :], preferred_element_type=F32)
        x = x + jnp.dot(ob_ref[...], wo_ref[A_WIDTH:, :], preferred_element_type=F32)
    else:
        x_ref, g_ref = refs[:2]
        x = x_ref[...]
    h = _rms(x, g_ref[...]).astype(BF16)
    acc = None
    for lo, hi in zip(FFN_CHUNKS[:-1], FFN_CHUNKS[1:]):
        gt = jnp.dot(h, wg_ref[:, lo:hi], preferred_element_type=F32)
        up = jnp.dot(h, wu_ref[:, lo:hi], preferred_element_type=F32)
        act = (gt * _sigmoid(gt) * up).astype(BF16)
        d = jnp.dot(act, wd_ref[lo:hi, :], preferred_element_type=F32)
        acc = d if acc is None else acc + d
    y = x + 0.5 * acc
    if final_norm:
        y = _rms(y, gf_ref[...])
    o_ref[...] = y


def _ffn(x, g, wg, wu, wd, lead, gf, final_norm, mixer=None):
    t, d = x.shape
    f = wg.shape[-1]
    tm = min(FFN_ROWS, t)
    row = pl.BlockSpec((tm, d), lambda i: (i, 0))
    hbm = pl.BlockSpec(memory_space=pl.ANY)
    args = [x, g, wg, wu, wd, gf]
    in_specs = [row, _const_spec((1, d)), hbm, hbm, hbm, _const_spec((1, d))]
    if mixer is not None:
        oa, ob, wo, lead_out = mixer
        args = [oa, ob, wo] + args
        in_specs = [pl.BlockSpec((tm, oa.shape[1]), lambda i: (i, 0)), pl.BlockSpec((tm, ob.shape[1]), lambda i: (i, 0)),
                    _pick_spec(wo.shape, lead_out)] + in_specs
    assert d % FFN_STAGE_IN_ROWS == 0 and f % FFN_STAGE_OUT_ROWS == 0
    return pl.pallas_call(
        functools.partial(_ffn_body, final_norm=final_norm, mixer_out=mixer is not None, lead=lead),
        out_shape=jax.ShapeDtypeStruct((t, d), F32),
        grid=(t // tm,),
        in_specs=in_specs,
        out_specs=row,
        scratch_shapes=[pltpu.VMEM((d, f), BF16), pltpu.VMEM((d, f), BF16), pltpu.VMEM((f, d), BF16),
                        pltpu.VMEM((FFN_STAGE_SLOTS, FFN_STAGE_IN_ROWS, f), F32),
                        pltpu.VMEM((FFN_STAGE_SLOTS, FFN_STAGE_OUT_ROWS, d), F32),
                        pltpu.SemaphoreType.DMA((FFN_STAGE_SLOTS,)), pltpu.SemaphoreType.DMA((FFN_STAGE_SLOTS,))],
        compiler_params=_params("arbitrary"),
        name="ffn_final" if final_norm else ("ffn_mix" if mixer is not None else "ffn"),
    )(*args)


def _ab_in_body(x_ref, g_ref, w_ref, qkv_ref, pb_ref):
    h = _rms(x_ref[...], g_ref[...]).astype(BF16)
    na = qkv_ref.shape[-1]
    qkv_ref[...] = jnp.dot(h, w_ref[:, :na], preferred_element_type=F32).astype(BF16)
    pb_ref[...] = jnp.dot(h, w_ref[:, na:], preferred_element_type=F32)


def _ab_in(x, g, w, lead):
    t, d = x.shape
    n = w.shape[-1]
    na = 3 * A_WIDTH
    tm = min(PROJ_ROWS, t)
    return pl.pallas_call(
        _ab_in_body,
        out_shape=(jax.ShapeDtypeStruct((t, na), BF16), jax.ShapeDtypeStruct((t, n - na), F32)),
        grid=(t // tm,),
        in_specs=[pl.BlockSpec((tm, d), lambda i: (i, 0)), _const_spec((1, d)), _pick_spec(w.shape, lead)],
        out_specs=(pl.BlockSpec((tm, na), lambda i: (i, 0)), pl.BlockSpec((tm, n - na), lambda i: (i, 0))),
        compiler_params=_params("parallel"),
        name="ab_in",
    )(x, g, w)


def _attn_body(*refs, n_kblocks):
    q_ref = refs[0]
    k_refs = refs[1:1 + n_kblocks]
    v_refs = refs[1 + n_kblocks:1 + 2 * n_kblocks]
    bias_ref = refs[1 + 2 * n_kblocks]
    o_ref = refs[2 + 2 * n_kblocks]
    qb = q_ref.shape[0]
    i = pl.program_id(1)
    n_prev = n_kblocks - 1
    pens = [jnp.where(i >= n_prev - j, 0.0, NEG_INF).astype(F32) for j in range(n_prev)] + [None]
    lane = lax.broadcasted_iota(jnp.int32, (1, 2 * A_HEAD_DIM), 1)
    first = lane < A_HEAD_DIM
    scale = A_HEAD_DIM ** -0.5
    for pr in range(A_HEADS // 2):
        ls = slice(2 * A_HEAD_DIM * pr, 2 * A_HEAD_DIM * (pr + 1))
        q_pair = q_ref[:, ls]
        ks = [r[:, ls] for r in k_refs]
        vs = [r[:, ls] for r in v_refs]
        outs = []
        for hh in range(2):
            head = 2 * pr + hh
            sel = first if hh == 0 else jnp.logical_not(first)
            qm = jnp.where(sel, scale, 0.0).astype(BF16)
            qh = q_pair * qm
            s = []
            for j in range(n_kblocks):
                sj = lax.dot_general(qh, ks[j], (((1,), (1,)), ((), ())), preferred_element_type=F32)
                sj = sj + bias_ref[head, :, j * qb:(j + 1) * qb]
                if pens[j] is not None:
                    sj = sj + pens[j]
                s.append(sj)
            m = s[0].max(axis=-1, keepdims=True)
            for sj in s[1:]:
                m = jnp.maximum(m, sj.max(axis=-1, keepdims=True))
            l = None
            pv = None
            for j in range(n_kblocks):
                p = jnp.exp(s[j] - m)
                lj = p.sum(axis=-1, keepdims=True)
                pvj = jnp.dot(p.astype(BF16), vs[j], preferred_element_type=F32)
                l = lj if l is None else l + lj
                pv = pvj if pv is None else pv + pvj
            outs.append(pv / l)
        o_ref[:, ls] = jnp.where(first, outs[0], outs[1]).astype(BF16)


def _attn_bias_tile(rel_bias, qb):
    nh = rel_bias.shape[0]
    w = ATT_PREV + qb
    period = w + qb
    near = 2 * REL_CLIP + 1
    assert ATT_PREV >= REL_CLIP and qb >= REL_CLIP
    far_past = jnp.broadcast_to(rel_bias[:, near - 1:], (nh, ATT_PREV - REL_CLIP))
    far_future = jnp.broadcast_to(rel_bias[:, :1], (nh, qb - REL_CLIP))
    wrapped = jnp.broadcast_to(rel_bias[:, near - 1:], (nh, qb - 1))
    t = jnp.concatenate([far_past, rel_bias[:, ::-1], far_future, wrapped], axis=1).astype(F32)
    assert t.shape[1] == period
    toe = jnp.broadcast_to(t[:, None, :], (nh, qb, period)).reshape(nh, qb * period)
    toe = toe[:, :qb * (period - 1)].reshape(nh, qb, period - 1)[:, :, :w]
    r = np.arange(qb)[:, None]
    c = np.arange(w)[None, :]
    band = (c // CHUNK >= r // CHUNK) & (c // CHUNK <= r // CHUNK + LEFT_CHUNKS)
    return jnp.where(band[None], toe, NEG_INF)


def _attention(qkv, rel_bias):
    b, s, _ = qkv.shape
    qb = min(ATT_ROWS, s)
    assert ATT_PREV % qb == 0 and s % qb == 0
    n_prev = ATT_PREV // qb
    n_kblocks = n_prev + 1
    bias = _attn_bias_tile(rel_bias, qb)

    def blk(col, back):
        return pl.BlockSpec((None, qb, A_WIDTH), lambda bi, i: (bi, jnp.maximum(i - back, 0), col))

    in_specs = ([blk(0, 0)] + [blk(1, n_prev - j) for j in range(n_kblocks)]
                + [blk(2, n_prev - j) for j in range(n_kblocks)] + [_const_spec(bias.shape)])
    return pl.pallas_call(
        functools.partial(_attn_body, n_kblocks=n_kblocks),
        out_shape=jax.ShapeDtypeStruct((b, s, A_WIDTH), BF16),
        grid=(b, s // qb),
        in_specs=in_specs,
        out_specs=pl.BlockSpec((None, qb, A_WIDTH), lambda bi, i: (bi, i, 0)),
        compiler_params=_params("parallel", "parallel"),
        name="attention",
    )(*([qkv] * (1 + 2 * n_kblocks)), bias)


def _split3(x):
    h1 = x.astype(BF16)
    r1 = x - h1.astype(F32)
    h2 = r1.astype(BF16)
    h3 = (r1 - h2.astype(F32)).astype(BF16)
    return h1, h2, h3


def _rwkv_body(*refs, has_vmix):
    if has_vmix:
        (pb_ref, vf_ref, mu_ref, w0_ref, wup_ref, a0_ref, aup_ref, gup_ref, kk_ref, ka_ref, rk_ref,
         lnw_ref, lnb_ref, v0_ref, vdn_ref, vup_ref, o_ref, state_ref, carry_ref) = refs
        vraw_ref = None
    else:
        (pb_ref, mu_ref, w0_ref, wup_ref, a0_ref, aup_ref, gup_ref, kk_ref, ka_ref, rk_ref,
         lnw_ref, lnb_ref, o_ref, vraw_ref, state_ref, carry_ref) = refs
    nb, c, ncol = pb_ref.shape
    n = B_HEAD_SIZE
    rows = nb * c

    @pl.when(pl.program_id(0) == 0)
    def _():
        state_ref[...] = jnp.zeros_like(state_ref)
        carry_ref[...] = jnp.zeros_like(carry_ref)

    p = pb_ref[...].reshape(rows, ncol)
    row = lax.broadcasted_iota(jnp.int32, (rows, 1), 0)
    prev = pltpu.roll(p, 1, axis=0)
    for b in range(nb):
        prev = jnp.where(row == b * c, carry_ref[b, 0:1, :], prev)
    for b in range(nb):
        carry_ref[b, 0:1, :] = p[(b + 1) * c - 1:(b + 1) * c, :]
    ps = p + (prev - p) * mu_ref[...]
    r = ps[:, 0:B_WIDTH]
    k = ps[:, B_WIDTH:2 * B_WIDTH]
    v = ps[:, 2 * B_WIDTH:3 * B_WIDTH]
    wad = ps[:, 3 * B_WIDTH:3 * B_WIDTH + W_LORA + A_LORA]
    gd = ps[:, 3 * B_WIDTH + W_LORA + A_LORA:]
    if has_vmix:
        mix = _sigmoid(v0_ref[...] + _mm(_mm(v, vdn_ref[...]), vup_ref[...]))
        v = v + (vf_ref[...].reshape(rows, B_WIDTH) - v) * mix
    else:
        vraw_ref[...] = v.reshape(nb, c, B_WIDTH)
    wlog = -DECAY_SCALE * _sigmoid(w0_ref[...] + _mm(jnp.tanh(wad), wup_ref[...]))
    a = _sigmoid(a0_ref[...] + _mm(wad, aup_ref[...]))
    g = _mm(_sigmoid(gd), gup_ref[...])
    kkf = k * kk_ref[...]
    kka = kkf * a
    kmod = k * (1.0 + (a - 1.0) * ka_ref[...])
    rkr = r * kmod * rk_ref[...]

    ti = lax.broadcasted_iota(jnp.int32, (c, 3 * c), 0)
    tj = lax.broadcasted_iota(jnp.int32, (c, 3 * c), 1) % c
    tri3 = jnp.where(tj <= ti, 1.0, 0.0).astype(BF16)
    lcum, lend = [], []
    for b in range(nb):
        lb = jnp.dot(tri3, jnp.concatenate(_split3(wlog[b * c:(b + 1) * c]), axis=0),
                     preferred_element_type=F32)
        lcum.append(lb)
        lend.append(jnp.broadcast_to(lb[c - 1:c, :], lb.shape))
    lcum = jnp.concatenate(lcum, axis=0)
    lend = jnp.concatenate(lend, axis=0)
    e_neg = jnp.exp(-lcum)
    e_end = jnp.exp(lend - lcum)
    w_end = jnp.exp(lend)

    pair = 2 * n
    npair = B_HEADS // 2
    lo = lax.broadcasted_iota(jnp.int32, (1, 1, pair), 2) < n

    def slabs(q):
        return jnp.stack([q[b * c:(b + 1) * c, p * pair:(p + 1) * pair]
                          for b in range(nb) for p in range(npair)], axis=0)

    def unslabs(q):
        return jnp.concatenate(
            [jnp.concatenate([q[b * npair + p] for p in range(npair)], axis=-1) for b in range(nb)], axis=0)

    def head_sum(q):
        s_lo = jnp.sum(jnp.where(lo, q, 0.0), axis=-1, keepdims=True)
        s_hi = jnp.sum(jnp.where(lo, 0.0, q), axis=-1, keepdims=True)
        return jnp.where(lo, s_lo, s_hi)

    def bd(q):
        return jnp.concatenate([jnp.where(lo, q, 0.0), jnp.where(lo, 0.0, q)], axis=1)

    def bmm(lhs, rhs):
        return jnp.einsum("gij,gjv->giv", lhs.astype(BF16), rhs.astype(BF16), preferred_element_type=F32)

    kk_s = slabs(kkf)
    rs = lax.rsqrt(jnp.maximum(head_sum(kk_s * kk_s), 1e-24))
    x = jnp.concatenate([slabs(-kkf * jnp.exp(lcum - wlog)) * rs, slabs(r * jnp.exp(lcum))], axis=1)
    bt = slabs(kka * e_neg) * rs
    kt = slabs(kmod * e_neg)
    bk_end = jnp.concatenate([slabs(kka * e_end) * rs, slabs(kmod * e_end)], axis=1)
    vh = slabs(v)
    s0 = state_ref[...]
    xb = x.astype(BF16)
    ybd = jnp.concatenate([bd(bt), bd(kt)], axis=1).astype(BF16)
    gm = jnp.einsum("gik,gjk->gij", xb, ybd, preferred_element_type=F32)
    xs = jnp.einsum("gik,gvk->giv", xb, s0.astype(BF16), preferred_element_type=F32)
    ri = lax.broadcasted_iota(jnp.int32, (1, c, 2 * c), 1)
    ci = lax.broadcasted_iota(jnp.int32, (1, c, 2 * c), 2) % c
    strict = ci < ri
    incl = ci <= ri
    m_ab = jnp.where(strict, gm[:, :c, :2 * c], 0.0)
    m_ak = jnp.where(strict, gm[:, :c, 2 * c:], 0.0)
    m_rb = jnp.where(incl, gm[:, c:, :2 * c], 0.0)
    m_rk = jnp.where(incl, gm[:, c:, 2 * c:], 0.0)
    bdv = bd(vh)
    u = xs[:, :c] + bmm(m_ak, bdv)
    pw = m_ab
    span = 1
    while True:
        u = u + bmm(pw, bd(u))
        span *= 2
        if span >= c:
            break
        pw = bmm(pw, bd(pw))
    o = xs[:, c:] + bmm(jnp.concatenate([m_rb, m_rk], axis=2), jnp.concatenate([bd(u), bdv], axis=1))
    w_end_s = jnp.stack([w_end[b * c:b * c + 1, p * pair:(p + 1) * pair]
                         for b in range(nb) for p in range(npair)], axis=0)
    uv = jnp.concatenate([u, vh], axis=1)
    upd = jnp.einsum("gjv,gjk->gvk", uv.astype(BF16), bk_end.astype(BF16), preferred_element_type=F32)
    same_head = (lax.broadcasted_iota(jnp.int32, (1, pair, pair), 1) < n) == lo
    state_ref[...] = s0 * w_end_s + jnp.where(same_head, upd, 0.0)
    mean = head_sum(o) * (1.0 / n)
    d = o - mean
    var = head_sum(d * d) * (1.0 / n)
    o = d * lax.rsqrt(var + GN_EPS)
    bonus = head_sum(slabs(rkr)) * vh
    out = (unslabs(o) * lnw_ref[...] + lnb_ref[...] + unslabs(bonus)) * g
    o_ref[...] = out.astype(BF16).reshape(nb, c, B_WIDTH)


def _rwkv(pb, v_first, prm, vmix):
    b, s, ncol = pb.shape
    c = min(RWKV_CHUNK, s)
    has_vmix = vmix is not None
    tok = lambda w: pl.BlockSpec((b, c, w), lambda i: (0, i, 0))
    vec = lambda a: a.reshape(1, -1).astype(F32)
    lw = W_LORA + A_LORA
    wup = jnp.zeros((lw, B_WIDTH), F32).at[:W_LORA].set(prm["w_up"]).astype(BF16)
    aup = jnp.zeros((lw, B_WIDTH), F32).at[W_LORA:].set(prm["a_up"]).astype(BF16)
    common = [vec(prm["mu"]), vec(prm["w0"]), wup, vec(prm["a0"]), aup, prm["g_up"].astype(BF16),
              vec(prm["k_k"]), vec(prm["k_a"]), vec(prm["r_k"]), vec(prm["ln_w"]), vec(prm["ln_b"])]
    if has_vmix:
        v0, v_down, v_up = vmix
        vdn = jnp.zeros((B_WIDTH, V7X_LANES), F32).at[:, :V_LORA].set(v_down).astype(BF16)
        vup = jnp.zeros((V7X_LANES, B_WIDTH), F32).at[:V_LORA].set(v_up).astype(BF16)
        args = [pb, v_first] + common + [vec(v0), vdn, vup]
        in_specs = [tok(ncol), tok(B_WIDTH)] + [_const_spec(a.shape) for a in args[2:]]
        out_shape = jax.ShapeDtypeStruct((b, s, B_WIDTH), BF16)
        out_specs = tok(B_WIDTH)
    else:
        args = [pb] + common
        in_specs = [tok(ncol)] + [_const_spec(a.shape) for a in args[1:]]
        out_shape = (jax.ShapeDtypeStruct((b, s, B_WIDTH), BF16), jax.ShapeDtypeStruct((b, s, B_WIDTH), F32))
        out_specs = (tok(B_WIDTH), tok(B_WIDTH))
    return pl.pallas_call(
        functools.partial(_rwkv_body, has_vmix=has_vmix),
        out_shape=out_shape,
        grid=(s // c,),
        in_specs=in_specs,
        out_specs=out_specs,
        scratch_shapes=[pltpu.VMEM((b * B_HEADS // 2, 2 * B_HEAD_SIZE, 2 * B_HEAD_SIZE), F32),
                        pltpu.VMEM((b, V7X_SUBLANES, ncol), F32)],
        compiler_params=_params("arbitrary"),
        name="rwkv_mix" if has_vmix else "rwkv",
    )(*args)


def _gelu_tanh(x):
    return 0.5 * x * (1.0 + jnp.tanh(np.sqrt(2.0 / np.pi).astype(np.float32) * (x + 0.044715 * (x * x * x))))


def _rglru_tile(bi, x_ref, g_ref, win_ref, cw_ref, cb_ref, wax_ref, ba_ref, bx_ref, lam_ref, wout_ref,
                o_ref, xprev_ref, hprev_ref):
    tr = x_ref.shape[1]
    sub = V7X_SUBLANES
    nrow = tr // sub
    ntail = CONV_W - 1
    x = x_ref[bi]
    h = _rms(x, g_ref[...]).astype(BF16)
    i0 = lax.broadcasted_iota(jnp.int32, (tr, tr), 0)
    i1 = lax.broadcasted_iota(jnp.int32, (tr, tr), 1)
    perm = jnp.where(i1 == (i0 % sub) * nrow + i0 // sub, 1.0, 0.0).astype(BF16)
    unperm = jnp.where(i0 == (i1 % sub) * nrow + i1 // sub, 1.0, 0.0).astype(BF16)
    hp = jnp.dot(perm, h, preferred_element_type=F32).astype(BF16)
    gate = _gelu_tanh(jnp.dot(hp, win_ref[:, :D_RNN], preferred_element_type=F32))
    xb3 = jnp.dot(hp, win_ref[:, D_RNN:], preferred_element_type=F32).reshape(nrow, sub, D_RNN)
    seg = lax.broadcasted_iota(jnp.int32, (1, sub, 1), 1)
    prev3 = xprev_ref[bi]
    tails = [jnp.where(seg == 0, pltpu.roll(prev3[q:q + 1], 1, axis=1),
                       pltpu.roll(xb3[nrow - ntail + q:nrow - ntail + q + 1], 1, axis=1)) for q in range(ntail)]
    xprev_ref[bi] = xb3[nrow - ntail:]
    xc3 = xb3 * cw_ref[CONV_W - 1:CONV_W, :] + cb_ref[...]
    for sft in range(1, CONV_W):
        shifted = jnp.concatenate(tails[ntail - sft:] + [xb3[:nrow - sft]], axis=0)
        xc3 = xc3 + shifted * cw_ref[CONV_W - 1 - sft:CONV_W - sft, :]
    xc = xc3.reshape(tr, D_RNN)
    xcb = xc.astype(BF16)
    pw = 2 * C_BLOCK_W
    ra, rx = [], []
    for q in range(C_BLOCKS // 2):
        t = jnp.dot(xcb[:, q * pw:(q + 1) * pw], wax_ref[q], preferred_element_type=F32)
        ra.append(t[:, :pw])
        rx.append(t[:, pw:])
    rgate = _sigmoid(jnp.concatenate(ra, axis=-1) + ba_ref[...])
    igate = _sigmoid(jnp.concatenate(rx, axis=-1) + bx_ref[...])
    z = -lam_ref[...]
    softplus = jnp.maximum(z, 0.0) + jnp.log(1.0 + jnp.exp(-jnp.abs(z)))
    a = jnp.exp(-RG_C * rgate * softplus)
    mult = jnp.sqrt(jnp.maximum(1.0 - a * a, 0.0))
    bb = mult * igate * xc
    a3 = a.reshape(nrow, sub, D_RNN)
    b3 = bb.reshape(nrow, sub, D_RNN)
    acs, bcs = [a3[0]], [b3[0]]
    for p in range(1, nrow):
        bcs.append(a3[p] * bcs[-1] + b3[p])
        acs.append(a3[p] * acs[-1])
    ga, gb = acs[-1], bcs[-1]
    srow = lax.broadcasted_iota(jnp.int32, (sub, 1), 0)
    sft = 1
    while sft < sub:
        keep = srow >= sft
        a_sh = jnp.where(keep, pltpu.roll(ga, sft, axis=0), 1.0)
        b_sh = jnp.where(keep, pltpu.roll(gb, sft, axis=0), 0.0)
        gb = ga * b_sh + gb
        ga = ga * a_sh
        sft *= 2
    hprev = hprev_ref[bi, 0:1, :]
    hend = ga * hprev + gb
    hin = jnp.where(srow == 0, hprev, pltpu.roll(hend, 1, axis=0))
    hprev_ref[bi, 0:1, :] = hend[sub - 1:sub, :]
    hs = jnp.stack([acs[p] * hin + bcs[p] for p in range(nrow)], axis=0).reshape(tr, D_RNN)
    yp = (hs * gate).astype(BF16)
    y = jnp.dot(unperm, yp, preferred_element_type=F32).astype(BF16)
    o_ref[bi] = x + jnp.dot(y, wout_ref[...], preferred_element_type=F32)


def _rglru_body(x_ref, *rest):
    xprev_ref, hprev_ref = rest[-2:]

    @pl.when(pl.program_id(1) == 0)
    def _():
        xprev_ref[...] = jnp.zeros_like(xprev_ref)
        hprev_ref[...] = jnp.zeros_like(hprev_ref)

    for bi in range(x_ref.shape[0]):
        _rglru_tile(bi, x_ref, *rest)


def _rglru(x, g, w_in, conv_w, conv_b, wa, ba, wx, bx, lam, w_out, lead):
    b, s, d = x.shape
    tr = min(RGLRU_ROWS, s)
    pw = 2 * C_BLOCK_W
    wax = jnp.zeros((C_BLOCKS // 2, pw, 2 * pw), F32)
    for half in range(2):
        sl = slice(half * C_BLOCK_W, (half + 1) * C_BLOCK_W)
        wax = wax.at[:, sl, sl].set(wa[half::2])
        wax = wax.at[:, sl, pw + half * C_BLOCK_W:pw + (half + 1) * C_BLOCK_W].set(wx[half::2])
    wax = wax.astype(BF16)
    vec = lambda a: a.reshape(1, -1).astype(F32)
    args = [x, vec(g), w_in, conv_w.astype(F32), vec(conv_b), wax, vec(ba), vec(bx), vec(lam), w_out]
    nb = min(RGLRU_BATCH, b)
    tok = pl.BlockSpec((nb, tr, d), lambda bi, i: (bi, i, 0))
    return pl.pallas_call(
        _rglru_body,
        out_shape=jax.ShapeDtypeStruct((b, s, d), F32),
        grid=(b // nb, s // tr),
        in_specs=([tok, _const_spec(args[1].shape), _pick_spec(w_in.shape, lead)]
                  + [_const_spec(a.shape) for a in args[3:-1]] + [_pick_spec(w_out.shape, lead)]),
        out_specs=tok,
        scratch_shapes=[pltpu.VMEM((nb, CONV_W - 1, V7X_SUBLANES, D_RNN), F32),
                        pltpu.VMEM((nb, V7X_SUBLANES, D_RNN), F32)],
        compiler_params=_params("parallel", "arbitrary"),
        name="rglru",
    )(*args)


def kernel(x, norm_ffn, ffn_w_gate, ffn_w_up, ffn_w_down, norm_mix, ab_w_in, ab_w_out, att_rel_bias, rwkv_mu, rwkv_w0, rwkv_w_up, rwkv_a0, rwkv_a_up, rwkv_g_up, rwkv_k_k, rwkv_k_a, rwkv_r_k, rwkv_ln_w, rwkv_ln_b, rwkv_v0, rwkv_v_down, rwkv_v_up, c_w_in, c_conv_w, c_conv_b, c_wa, c_ba, c_wx, c_bx, c_lambda, c_w_out, norm_final):
    b, s, d = x.shape
    t = b * s
    depth = norm_mix.shape[0]
    wg, wu, wd = ffn_w_gate.astype(F32), ffn_w_up.astype(F32), ffn_w_down.astype(F32)
    w_in, w_out = ab_w_in.astype(BF16), ab_w_out.astype(BF16)
    cw_in, cw_out = c_w_in.astype(BF16), c_w_out.astype(BF16)
    gfin = norm_final.reshape(1, d).astype(F32)
    x = x.astype(F32).reshape(t, d)
    v_first = None
    for l in range(depth):
        x = _ffn(x, norm_ffn[l, 0].reshape(1, d), wg, wu, wd, (l, 0), gfin, False)
        mixer = None
        if l % 2 == 0:
            i = l // 2
            qkv, pb = _ab_in(x, norm_mix[l].reshape(1, d), w_in, (i,))
            oa = _attention(qkv.reshape(b, s, -1), att_rel_bias[i])
            prm = dict(mu=rwkv_mu[i], w0=rwkv_w0[i], w_up=rwkv_w_up[i], a0=rwkv_a0[i], a_up=rwkv_a_up[i],
                       g_up=rwkv_g_up[i], k_k=rwkv_k_k[i], k_a=rwkv_k_a[i], r_k=rwkv_r_k[i],
                       ln_w=rwkv_ln_w[i], ln_b=rwkv_ln_b[i])
            if i == 0:
                ob, v_first = _rwkv(pb.reshape(b, s, -1), None, prm, None)
            else:
                ob = _rwkv(pb.reshape(b, s, -1), v_first, prm,
                           (rwkv_v0[i - 1], rwkv_v_down[i - 1], rwkv_v_up[i - 1]))
            mixer = (oa.reshape(t, -1), ob.reshape(t, -1), w_out, (i,))
        else:
            j = l // 2
            x = _rglru(x.reshape(b, s, d), norm_mix[l], cw_in, c_conv_w[j], c_conv_b[j], c_wa[j], c_ba[j],
                       c_wx[j], c_bx[j], c_lambda[j], cw_out, (j,)).reshape(t, d)
        x = _ffn(x, norm_ffn[l, 1].reshape(1, d), wg, wu, wd, (l, 1), gfin, l == depth - 1, mixer)
    return x.reshape(b, s, d)
```

```python
import functools

import jax
import jax.numpy as jnp
import numpy as np
from jax import lax
from jax.experimental import pallas as pl
from jax.experimental.pallas import tpu as pltpu

F32 = jnp.float32
BF16 = jnp.bfloat16

D_MODEL = 1024
DEPTH = 4
CHUNK = 64
LEFT_CHUNKS = 8
A_HEADS = 8
A_HEAD_DIM = 64
A_WIDTH = A_HEADS * A_HEAD_DIM
REL_CLIP = 128
B_HEADS = 8
B_HEAD_SIZE = 64
B_WIDTH = B_HEADS * B_HEAD_SIZE
W_LORA = 64
A_LORA = 64
G_LORA = 128
V_LORA = 32
DECAY_SCALE = 0.606531
GN_EPS = 64e-5
B_COLS = 3 * B_WIDTH + W_LORA + A_LORA + G_LORA
D_RNN = 1280
C_BLOCKS = 10
C_BLOCK_W = D_RNN // C_BLOCKS
CONV_W = 4
RG_C = 8.0
D_FF = 2816
NORM_EPS = 1e-6
NEG_INF = -1e30

V7X_LANES = 128
V7X_SUBLANES = 8
V7X_BF16_SUBLANES = 16
V7X_MXU_DIM = 256
V7X_VMEM_BYTES = 64 * 1024 * 1024

FFN_ROWS = 512
FFN_CHUNKS = (0, 768, 1536, 2304, 2816)
PROJ_ROWS = 512
ATT_ROWS = 256
ATT_PREV = LEFT_CHUNKS * CHUNK
RWKV_CHUNK = 64
RGLRU_ROWS = 256
RGLRU_BATCH = 1
VMEM_LIMIT = 56 * 1024 * 1024


def _params(*sem):
    return pltpu.CompilerParams(dimension_semantics=sem, vmem_limit_bytes=VMEM_LIMIT)


def _const_spec(shape):
    nd = len(shape)
    return pl.BlockSpec(shape, lambda *_: (0,) * nd, pipeline_mode=pl.Buffered(1))


def _pick_spec(shape, lead):
    tail = tuple(shape[len(lead):])
    index = tuple(lead) + (0,) * len(tail)
    return pl.BlockSpec((None,) * len(lead) + tail, lambda *_: index, pipeline_mode=pl.Buffered(1))


def _mm(a, b):
    return jnp.dot(a.astype(BF16), b.astype(BF16), preferred_element_type=F32)


def _mm_nt(a, b):
    return lax.dot_general(a.astype(BF16), b.astype(BF16), (((1,), (1,)), ((), ())),
                           preferred_element_type=F32)


def _mm_tn(a, b):
    return lax.dot_general(a.astype(BF16), b.astype(BF16), (((0,), (0,)), ((), ())),
                           preferred_element_type=F32)


def _rms(x, g):
    ms = jnp.mean(x * x, axis=-1, keepdims=True)
    return x * lax.rsqrt(ms + NORM_EPS) * g


def _sigmoid(x):
    return 0.5 * jnp.tanh(0.5 * x) + 0.5


def _slice_rows(total, steps):
    k = 1
    while (total * k) % steps or (total * k // steps) % V7X_BF16_SUBLANES:
        k *= 2
    return total * k // steps, k


def _ffn_body(*refs, final_norm, mixer_out, cast_next):
    refs = list(refs)
    if mixer_out:
        oa_ref, ob_ref, wo_ref = refs[:3]
        refs = refs[3:]
    x_ref, g_ref, wg_ref, wu_ref, wd_ref, gf_ref = refs[:6]
    refs = refs[6:]
    if cast_next:
        o_ref = refs[3]
        for src, dst in zip(refs[:3], refs[4:]):
            dst[...] = src[...].astype(BF16)
    else:
        o_ref = refs[0]
    x = x_ref[...]
    if mixer_out:
        x = x + jnp.dot(oa_ref[...], wo_ref[:A_WIDTH, :], preferred_element_type=F32)
        x = x + jnp.dot(ob_ref[...], wo_ref[A_WIDTH:, :], preferred_element_type=F32)
    h = _rms(x, g_ref[...]).astype(BF16)
    acc = None
    for lo, hi in zip(FFN_CHUNKS[:-1], FFN_CHUNKS[1:]):
        gt = jnp.dot(h, wg_ref[:, lo:hi], preferred_element_type=F32)
        up = jnp.dot(h, wu_ref[:, lo:hi], preferred_element_type=F32)
        act = (gt * _sigmoid(gt) * up).astype(BF16)
        d = jnp.dot(act, wd_ref[lo:hi, :], preferred_element_type=F32)
        acc = d if acc is None else acc + d
    y = x + 0.5 * acc
    if final_norm:
        y = _rms(y, gf_ref[...])
    o_ref[...] = y


def _ffn(x, g, wb, gf, final_norm, mixer=None, nxt=None):
    t, d = x.shape
    tm = min(FFN_ROWS, t)
    steps = t // tm
    row = pl.BlockSpec((tm, d), lambda i: (i, 0))
    args = [x, g, *wb, gf]
    in_specs = [row, _const_spec((1, d))] + [_const_spec(w.shape) for w in wb] + [_const_spec((1, d))]
    if mixer is not None:
        oa, ob, wo, lead_out = mixer
        args = [oa, ob, wo] + args
        in_specs = [pl.BlockSpec((tm, oa.shape[1]), lambda i: (i, 0)), pl.BlockSpec((tm, ob.shape[1]), lambda i: (i, 0)),
                    _pick_spec(wo.shape, lead_out)] + in_specs
    out_shape = [jax.ShapeDtypeStruct((t, d), F32)]
    out_specs = [row]
    if nxt is not None:
        lead = tuple(nxt[3])
        for w in nxt[:3]:
            nr, nc = w.shape[-2:]
            rows, k = _slice_rows(nr, steps)
            args.append(w)
            in_specs.append(pl.BlockSpec((None,) * len(lead) + (rows, nc), lambda i, k=k: lead + (i // k, 0)))
            out_shape.append(jax.ShapeDtypeStruct((nr, nc), BF16))
            out_specs.append(pl.BlockSpec((rows, nc), lambda i, k=k: (i // k, 0)))
    out = pl.pallas_call(
        functools.partial(_ffn_body, final_norm=final_norm, mixer_out=mixer is not None, cast_next=nxt is not None),
        out_shape=tuple(out_shape),
        grid=(steps,),
        in_specs=in_specs,
        out_specs=tuple(out_specs),
        compiler_params=_params("arbitrary"),
        name="ffn_final" if final_norm else ("ffn_mix" if mixer is not None else "ffn"),
    )(*args)
    return out[0], tuple(out[1:])


def _ab_in_body(x_ref, g_ref, w_ref, qkv_ref, pb_ref):
    h = _rms(x_ref[...], g_ref[...]).astype(BF16)
    na = qkv_ref.shape[-1]
    qkv_ref[...] = jnp.dot(h, w_ref[:, :na], preferred_element_type=F32).astype(BF16)
    pb_ref[...] = jnp.dot(h, w_ref[:, na:], preferred_element_type=F32)


def _ab_in(x, g, w, lead):
    t, d = x.shape
    n = w.shape[-1]
    na = 3 * A_WIDTH
    tm = min(PROJ_ROWS, t)
    return pl.pallas_call(
        _ab_in_body,
        out_shape=(jax.ShapeDtypeStruct((t, na), BF16), jax.ShapeDtypeStruct((t, n - na), F32)),
        grid=(t // tm,),
        in_specs=[pl.BlockSpec((tm, d), lambda i: (i, 0)), _const_spec((1, d)), _pick_spec(w.shape, lead)],
        out_specs=(pl.BlockSpec((tm, na), lambda i: (i, 0)), pl.BlockSpec((tm, n - na), lambda i: (i, 0))),
        compiler_params=_params("parallel"),
        name="ab_in",
    )(x, g, w)


def _attn_body(*refs, n_kblocks):
    q_ref = refs[0]
    k_refs = refs[1:1 + n_kblocks]
    v_refs = refs[1 + n_kblocks:1 + 2 * n_kblocks]
    bias_ref = refs[1 + 2 * n_kblocks]
    o_ref = refs[2 + 2 * n_kblocks]
    qb = q_ref.shape[0]
    i = pl.program_id(1)
    n_prev = n_kblocks - 1
    pens = [jnp.where(i >= n_prev - j, 0.0, NEG_INF).astype(F32) for j in range(n_prev)] + [None]
    lane = lax.broadcasted_iota(jnp.int32, (1, 2 * A_HEAD_DIM), 1)
    first = lane < A_HEAD_DIM
    scale = A_HEAD_DIM ** -0.5
    for pr in range(A_HEADS // 2):
        ls = slice(2 * A_HEAD_DIM * pr, 2 * A_HEAD_DIM * (pr + 1))
        q_pair = q_ref[:, ls]
        ks = [r[:, ls] for r in k_refs]
        vs = [r[:, ls] for r in v_refs]
        outs = []
        for hh in range(2):
            head = 2 * pr + hh
            sel = first if hh == 0 else jnp.logical_not(first)
            qm = jnp.where(sel, scale, 0.0).astype(BF16)
            qh = q_pair * qm
            s = []
            for j in range(n_kblocks):
                sj = lax.dot_general(qh, ks[j], (((1,), (1,)), ((), ())), preferred_element_type=F32)
                sj = sj + bias_ref[head, :, j * qb:(j + 1) * qb]
                if pens[j] is not None:
                    sj = sj + pens[j]
                s.append(sj)
            m = s[0].max(axis=-1, keepdims=True)
            for sj in s[1:]:
                m = jnp.maximum(m, sj.max(axis=-1, keepdims=True))
            l = None
            pv = None
            for j in range(n_kblocks):
                p = jnp.exp(s[j] - m)
                lj = p.sum(axis=-1, keepdims=True)
                pvj = jnp.dot(p.astype(BF16), vs[j], preferred_element_type=F32)
                l = lj if l is None else l + lj
                pv = pvj if pv is None else pv + pvj
            outs.append(pv / l)
        o_ref[:, ls] = jnp.where(first, outs[0], outs[1]).astype(BF16)


def _attn_bias_tile(rel_bias, qb):
    nh = rel_bias.shape[0]
    w = ATT_PREV + qb
    period = w + qb
    near = 2 * REL_CLIP + 1
    assert ATT_PREV >= REL_CLIP and qb >= REL_CLIP
    far_past = jnp.broadcast_to(rel_bias[:, near - 1:], (nh, ATT_PREV - REL_CLIP))
    far_future = jnp.broadcast_to(rel_bias[:, :1], (nh, qb - REL_CLIP))
    wrapped = jnp.broadcast_to(rel_bias[:, near - 1:], (nh, qb - 1))
    t = jnp.concatenate([far_past, rel_bias[:, ::-1], far_future, wrapped], axis=1).astype(F32)
    assert t.shape[1] == period
    toe = jnp.broadcast_to(t[:, None, :], (nh, qb, period)).reshape(nh, qb * period)
    toe = toe[:, :qb * (period - 1)].reshape(nh, qb, period - 1)[:, :, :w]
    r = np.arange(qb)[:, None]
    c = np.arange(w)[None, :]
    band = (c // CHUNK >= r // CHUNK) & (c // CHUNK <= r // CHUNK + LEFT_CHUNKS)
    return jnp.where(band[None], toe, NEG_INF)


def _attention(qkv, rel_bias):
    b, s, _ = qkv.shape
    qb = min(ATT_ROWS, s)
    assert ATT_PREV % qb == 0 and s % qb == 0
    n_prev = ATT_PREV // qb
    n_kblocks = n_prev + 1
    bias = _attn_bias_tile(rel_bias, qb)

    def blk(col, back):
        return pl.BlockSpec((None, qb, A_WIDTH), lambda bi, i: (bi, jnp.maximum(i - back, 0), col))

    in_specs = ([blk(0, 0)] + [blk(1, n_prev - j) for j in range(n_kblocks)]
                + [blk(2, n_prev - j) for j in range(n_kblocks)] + [_const_spec(bias.shape)])
    return pl.pallas_call(
        functools.partial(_attn_body, n_kblocks=n_kblocks),
        out_shape=jax.ShapeDtypeStruct((b, s, A_WIDTH), BF16),
        grid=(b, s // qb),
        in_specs=in_specs,
        out_specs=pl.BlockSpec((None, qb, A_WIDTH), lambda bi, i: (bi, i, 0)),
        compiler_params=_params("parallel", "parallel"),
        name="attention",
    )(*([qkv] * (1 + 2 * n_kblocks)), bias)


def _split3(x):
    h1 = x.astype(BF16)
    r1 = x - h1.astype(F32)
    h2 = r1.astype(BF16)
    h3 = (r1 - h2.astype(F32)).astype(BF16)
    return h1, h2, h3


def _rwkv_body(*refs, has_vmix):
    if has_vmix:
        (pb_ref, vf_ref, mu_ref, w0_ref, wup_ref, a0_ref, aup_ref, gup_ref, kk_ref, ka_ref, rk_ref,
         lnw_ref, lnb_ref, v0_ref, vdn_ref, vup_ref, o_ref, state_ref, carry_ref) = refs
        vraw_ref = None
    else:
        (pb_ref, mu_ref, w0_ref, wup_ref, a0_ref, aup_ref, gup_ref, kk_ref, ka_ref, rk_ref,
         lnw_ref, lnb_ref, o_ref, vraw_ref, state_ref, carry_ref) = refs
    nb, c, ncol = pb_ref.shape
    n = B_HEAD_SIZE
    rows = nb * c

    @pl.when(pl.program_id(0) == 0)
    def _():
        state_ref[...] = jnp.zeros_like(state_ref)
        carry_ref[...] = jnp.zeros_like(carry_ref)

    p = pb_ref[...].reshape(rows, ncol)
    row = lax.broadcasted_iota(jnp.int32, (rows, 1), 0)
    prev = pltpu.roll(p, 1, axis=0)
    for b in range(nb):
        prev = jnp.where(row == b * c, carry_ref[b, 0:1, :], prev)
    for b in range(nb):
        carry_ref[b, 0:1, :] = p[(b + 1) * c - 1:(b + 1) * c, :]
    ps = p + (prev - p) * mu_ref[...]
    r = ps[:, 0:B_WIDTH]
    k = ps[:, B_WIDTH:2 * B_WIDTH]
    v = ps[:, 2 * B_WIDTH:3 * B_WIDTH]
    wad = ps[:, 3 * B_WIDTH:3 * B_WIDTH + W_LORA + A_LORA]
    gd = ps[:, 3 * B_WIDTH + W_LORA + A_LORA:]
    if has_vmix:
        mix = _sigmoid(v0_ref[...] + _mm(_mm(v, vdn_ref[...]), vup_ref[...]))
        v = v + (vf_ref[...].reshape(rows, B_WIDTH) - v) * mix
    else:
        vraw_ref[...] = v.reshape(nb, c, B_WIDTH)
    wlog = -DECAY_SCALE * _sigmoid(w0_ref[...] + _mm(jnp.tanh(wad), wup_ref[...]))
    a = _sigmoid(a0_ref[...] + _mm(wad, aup_ref[...]))
    g = _mm(_sigmoid(gd), gup_ref[...])
    kkf = k * kk_ref[...]
    kka = kkf * a
    kmod = k * (1.0 + (a - 1.0) * ka_ref[...])
    rkr = r * kmod * rk_ref[...]

    ti = lax.broadcasted_iota(jnp.int32, (c, 3 * c), 0)
    tj = lax.broadcasted_iota(jnp.int32, (c, 3 * c), 1) % c
    tri3 = jnp.where(tj <= ti, 1.0, 0.0).astype(BF16)
    lcum, lend = [], []
    for b in range(nb):
        lb = jnp.dot(tri3, jnp.concatenate(_split3(wlog[b * c:(b + 1) * c]), axis=0),
                     preferred_element_type=F32)
        lcum.append(lb)
        lend.append(jnp.broadcast_to(lb[c - 1:c, :], lb.shape))
    lcum = jnp.concatenate(lcum, axis=0)
    lend = jnp.concatenate(lend, axis=0)
    e_neg = jnp.exp(-lcum)
    e_end = jnp.exp(lend - lcum)
    w_end = jnp.exp(lend)

    pair = 2 * n
    npair = B_HEADS // 2
    lo = lax.broadcasted_iota(jnp.int32, (1, 1, pair), 2) < n

    def slabs(q):
        return jnp.stack([q[b * c:(b + 1) * c, p * pair:(p + 1) * pair]
                          for b in range(nb) for p in range(npair)], axis=0)

    def unslabs(q):
        return jnp.concatenate(
            [jnp.concatenate([q[b * npair + p] for p in range(npair)], axis=-1) for b in range(nb)], axis=0)

    def head_sum(q):
        s_lo = jnp.sum(jnp.where(lo, q, 0.0), axis=-1, keepdims=True)
        s_hi = jnp.sum(jnp.where(lo, 0.0, q), axis=-1, keepdims=True)
        return jnp.where(lo, s_lo, s_hi)

    def bd(q):
        return jnp.concatenate([jnp.where(lo, q, 0.0), jnp.where(lo, 0.0, q)], axis=1)

    def bmm(lhs, rhs):
        return jnp.einsum("gij,gjv->giv", lhs.astype(BF16), rhs.astype(BF16), preferred_element_type=F32)

    kk_s = slabs(kkf)
    rs = lax.rsqrt(jnp.maximum(head_sum(kk_s * kk_s), 1e-24))
    x = jnp.concatenate([slabs(-kkf * jnp.exp(lcum - wlog)) * rs, slabs(r * jnp.exp(lcum))], axis=1)
    bt = slabs(kka * e_neg) * rs
    kt = slabs(kmod * e_neg)
    bk_end = jnp.concatenate([slabs(kka * e_end) * rs, slabs(kmod * e_end)], axis=1)
    vh = slabs(v)
    s0 = state_ref[...]
    xb = x.astype(BF16)
    ybd = jnp.concatenate([bd(bt), bd(kt)], axis=1).astype(BF16)
    gm = jnp.einsum("gik,gjk->gij", xb, ybd, preferred_element_type=F32)
    xs = jnp.einsum("gik,gvk->giv", xb, s0.astype(BF16), preferred_element_type=F32)
    ri = lax.broadcasted_iota(jnp.int32, (1, c, 2 * c), 1)
    ci = lax.broadcasted_iota(jnp.int32, (1, c, 2 * c), 2) % c
    strict = ci < ri
    incl = ci <= ri
    m_ab = jnp.where(strict, gm[:, :c, :2 * c], 0.0)
    m_ak = jnp.where(strict, gm[:, :c, 2 * c:], 0.0)
    m_rb = jnp.where(incl, gm[:, c:, :2 * c], 0.0)
    m_rk = jnp.where(incl, gm[:, c:, 2 * c:], 0.0)
    bdv = bd(vh)
    u = xs[:, :c] + bmm(m_ak, bdv)
    pw = m_ab
    span = 1
    while True:
        u = u + bmm(pw, bd(u))
        span *= 2
        if span >= c:
            break
        pw = bmm(pw, bd(pw))
    o = xs[:, c:] + bmm(jnp.concatenate([m_rb, m_rk], axis=2), jnp.concatenate([bd(u), bdv], axis=1))
    w_end_s = jnp.stack([w_end[b * c:b * c + 1, p * pair:(p + 1) * pair]
                         for b in range(nb) for p in range(npair)], axis=0)
    uv = jnp.concatenate([u, vh], axis=1)
    upd = jnp.einsum("gjv,gjk->gvk", uv.astype(BF16), bk_end.astype(BF16), preferred_element_type=F32)
    same_head = (lax.broadcasted_iota(jnp.int32, (1, pair, pair), 1) < n) == lo
    state_ref[...] = s0 * w_end_s + jnp.where(same_head, upd, 0.0)
    mean = head_sum(o) * (1.0 / n)
    d = o - mean
    var = head_sum(d * d) * (1.0 / n)
    o = d * lax.rsqrt(var + GN_EPS)
    bonus = head_sum(slabs(rkr)) * vh
    out = (unslabs(o) * lnw_ref[...] + lnb_ref[...] + unslabs(bonus)) * g
    o_ref[...] = out.astype(BF16).reshape(nb, c, B_WIDTH)


def _rwkv(pb, v_first, prm, vmix):
    b, s, ncol = pb.shape
    c = min(RWKV_CHUNK, s)
    has_vmix = vmix is not None
    tok = lambda w: pl.BlockSpec((b, c, w), lambda i: (0, i, 0))
    vec = lambda a: a.reshape(1, -1).astype(F32)
    lw = W_LORA + A_LORA
    wup = jnp.zeros((lw, B_WIDTH), F32).at[:W_LORA].set(prm["w_up"]).astype(BF16)
    aup = jnp.zeros((lw, B_WIDTH), F32).at[W_LORA:].set(prm["a_up"]).astype(BF16)
    common = [vec(prm["mu"]), vec(prm["w0"]), wup, vec(prm["a0"]), aup, prm["g_up"].astype(BF16),
              vec(prm["k_k"]), vec(prm["k_a"]), vec(prm["r_k"]), vec(prm["ln_w"]), vec(prm["ln_b"])]
    if has_vmix:
        v0, v_down, v_up = vmix
        vdn = jnp.zeros((B_WIDTH, V7X_LANES), F32).at[:, :V_LORA].set(v_down).astype(BF16)
        vup = jnp.zeros((V7X_LANES, B_WIDTH), F32).at[:V_LORA].set(v_up).astype(BF16)
        args = [pb, v_first] + common + [vec(v0), vdn, vup]
        in_specs = [tok(ncol), tok(B_WIDTH)] + [_const_spec(a.shape) for a in args[2:]]
        out_shape = jax.ShapeDtypeStruct((b, s, B_WIDTH), BF16)
        out_specs = tok(B_WIDTH)
    else:
        args = [pb] + common
        in_specs = [tok(ncol)] + [_const_spec(a.shape) for a in args[1:]]
        out_shape = (jax.ShapeDtypeStruct((b, s, B_WIDTH), BF16), jax.ShapeDtypeStruct((b, s, B_WIDTH), F32))
        out_specs = (tok(B_WIDTH), tok(B_WIDTH))
    return pl.pallas_call(
        functools.partial(_rwkv_body, has_vmix=has_vmix),
        out_shape=out_shape,
        grid=(s // c,),
        in_specs=in_specs,
        out_specs=out_specs,
        scratch_shapes=[pltpu.VMEM((b * B_HEADS // 2, 2 * B_HEAD_SIZE, 2 * B_HEAD_SIZE), F32),
                        pltpu.VMEM((b, V7X_SUBLANES, ncol), F32)],
        compiler_params=_params("arbitrary"),
        name="rwkv_mix" if has_vmix else "rwkv",
    )(*args)


def _gelu_tanh(x):
    return 0.5 * x * (1.0 + jnp.tanh(np.sqrt(2.0 / np.pi).astype(np.float32) * (x + 0.044715 * (x * x * x))))


def _rglru_tile(bi, x_ref, g_ref, win_ref, cw_ref, cb_ref, wax_ref, ba_ref, bx_ref, lam_ref, wout_ref,
                o_ref, xprev_ref, hprev_ref):
    tr = x_ref.shape[1]
    sub = V7X_SUBLANES
    nrow = tr // sub
    ntail = CONV_W - 1
    x = x_ref[bi]
    h = _rms(x, g_ref[...]).astype(BF16)
    i0 = lax.broadcasted_iota(jnp.int32, (tr, tr), 0)
    i1 = lax.broadcasted_iota(jnp.int32, (tr, tr), 1)
    perm = jnp.where(i1 == (i0 % sub) * nrow + i0 // sub, 1.0, 0.0).astype(BF16)
    unperm = jnp.where(i0 == (i1 % sub) * nrow + i1 // sub, 1.0, 0.0).astype(BF16)
    hp = jnp.dot(perm, h, preferred_element_type=F32).astype(BF16)
    gate = _gelu_tanh(jnp.dot(hp, win_ref[:, :D_RNN], preferred_element_type=F32))
    xb3 = jnp.dot(hp, win_ref[:, D_RNN:], preferred_element_type=F32).reshape(nrow, sub, D_RNN)
    seg = lax.broadcasted_iota(jnp.int32, (1, sub, 1), 1)
    prev3 = xprev_ref[bi]
    tails = [jnp.where(seg == 0, pltpu.roll(prev3[q:q + 1], 1, axis=1),
                       pltpu.roll(xb3[nrow - ntail + q:nrow - ntail + q + 1], 1, axis=1)) for q in range(ntail)]
    xprev_ref[bi] = xb3[nrow - ntail:]
    xc3 = xb3 * cw_ref[CONV_W - 1:CONV_W, :] + cb_ref[...]
    for sft in range(1, CONV_W):
        shifted = jnp.concatenate(tails[ntail - sft:] + [xb3[:nrow - sft]], axis=0)
        xc3 = xc3 + shifted * cw_ref[CONV_W - 1 - sft:CONV_W - sft, :]
    xc = xc3.reshape(tr, D_RNN)
    xcb = xc.astype(BF16)
    pw = 2 * C_BLOCK_W
    ra, rx = [], []
    for q in range(C_BLOCKS // 2):
        t = jnp.dot(xcb[:, q * pw:(q + 1) * pw], wax_ref[q], preferred_element_type=F32)
        ra.append(t[:, :pw])
        rx.append(t[:, pw:])
    rgate = _sigmoid(jnp.concatenate(ra, axis=-1) + ba_ref[...])
    igate = _sigmoid(jnp.concatenate(rx, axis=-1) + bx_ref[...])
    z = -lam_ref[...]
    softplus = jnp.maximum(z, 0.0) + jnp.log(1.0 + jnp.exp(-jnp.abs(z)))
    a = jnp.exp(-RG_C * rgate * softplus)
    mult = jnp.sqrt(jnp.maximum(1.0 - a * a, 0.0))
    bb = mult * igate * xc
    a3 = a.reshape(nrow, sub, D_RNN)
    b3 = bb.reshape(nrow, sub, D_RNN)
    acs, bcs = [a3[0]], [b3[0]]
    for p in range(1, nrow):
        bcs.append(a3[p] * bcs[-1] + b3[p])
        acs.append(a3[p] * acs[-1])
    ga, gb = acs[-1], bcs[-1]
    srow = lax.broadcasted_iota(jnp.int32, (sub, 1), 0)
    sft = 1
    while sft < sub:
        keep = srow >= sft
        a_sh = jnp.where(keep, pltpu.roll(ga, sft, axis=0), 1.0)
        b_sh = jnp.where(keep, pltpu.roll(gb, sft, axis=0), 0.0)
        gb = ga * b_sh + gb
        ga = ga * a_sh
        sft *= 2
    hprev = hprev_ref[bi, 0:1, :]
    hend = ga * hprev + gb
    hin = jnp.where(srow == 0, hprev, pltpu.roll(hend, 1, axis=0))
    hprev_ref[bi, 0:1, :] = hend[sub - 1:sub, :]
    hs = jnp.stack([acs[p] * hin + bcs[p] for p in range(nrow)], axis=0).reshape(tr, D_RNN)
    yp = (hs * gate).astype(BF16)
    y = jnp.dot(unperm, yp, preferred_element_type=F32).astype(BF16)
    o_ref[bi] = x + jnp.dot(y, wout_ref[...], preferred_element_type=F32)


def _rglru_body(x_ref, *rest):
    xprev_ref, hprev_ref = rest[-2:]

    @pl.when(pl.program_id(1) == 0)
    def _():
        xprev_ref[...] = jnp.zeros_like(xprev_ref)
        hprev_ref[...] = jnp.zeros_like(hprev_ref)

    for bi in range(x_ref.shape[0]):
        _rglru_tile(bi, x_ref, *rest)


def _rglru(x, g, w_in, conv_w, conv_b, wa, ba, wx, bx, lam, w_out, lead):
    b, s, d = x.shape
    tr = min(RGLRU_ROWS, s)
    pw = 2 * C_BLOCK_W
    wax = jnp.zeros((C_BLOCKS // 2, pw, 2 * pw), F32)
    for half in range(2):
        sl = slice(half * C_BLOCK_W, (half + 1) * C_BLOCK_W)
        wax = wax.at[:, sl, sl].set(wa[half::2])
        wax = wax.at[:, sl, pw + half * C_BLOCK_W:pw + (half + 1) * C_BLOCK_W].set(wx[half::2])
    wax = wax.astype(BF16)
    vec = lambda a: a.reshape(1, -1).astype(F32)
    args = [x, vec(g), w_in, conv_w.astype(F32), vec(conv_b), wax, vec(ba), vec(bx), vec(lam), w_out]
    nb = min(RGLRU_BATCH, b)
    tok = pl.BlockSpec((nb, tr, d), lambda bi, i: (bi, i, 0))
    return pl.pallas_call(
        _rglru_body,
        out_shape=jax.ShapeDtypeStruct((b, s, d), F32),
        grid=(b // nb, s // tr),
        in_specs=([tok, _const_spec(args[1].shape), _pick_spec(w_in.shape, lead)]
                  + [_const_spec(a.shape) for a in args[3:-1]] + [_pick_spec(w_out.shape, lead)]),
        out_specs=tok,
        scratch_shapes=[pltpu.VMEM((nb, CONV_W - 1, V7X_SUBLANES, D_RNN), F32),
                        pltpu.VMEM((nb, V7X_SUBLANES, D_RNN), F32)],
        compiler_params=_params("parallel", "arbitrary"),
        name="rglru",
    )(*args)


def kernel(x, norm_ffn, ffn_w_gate, ffn_w_up, ffn_w_down, norm_mix, ab_w_in, ab_w_out, att_rel_bias, rwkv_mu, rwkv_w0, rwkv_w_up, rwkv_a0, rwkv_a_up, rwkv_g_up, rwkv_k_k, rwkv_k_a, rwkv_r_k, rwkv_ln_w, rwkv_ln_b, rwkv_v0, rwkv_v_down, rwkv_v_up, c_w_in, c_conv_w, c_conv_b, c_wa, c_ba, c_wx, c_bx, c_lambda, c_w_out, norm_final):
    b, s, d = x.shape
    t = b * s
    depth = norm_mix.shape[0]
    ffn_w = (ffn_w_gate.astype(F32), ffn_w_up.astype(F32), ffn_w_down.astype(F32))
    ffn_calls = [(l, j) for l in range(depth) for j in range(2)]
    wb = tuple(w[0, 0].astype(BF16) for w in ffn_w)

    def ffn(x, wb, lead, mixer=None):
        idx = ffn_calls.index(lead)
        nxt = ffn_w + (ffn_calls[idx + 1],) if idx + 1 < len(ffn_calls) else None
        return _ffn(x, norm_ffn[lead].reshape(1, d), wb, gfin, nxt is None, mixer, nxt)

    w_in, w_out = ab_w_in.astype(BF16), ab_w_out.astype(BF16)
    cw_in, cw_out = c_w_in.astype(BF16), c_w_out.astype(BF16)
    gfin = norm_final.reshape(1, d).astype(F32)
    x = x.astype(F32).reshape(t, d)
    v_first = None
    for l in range(depth):
        x, wb = ffn(x, wb, (l, 0))
        mixer = None
        if l % 2 == 0:
            i = l // 2
            qkv, pb = _ab_in(x, norm_mix[l].reshape(1, d), w_in, (i,))
            oa = _attention(qkv.reshape(b, s, -1), att_rel_bias[i])
            prm = dict(mu=rwkv_mu[i], w0=rwkv_w0[i], w_up=rwkv_w_up[i], a0=rwkv_a0[i], a_up=rwkv_a_up[i],
                       g_up=rwkv_g_up[i], k_k=rwkv_k_k[i], k_a=rwkv_k_a[i], r_k=rwkv_r_k[i],
                       ln_w=rwkv_ln_w[i], ln_b=rwkv_ln_b[i])
            if i == 0:
                ob, v_first = _rwkv(pb.reshape(b, s, -1), None, prm, None)
            else:
                ob = _rwkv(pb.reshape(b, s, -1), v_first, prm,
                           (rwkv_v0[i - 1], rwkv_v_down[i - 1], rwkv_v_up[i - 1]))
            mixer = (oa.reshape(t, -1), ob.reshape(t, -1), w_out, (i,))
        else:
            j = l // 2
            x = _rglru(x.reshape(b, s, d), norm_mix[l], cw_in, c_conv_w[j], c_conv_b[j], c_wa[j], c_ba[j],
                       c_wx[j], c_bx[j], c_lambda[j], cw_out, (j,)).reshape(t, d)
        x, wb = ffn(x, wb, (l, 1), mixer)
    return x.reshape(b, s, d)
```

```python
import functools

import jax
import jax.numpy as jnp
import numpy as np
from jax import lax
from jax.experimental import pallas as pl
from jax.experimental.pallas import tpu as pltpu

F32 = jnp.float32
BF16 = jnp.bfloat16

D_MODEL = 1024
DEPTH = 4
CHUNK = 64
LEFT_CHUNKS = 8
A_HEADS = 8
A_HEAD_DIM = 64
A_WIDTH = A_HEADS * A_HEAD_DIM
REL_CLIP = 128
B_HEADS = 8
B_HEAD_SIZE = 64
B_WIDTH = B_HEADS * B_HEAD_SIZE
W_LORA = 64
A_LORA = 64
G_LORA = 128
V_LORA = 32
DECAY_SCALE = 0.606531
GN_EPS = 64e-5
B_COLS = 3 * B_WIDTH + W_LORA + A_LORA + G_LORA
D_RNN = 1280
C_BLOCKS = 10
C_BLOCK_W = D_RNN // C_BLOCKS
CONV_W = 4
RG_C = 8.0
D_FF = 2816
NORM_EPS = 1e-6
NEG_INF = -1e30

V7X_LANES = 128
V7X_SUBLANES = 8
V7X_BF16_SUBLANES = 16
V7X_MXU_DIM = 256
V7X_VMEM_BYTES = 64 * 1024 * 1024

FFN_ROWS = 512
FFN_CHUNKS = (0, 768, 1536, 2304, 2816)
PROJ_ROWS = 1024
ATT_ROWS = 256
ATT_PREV = LEFT_CHUNKS * CHUNK
RWKV_CHUNK = 64
RGLRU_ROWS = 256
RGLRU_BATCH = 1
VMEM_LIMIT = 56 * 1024 * 1024


def _params(*sem):
    return pltpu.CompilerParams(dimension_semantics=sem, vmem_limit_bytes=VMEM_LIMIT)


def _const_spec(shape):
    nd = len(shape)
    return pl.BlockSpec(shape, lambda *_: (0,) * nd, pipeline_mode=pl.Buffered(1))


def _mm(a, b):
    return jnp.dot(a.astype(BF16), b.astype(BF16), preferred_element_type=F32)


def _mm_nt(a, b):
    return lax.dot_general(a.astype(BF16), b.astype(BF16), (((1,), (1,)), ((), ())),
                           preferred_element_type=F32)


def _mm_tn(a, b):
    return lax.dot_general(a.astype(BF16), b.astype(BF16), (((0,), (0,)), ((), ())),
                           preferred_element_type=F32)


def _rms(x, g):
    ms = jnp.mean(x * x, axis=-1, keepdims=True)
    return x * lax.rsqrt(ms + NORM_EPS) * g


def _sigmoid(x):
    return 0.5 * jnp.tanh(0.5 * x) + 0.5


def _slice_rows(total, steps):
    k = 1
    while (total * k) % steps or (total * k // steps) % V7X_BF16_SUBLANES:
        k *= 2
    return total * k // steps, k


def _ffn_body(*refs, final_norm, mixer_out, n_casts):
    refs = list(refs)
    if mixer_out:
        oa_ref, ob_ref, wo_ref = refs[:3]
        refs = refs[3:]
    x_ref, g_ref, wg_ref, wu_ref, wd_ref, gf_ref = refs[:6]
    refs = refs[6:]
    o_ref = refs[n_casts]
    for src, dst in zip(refs[:n_casts], refs[n_casts + 1:]):
        dst[...] = src[...].astype(BF16)
    x = x_ref[...]
    if mixer_out:
        x = x + jnp.dot(oa_ref[...], wo_ref[:A_WIDTH, :], preferred_element_type=F32)
        x = x + jnp.dot(ob_ref[...], wo_ref[A_WIDTH:, :], preferred_element_type=F32)
    h = _rms(x, g_ref[...]).astype(BF16)
    acc = None
    for lo, hi in zip(FFN_CHUNKS[:-1], FFN_CHUNKS[1:]):
        gt = jnp.dot(h, wg_ref[:, lo:hi], preferred_element_type=F32)
        up = jnp.dot(h, wu_ref[:, lo:hi], preferred_element_type=F32)
        act = (gt * _sigmoid(gt) * up).astype(BF16)
        d = jnp.dot(act, wd_ref[lo:hi, :], preferred_element_type=F32)
        acc = d if acc is None else acc + d
    y = x + 0.5 * acc
    if final_norm:
        y = _rms(y, gf_ref[...])
    o_ref[...] = y


def _ffn(x, g, wb, gf, final_norm, mixer=None, casts=()):
    t, d = x.shape
    tm = min(FFN_ROWS, t)
    steps = t // tm
    row = pl.BlockSpec((tm, d), lambda i: (i, 0))
    args = [x, g, *wb, gf]
    in_specs = [row, _const_spec((1, d))] + [_const_spec(w.shape) for w in wb] + [_const_spec((1, d))]
    if mixer is not None:
        oa, ob, wo = mixer
        args = [oa, ob, wo] + args
        in_specs = [pl.BlockSpec((tm, oa.shape[1]), lambda i: (i, 0)), pl.BlockSpec((tm, ob.shape[1]), lambda i: (i, 0)),
                    _const_spec(wo.shape)] + in_specs
    out_shape = [jax.ShapeDtypeStruct((t, d), F32)]
    out_specs = [row]
    for w, lead in casts:
        lead = tuple(lead)
        nr, nc = w.shape[-2:]
        rows, k = _slice_rows(nr, steps)
        args.append(w)
        in_specs.append(pl.BlockSpec((None,) * len(lead) + (rows, nc), lambda i, k=k, lead=lead: lead + (i // k, 0)))
        out_shape.append(jax.ShapeDtypeStruct((nr, nc), BF16))
        out_specs.append(pl.BlockSpec((rows, nc), lambda i, k=k: (i // k, 0)))
    out = pl.pallas_call(
        functools.partial(_ffn_body, final_norm=final_norm, mixer_out=mixer is not None, n_casts=len(casts)),
        out_shape=tuple(out_shape),
        grid=(steps,),
        in_specs=in_specs,
        out_specs=tuple(out_specs),
        compiler_params=_params("arbitrary"),
        name="ffn_final" if final_norm else ("ffn_mix" if mixer is not None else "ffn"),
    )(*args)
    return out[0], tuple(out[1:])


def _ab_in_body(x_ref, g_ref, w_ref, qkv_ref, pb_ref):
    h = _rms(x_ref[...], g_ref[...]).astype(BF16)
    na = qkv_ref.shape[-1]
    qkv_ref[...] = jnp.dot(h, w_ref[:, :na], preferred_element_type=F32).astype(BF16)
    pb_ref[...] = jnp.dot(h, w_ref[:, na:], preferred_element_type=F32)


def _ab_in(x, g, w):
    t, d = x.shape
    n = w.shape[-1]
    na = 3 * A_WIDTH
    tm = min(PROJ_ROWS, t)
    return pl.pallas_call(
        _ab_in_body,
        out_shape=(jax.ShapeDtypeStruct((t, na), BF16), jax.ShapeDtypeStruct((t, n - na), F32)),
        grid=(t // tm,),
        in_specs=[pl.BlockSpec((tm, d), lambda i: (i, 0)), _const_spec((1, d)), _const_spec(w.shape)],
        out_specs=(pl.BlockSpec((tm, na), lambda i: (i, 0)), pl.BlockSpec((tm, n - na), lambda i: (i, 0))),
        compiler_params=_params("parallel"),
        name="ab_in",
    )(x, g, w)


def _attn_body(*refs, n_kblocks):
    q_ref = refs[0]
    k_refs = refs[1:1 + n_kblocks]
    v_refs = refs[1 + n_kblocks:1 + 2 * n_kblocks]
    bias_ref = refs[1 + 2 * n_kblocks]
    o_ref = refs[2 + 2 * n_kblocks]
    qb = q_ref.shape[0]
    i = pl.program_id(1)
    n_prev = n_kblocks - 1
    pens = [jnp.where(i >= n_prev - j, 0.0, NEG_INF).astype(F32) for j in range(n_prev)] + [None]
    lane = lax.broadcasted_iota(jnp.int32, (1, 2 * A_HEAD_DIM), 1)
    first = lane < A_HEAD_DIM
    scale = A_HEAD_DIM ** -0.5
    for pr in range(A_HEADS // 2):
        ls = slice(2 * A_HEAD_DIM * pr, 2 * A_HEAD_DIM * (pr + 1))
        q_pair = q_ref[:, ls]
        ks = [r[:, ls] for r in k_refs]
        vs = [r[:, ls] for r in v_refs]
        outs = []
        for hh in range(2):
            head = 2 * pr + hh
            sel = first if hh == 0 else jnp.logical_not(first)
            qm = jnp.where(sel, scale, 0.0).astype(BF16)
            qh = q_pair * qm
            s = []
            for j in range(n_kblocks):
                sj = lax.dot_general(qh, ks[j], (((1,), (1,)), ((), ())), preferred_element_type=F32)
                sj = sj + bias_ref[head, :, j * qb:(j + 1) * qb]
                if pens[j] is not None:
                    sj = sj + pens[j]
                s.append(sj)
            m = s[0].max(axis=-1, keepdims=True)
            for sj in s[1:]:
                m = jnp.maximum(m, sj.max(axis=-1, keepdims=True))
            l = None
            pv = None
            for j in range(n_kblocks):
                p = jnp.exp(s[j] - m)
                lj = p.sum(axis=-1, keepdims=True)
                pvj = jnp.dot(p.astype(BF16), vs[j], preferred_element_type=F32)
                l = lj if l is None else l + lj
                pv = pvj if pv is None else pv + pvj
            outs.append(pv / l)
        o_ref[:, ls] = jnp.where(first, outs[0], outs[1]).astype(BF16)


def _attn_bias_tile(rel_bias, qb):
    nh = rel_bias.shape[0]
    w = ATT_PREV + qb
    period = w + qb
    near = 2 * REL_CLIP + 1
    assert ATT_PREV >= REL_CLIP and qb >= REL_CLIP
    far_past = jnp.broadcast_to(rel_bias[:, near - 1:], (nh, ATT_PREV - REL_CLIP))
    far_future = jnp.broadcast_to(rel_bias[:, :1], (nh, qb - REL_CLIP))
    wrapped = jnp.broadcast_to(rel_bias[:, near - 1:], (nh, qb - 1))
    t = jnp.concatenate([far_past, rel_bias[:, ::-1], far_future, wrapped], axis=1).astype(F32)
    assert t.shape[1] == period
    toe = jnp.broadcast_to(t[:, None, :], (nh, qb, period)).reshape(nh, qb * period)
    toe = toe[:, :qb * (period - 1)].reshape(nh, qb, period - 1)[:, :, :w]
    r = np.arange(qb)[:, None]
    c = np.arange(w)[None, :]
    band = (c // CHUNK >= r // CHUNK) & (c // CHUNK <= r // CHUNK + LEFT_CHUNKS)
    return jnp.where(band[None], toe, NEG_INF)


def _attention(qkv, rel_bias):
    b, s, _ = qkv.shape
    qb = min(ATT_ROWS, s)
    assert ATT_PREV % qb == 0 and s % qb == 0
    n_prev = ATT_PREV // qb
    n_kblocks = n_prev + 1
    bias = _attn_bias_tile(rel_bias, qb)

    def blk(col, back):
        return pl.BlockSpec((None, qb, A_WIDTH), lambda bi, i: (bi, jnp.maximum(i - back, 0), col))

    in_specs = ([blk(0, 0)] + [blk(1, n_prev - j) for j in range(n_kblocks)]
                + [blk(2, n_prev - j) for j in range(n_kblocks)] + [_const_spec(bias.shape)])
    return pl.pallas_call(
        functools.partial(_attn_body, n_kblocks=n_kblocks),
        out_shape=jax.ShapeDtypeStruct((b, s, A_WIDTH), BF16),
        grid=(b, s // qb),
        in_specs=in_specs,
        out_specs=pl.BlockSpec((None, qb, A_WIDTH), lambda bi, i: (bi, i, 0)),
        compiler_params=_params("parallel", "parallel"),
        name="attention",
    )(*([qkv] * (1 + 2 * n_kblocks)), bias)


def _split3(x):
    h1 = x.astype(BF16)
    r1 = x - h1.astype(F32)
    h2 = r1.astype(BF16)
    h3 = (r1 - h2.astype(F32)).astype(BF16)
    return h1, h2, h3


def _rwkv_body(*refs, has_vmix):
    if has_vmix:
        (pb_ref, vf_ref, mu_ref, w0_ref, wup_ref, a0_ref, aup_ref, gup_ref, kk_ref, ka_ref, rk_ref,
         lnw_ref, lnb_ref, v0_ref, vdn_ref, vup_ref, o_ref, state_ref, carry_ref) = refs
        vraw_ref = None
    else:
        (pb_ref, mu_ref, w0_ref, wup_ref, a0_ref, aup_ref, gup_ref, kk_ref, ka_ref, rk_ref,
         lnw_ref, lnb_ref, o_ref, vraw_ref, state_ref, carry_ref) = refs
    nb, c, ncol = pb_ref.shape
    n = B_HEAD_SIZE
    rows = nb * c

    @pl.when(pl.program_id(0) == 0)
    def _():
        state_ref[...] = jnp.zeros_like(state_ref)
        carry_ref[...] = jnp.zeros_like(carry_ref)

    p = pb_ref[...].reshape(rows, ncol)
    row = lax.broadcasted_iota(jnp.int32, (rows, 1), 0)
    prev = pltpu.roll(p, 1, axis=0)
    for b in range(nb):
        prev = jnp.where(row == b * c, carry_ref[b, 0:1, :], prev)
    for b in range(nb):
        carry_ref[b, 0:1, :] = p[(b + 1) * c - 1:(b + 1) * c, :]
    ps = p + (prev - p) * mu_ref[...]
    r = ps[:, 0:B_WIDTH]
    k = ps[:, B_WIDTH:2 * B_WIDTH]
    v = ps[:, 2 * B_WIDTH:3 * B_WIDTH]
    wad = ps[:, 3 * B_WIDTH:3 * B_WIDTH + W_LORA + A_LORA]
    gd = ps[:, 3 * B_WIDTH + W_LORA + A_LORA:]
    if has_vmix:
        mix = _sigmoid(v0_ref[...] + _mm(_mm(v, vdn_ref[...]), vup_ref[...]))
        v = v + (vf_ref[...].reshape(rows, B_WIDTH) - v) * mix
    else:
        vraw_ref[...] = v.reshape(nb, c, B_WIDTH)
    wlog = -DECAY_SCALE * _sigmoid(w0_ref[...] + _mm(jnp.tanh(wad), wup_ref[...]))
    a = _sigmoid(a0_ref[...] + _mm(wad, aup_ref[...]))
    g = _mm(_sigmoid(gd), gup_ref[...])
    kkf = k * kk_ref[...]
    kka = kkf * a
    kmod = k * (1.0 + (a - 1.0) * ka_ref[...])
    rkr = r * kmod * rk_ref[...]

    ti = lax.broadcasted_iota(jnp.int32, (c, 3 * c), 0)
    tj = lax.broadcasted_iota(jnp.int32, (c, 3 * c), 1) % c
    tri3 = jnp.where(tj <= ti, 1.0, 0.0).astype(BF16)
    lcum, lend = [], []
    for b in range(nb):
        lb = jnp.dot(tri3, jnp.concatenate(_split3(wlog[b * c:(b + 1) * c]), axis=0),
                     preferred_element_type=F32)
        lcum.append(lb)
        lend.append(jnp.broadcast_to(lb[c - 1:c, :], lb.shape))
    lcum = jnp.concatenate(lcum, axis=0)
    lend = jnp.concatenate(lend, axis=0)
    e_neg = jnp.exp(-lcum)
    e_end = jnp.exp(lend - lcum)
    w_end = jnp.exp(lend)

    pair = 2 * n
    npair = B_HEADS // 2
    lo = lax.broadcasted_iota(jnp.int32, (1, 1, pair), 2) < n

    def slabs(q):
        return jnp.stack([q[b * c:(b + 1) * c, p * pair:(p + 1) * pair]
                          for b in range(nb) for p in range(npair)], axis=0)

    def unslabs(q):
        return jnp.concatenate(
            [jnp.concatenate([q[b * npair + p] for p in range(npair)], axis=-1) for b in range(nb)], axis=0)

    def head_sum(q):
        s_lo = jnp.sum(jnp.where(lo, q, 0.0), axis=-1, keepdims=True)
        s_hi = jnp.sum(jnp.where(lo, 0.0, q), axis=-1, keepdims=True)
        return jnp.where(lo, s_lo, s_hi)

    def bd(q):
        qb = q.astype(BF16)
        zero = jnp.zeros_like(qb)
        return jnp.concatenate([jnp.where(lo, qb, zero), jnp.where(lo, zero, qb)], axis=1)

    def bmm(lhs, rhs):
        return jnp.einsum("gij,gjv->giv", lhs.astype(BF16), rhs.astype(BF16), preferred_element_type=F32)

    kk_s = slabs(kkf)
    rs = lax.rsqrt(jnp.maximum(head_sum(kk_s * kk_s), 1e-24))
    x = jnp.concatenate([slabs(-kkf * jnp.exp(lcum - wlog)) * rs, slabs(r * jnp.exp(lcum))], axis=1)
    bt = slabs(kka * e_neg) * rs
    kt = slabs(kmod * e_neg)
    bk_end = jnp.concatenate([slabs(kka * e_end) * rs, slabs(kmod * e_end)], axis=1)
    vh = slabs(v)
    s0 = state_ref[...]
    xb = x.astype(BF16)
    ybd = jnp.concatenate([bd(bt), bd(kt)], axis=1).astype(BF16)
    gm = jnp.einsum("gik,gjk->gij", xb, ybd, preferred_element_type=F32)
    xs = jnp.einsum("gik,gvk->giv", xb, s0.astype(BF16), preferred_element_type=F32)
    ri = lax.broadcasted_iota(jnp.int32, (1, c, 2 * c), 1)
    ci = lax.broadcasted_iota(jnp.int32, (1, c, 2 * c), 2) % c
    strict = ci < ri
    incl = ci <= ri
    m_ab = jnp.where(strict, gm[:, :c, :2 * c], 0.0)
    m_ak = jnp.where(strict, gm[:, :c, 2 * c:], 0.0)
    m_rb = jnp.where(incl, gm[:, c:, :2 * c], 0.0)
    m_rk = jnp.where(incl, gm[:, c:, 2 * c:], 0.0)
    bdv = bd(vh)
    u = xs[:, :c] + bmm(m_ak, bdv)
    pw = m_ab.astype(BF16)
    span = 1
    while True:
        u = u + bmm(pw, bd(u))
        span *= 2
        if span >= c:
            break
        pw = bmm(pw, bd(pw)).astype(BF16)
    o = xs[:, c:] + bmm(jnp.concatenate([m_rb, m_rk], axis=2), jnp.concatenate([bd(u), bdv], axis=1))
    w_end_s = jnp.stack([w_end[b * c:b * c + 1, p * pair:(p + 1) * pair]
                         for b in range(nb) for p in range(npair)], axis=0)
    uv = jnp.concatenate([u, vh], axis=1)
    upd = jnp.einsum("gjv,gjk->gvk", uv.astype(BF16), bk_end.astype(BF16), preferred_element_type=F32)
    same_head = (lax.broadcasted_iota(jnp.int32, (1, pair, pair), 1) < n) == lo
    state_ref[...] = s0 * w_end_s + jnp.where(same_head, upd, 0.0)
    mean = head_sum(o) * (1.0 / n)
    d = o - mean
    var = head_sum(d * d) * (1.0 / n)
    o = d * lax.rsqrt(var + GN_EPS)
    bonus = head_sum(slabs(rkr)) * vh
    out = (unslabs(o) * lnw_ref[...] + lnb_ref[...] + unslabs(bonus)) * g
    o_ref[...] = out.astype(BF16).reshape(nb, c, B_WIDTH)


def _rwkv(pb, v_first, prm, vmix):
    b, s, ncol = pb.shape
    c = min(RWKV_CHUNK, s)
    has_vmix = vmix is not None
    tok = lambda w: pl.BlockSpec((b, c, w), lambda i: (0, i, 0))
    vec = lambda a: a.reshape(1, -1).astype(F32)
    lw = W_LORA + A_LORA
    wup = jnp.zeros((lw, B_WIDTH), F32).at[:W_LORA].set(prm["w_up"]).astype(BF16)
    aup = jnp.zeros((lw, B_WIDTH), F32).at[W_LORA:].set(prm["a_up"]).astype(BF16)
    common = [vec(prm["mu"]), vec(prm["w0"]), wup, vec(prm["a0"]), aup, prm["g_up"].astype(BF16),
              vec(prm["k_k"]), vec(prm["k_a"]), vec(prm["r_k"]), vec(prm["ln_w"]), vec(prm["ln_b"])]
    if has_vmix:
        v0, v_down, v_up = vmix
        vdn = jnp.zeros((B_WIDTH, V7X_LANES), F32).at[:, :V_LORA].set(v_down).astype(BF16)
        vup = jnp.zeros((V7X_LANES, B_WIDTH), F32).at[:V_LORA].set(v_up).astype(BF16)
        args = [pb, v_first] + common + [vec(v0), vdn, vup]
        in_specs = [tok(ncol), tok(B_WIDTH)] + [_const_spec(a.shape) for a in args[2:]]
        out_shape = jax.ShapeDtypeStruct((b, s, B_WIDTH), BF16)
        out_specs = tok(B_WIDTH)
    else:
        args = [pb] + common
        in_specs = [tok(ncol)] + [_const_spec(a.shape) for a in args[1:]]
        out_shape = (jax.ShapeDtypeStruct((b, s, B_WIDTH), BF16), jax.ShapeDtypeStruct((b, s, B_WIDTH), F32))
        out_specs = (tok(B_WIDTH), tok(B_WIDTH))
    return pl.pallas_call(
        functools.partial(_rwkv_body, has_vmix=has_vmix),
        out_shape=out_shape,
        grid=(s // c,),
        in_specs=in_specs,
        out_specs=out_specs,
        scratch_shapes=[pltpu.VMEM((b * B_HEADS // 2, 2 * B_HEAD_SIZE, 2 * B_HEAD_SIZE), F32),
                        pltpu.VMEM((b, V7X_SUBLANES, ncol), F32)],
        compiler_params=_params("arbitrary"),
        name="rwkv_mix" if has_vmix else "rwkv",
    )(*args)


def _gelu_tanh(x):
    return 0.5 * x * (1.0 + jnp.tanh(np.sqrt(2.0 / np.pi).astype(np.float32) * (x + 0.044715 * (x * x * x))))


def _rglru_tile(bi, x_ref, g_ref, win_ref, cw_ref, cb_ref, wax_ref, ba_ref, bx_ref, lam_ref, wout_ref,
                o_ref, xprev_ref, hprev_ref):
    tr = x_ref.shape[1]
    sub = V7X_SUBLANES
    nrow = tr // sub
    ntail = CONV_W - 1
    x = x_ref[bi]
    h = _rms(x, g_ref[...]).astype(BF16)
    i0 = lax.broadcasted_iota(jnp.int32, (tr, tr), 0)
    i1 = lax.broadcasted_iota(jnp.int32, (tr, tr), 1)
    perm = jnp.where(i1 == (i0 % sub) * nrow + i0 // sub, 1.0, 0.0).astype(BF16)
    unperm = jnp.where(i0 == (i1 % sub) * nrow + i1 // sub, 1.0, 0.0).astype(BF16)
    hp = jnp.dot(perm, h, preferred_element_type=F32).astype(BF16)
    gate = _gelu_tanh(jnp.dot(hp, win_ref[:, :D_RNN], preferred_element_type=F32))
    xb3 = jnp.dot(hp, win_ref[:, D_RNN:], preferred_element_type=F32).reshape(nrow, sub, D_RNN)
    seg = lax.broadcasted_iota(jnp.int32, (1, sub, 1), 1)
    prev3 = xprev_ref[bi]
    tails = [jnp.where(seg == 0, pltpu.roll(prev3[q:q + 1], 1, axis=1),
                       pltpu.roll(xb3[nrow - ntail + q:nrow - ntail + q + 1], 1, axis=1)) for q in range(ntail)]
    xprev_ref[bi] = xb3[nrow - ntail:]
    xc3 = xb3 * cw_ref[CONV_W - 1:CONV_W, :] + cb_ref[...]
    for sft in range(1, CONV_W):
        shifted = jnp.concatenate(tails[ntail - sft:] + [xb3[:nrow - sft]], axis=0)
        xc3 = xc3 + shifted * cw_ref[CONV_W - 1 - sft:CONV_W - sft, :]
    xc = xc3.reshape(tr, D_RNN)
    xcb = xc.astype(BF16)
    pw = 2 * C_BLOCK_W
    ra, rx = [], []
    for q in range(C_BLOCKS // 2):
        t = jnp.dot(xcb[:, q * pw:(q + 1) * pw], wax_ref[q], preferred_element_type=F32)
        ra.append(t[:, :pw])
        rx.append(t[:, pw:])
    rgate = _sigmoid(jnp.concatenate(ra, axis=-1) + ba_ref[...])
    igate = _sigmoid(jnp.concatenate(rx, axis=-1) + bx_ref[...])
    z = -lam_ref[...]
    softplus = jnp.maximum(z, 0.0) + jnp.log(1.0 + jnp.exp(-jnp.abs(z)))
    a = jnp.exp(-RG_C * rgate * softplus)
    mult = jnp.sqrt(jnp.maximum(1.0 - a * a, 0.0))
    bb = mult * igate * xc
    a3 = a.reshape(nrow, sub, D_RNN)
    b3 = bb.reshape(nrow, sub, D_RNN)
    acs, bcs = [a3[0]], [b3[0]]
    for p in range(1, nrow):
        bcs.append(a3[p] * bcs[-1] + b3[p])
        acs.append(a3[p] * acs[-1])
    ga, gb = acs[-1], bcs[-1]
    srow = lax.broadcasted_iota(jnp.int32, (sub, 1), 0)
    sft = 1
    while sft < sub:
        keep = srow >= sft
        a_sh = jnp.where(keep, pltpu.roll(ga, sft, axis=0), 1.0)
        b_sh = jnp.where(keep, pltpu.roll(gb, sft, axis=0), 0.0)
        gb = ga * b_sh + gb
        ga = ga * a_sh
        sft *= 2
    hprev = hprev_ref[bi, 0:1, :]
    hend = ga * hprev + gb
    hin = jnp.where(srow == 0, hprev, pltpu.roll(hend, 1, axis=0))
    hprev_ref[bi, 0:1, :] = hend[sub - 1:sub, :]
    hs = jnp.stack([acs[p] * hin + bcs[p] for p in range(nrow)], axis=0).reshape(tr, D_RNN)
    yp = (hs * gate).astype(BF16)
    y = jnp.dot(unperm, yp, preferred_element_type=F32).astype(BF16)
    o_ref[bi] = x + jnp.dot(y, wout_ref[...], preferred_element_type=F32)


def _rglru_body(x_ref, *rest):
    xprev_ref, hprev_ref = rest[-2:]

    @pl.when(pl.program_id(1) == 0)
    def _():
        xprev_ref[...] = jnp.zeros_like(xprev_ref)
        hprev_ref[...] = jnp.zeros_like(hprev_ref)

    for bi in range(x_ref.shape[0]):
        _rglru_tile(bi, x_ref, *rest)


def _rglru(x, g, w_in, conv_w, conv_b, wa, ba, wx, bx, lam, w_out):
    b, s, d = x.shape
    tr = min(RGLRU_ROWS, s)
    pw = 2 * C_BLOCK_W
    wax = jnp.zeros((C_BLOCKS // 2, pw, 2 * pw), F32)
    for half in range(2):
        sl = slice(half * C_BLOCK_W, (half + 1) * C_BLOCK_W)
        wax = wax.at[:, sl, sl].set(wa[half::2])
        wax = wax.at[:, sl, pw + half * C_BLOCK_W:pw + (half + 1) * C_BLOCK_W].set(wx[half::2])
    wax = wax.astype(BF16)
    vec = lambda a: a.reshape(1, -1).astype(F32)
    args = [x, vec(g), w_in, conv_w.astype(F32), vec(conv_b), wax, vec(ba), vec(bx), vec(lam), w_out]
    nb = min(RGLRU_BATCH, b)
    tok = pl.BlockSpec((nb, tr, d), lambda bi, i: (bi, i, 0))
    return pl.pallas_call(
        _rglru_body,
        out_shape=jax.ShapeDtypeStruct((b, s, d), F32),
        grid=(b // nb, s // tr),
        in_specs=[tok] + [_const_spec(a.shape) for a in args[1:]],
        out_specs=tok,
        scratch_shapes=[pltpu.VMEM((nb, CONV_W - 1, V7X_SUBLANES, D_RNN), F32),
                        pltpu.VMEM((nb, V7X_SUBLANES, D_RNN), F32)],
        compiler_params=_params("parallel", "arbitrary"),
        name="rglru",
    )(*args)


def kernel(x, norm_ffn, ffn_w_gate, ffn_w_up, ffn_w_down, norm_mix, ab_w_in, ab_w_out, att_rel_bias, rwkv_mu, rwkv_w0, rwkv_w_up, rwkv_a0, rwkv_a_up, rwkv_g_up, rwkv_k_k, rwkv_k_a, rwkv_r_k, rwkv_ln_w, rwkv_ln_b, rwkv_v0, rwkv_v_down, rwkv_v_up, c_w_in, c_conv_w, c_conv_b, c_wa, c_ba, c_wx, c_bx, c_lambda, c_w_out, norm_final):
    b, s, d = x.shape
    t = b * s
    depth = norm_mix.shape[0]
    ffn_w = (ffn_w_gate.astype(F32), ffn_w_up.astype(F32), ffn_w_down.astype(F32))
    ffn_calls = [(l, j) for l in range(depth) for j in range(2)]
    gfin = norm_final.reshape(1, d).astype(F32)
    wb = tuple(w[0, 0].astype(BF16) for w in ffn_w)

    def ffn(x, wb, lead, mixer=None, extra=()):
        idx = ffn_calls.index(lead)
        last = idx + 1 == len(ffn_calls)
        casts = [] if last else [(w, ffn_calls[idx + 1]) for w in ffn_w]
        x, cast = _ffn(x, norm_ffn[lead].reshape(1, d), wb, gfin, last, mixer, casts + list(extra))
        return x, cast[:len(casts)], cast[len(casts):]

    x = x.astype(F32).reshape(t, d)
    v_first = None
    for l in range(depth):
        mixer = None
        if l % 2 == 0:
            i = l // 2
            x, wb, (w_in, w_out) = ffn(x, wb, (l, 0), extra=[(ab_w_in.astype(F32), (i,)), (ab_w_out.astype(F32), (i,))])
            qkv, pb = _ab_in(x, norm_mix[l].reshape(1, d), w_in)
            oa = _attention(qkv.reshape(b, s, -1), att_rel_bias[i])
            prm = dict(mu=rwkv_mu[i], w0=rwkv_w0[i], w_up=rwkv_w_up[i], a0=rwkv_a0[i], a_up=rwkv_a_up[i],
                       g_up=rwkv_g_up[i], k_k=rwkv_k_k[i], k_a=rwkv_k_a[i], r_k=rwkv_r_k[i],
                       ln_w=rwkv_ln_w[i], ln_b=rwkv_ln_b[i])
            if i == 0:
                ob, v_first = _rwkv(pb.reshape(b, s, -1), None, prm, None)
            else:
                ob = _rwkv(pb.reshape(b, s, -1), v_first, prm,
                           (rwkv_v0[i - 1], rwkv_v_down[i - 1], rwkv_v_up[i - 1]))
            mixer = (oa.reshape(t, -1), ob.reshape(t, -1), w_out)
        else:
            j = l // 2
            x, wb, (cw_in, cw_out) = ffn(x, wb, (l, 0), extra=[(c_w_in.astype(F32), (j,)), (c_w_out.astype(F32), (j,))])
            x = _rglru(x.reshape(b, s, d), norm_mix[l], cw_in, c_conv_w[j], c_conv_b[j], c_wa[j], c_ba[j],
                       c_wx[j], c_bx[j], c_lambda[j], cw_out).reshape(t, d)
        x, wb, _ = ffn(x, wb, (l, 1), mixer)
    return x.reshape(b, s, d)
```

```python
import functools

import jax
import jax.numpy as jnp
import numpy as np
from jax import lax
from jax.experimental import pallas as pl
from jax.experimental.pallas import tpu as pltpu

F32 = jnp.float32
BF16 = jnp.bfloat16

D_MODEL = 1024
DEPTH = 4
CHUNK = 64
LEFT_CHUNKS = 8
A_HEADS = 8
A_HEAD_DIM = 64
A_WIDTH = A_HEADS * A_HEAD_DIM
REL_CLIP = 128
B_HEADS = 8
B_HEAD_SIZE = 64
B_WIDTH = B_HEADS * B_HEAD_SIZE
W_LORA = 64
A_LORA = 64
G_LORA = 128
V_LORA = 32
DECAY_SCALE = 0.606531
GN_EPS = 64e-5
B_COLS = 3 * B_WIDTH + W_LORA + A_LORA + G_LORA
D_RNN = 1280
C_BLOCKS = 10
C_BLOCK_W = D_RNN // C_BLOCKS
CONV_W = 4
RG_C = 8.0
D_FF = 2816
NORM_EPS = 1e-6
NEG_INF = -1e30

V7X_LANES = 128
V7X_SUBLANES = 8
V7X_BF16_SUBLANES = 16
V7X_MXU_DIM = 256
V7X_VMEM_BYTES = 64 * 1024 * 1024

FFN_ROWS = 512
FFN_CHUNKS = (0, 768, 1536, 2304, 2816)
PROJ_ROWS = 1024
ATT_ROWS = 256
ATT_PREV = LEFT_CHUNKS * CHUNK
RWKV_CHUNK = 64
RGLRU_ROWS = 256
RGLRU_BATCH = 1
VMEM_LIMIT = 56 * 1024 * 1024


def _params(*sem):
    return pltpu.CompilerParams(dimension_semantics=sem, vmem_limit_bytes=VMEM_LIMIT)


def _const_spec(shape):
    nd = len(shape)
    return pl.BlockSpec(shape, lambda *_: (0,) * nd, pipeline_mode=pl.Buffered(1))


def _mm(a, b):
    return jnp.dot(a.astype(BF16), b.astype(BF16), preferred_element_type=F32)


def _mm_nt(a, b):
    return lax.dot_general(a.astype(BF16), b.astype(BF16), (((1,), (1,)), ((), ())),
                           preferred_element_type=F32)


def _mm_tn(a, b):
    return lax.dot_general(a.astype(BF16), b.astype(BF16), (((0,), (0,)), ((), ())),
                           preferred_element_type=F32)


def _rms(x, g):
    ms = jnp.mean(x * x, axis=-1, keepdims=True)
    return x * lax.rsqrt(ms + NORM_EPS) * g


def _sigmoid(x):
    return 0.5 * jnp.tanh(0.5 * x) + 0.5


def _slice_rows(total, steps):
    k = 1
    while (total * k) % steps or (total * k // steps) % V7X_BF16_SUBLANES:
        k *= 2
    return total * k // steps, k


def _ffn_body(*refs, final_norm, mixer_out, n_casts):
    refs = list(refs)
    if mixer_out:
        oa_ref, ob_ref, wo_ref = refs[:3]
        refs = refs[3:]
    x_ref, g_ref, wg_ref, wu_ref, wd_ref, gf_ref = refs[:6]
    refs = refs[6:]
    o_ref = refs[n_casts]
    for src, dst in zip(refs[:n_casts], refs[n_casts + 1:]):
        dst[...] = src[...].astype(BF16)
    x = x_ref[...]
    if mixer_out:
        x = x + jnp.dot(oa_ref[...], wo_ref[:A_WIDTH, :], preferred_element_type=F32)
        x = x + jnp.dot(ob_ref[...], wo_ref[A_WIDTH:, :], preferred_element_type=F32)
    h = _rms(x, g_ref[...]).astype(BF16)
    acc = None
    for lo, hi in zip(FFN_CHUNKS[:-1], FFN_CHUNKS[1:]):
        gt = jnp.dot(h, wg_ref[:, lo:hi], preferred_element_type=F32)
        up = jnp.dot(h, wu_ref[:, lo:hi], preferred_element_type=F32)
        act = (gt * _sigmoid(gt) * up).astype(BF16)
        d = jnp.dot(act, wd_ref[lo:hi, :], preferred_element_type=F32)
        acc = d if acc is None else acc + d
    y = x + 0.5 * acc
    if final_norm:
        y = _rms(y, gf_ref[...])
    o_ref[...] = y


def _ffn(x, g, wb, gf, final_norm, mixer=None, casts=()):
    t, d = x.shape
    tm = min(FFN_ROWS, t)
    steps = t // tm
    row = pl.BlockSpec((tm, d), lambda i: (i, 0))
    args = [x, g, *wb, gf]
    in_specs = [row, _const_spec((1, d))] + [_const_spec(w.shape) for w in wb] + [_const_spec((1, d))]
    if mixer is not None:
        oa, ob, wo = mixer
        args = [oa, ob, wo] + args
        in_specs = [pl.BlockSpec((tm, oa.shape[1]), lambda i: (i, 0)), pl.BlockSpec((tm, ob.shape[1]), lambda i: (i, 0)),
                    _const_spec(wo.shape)] + in_specs
    out_shape = [jax.ShapeDtypeStruct((t, d), F32)]
    out_specs = [row]
    for w, lead in casts:
        lead = tuple(lead)
        nr, nc = w.shape[-2:]
        rows, k = _slice_rows(nr, steps)
        args.append(w)
        in_specs.append(pl.BlockSpec((None,) * len(lead) + (rows, nc), lambda i, k=k, lead=lead: lead + (i // k, 0)))
        out_shape.append(jax.ShapeDtypeStruct((nr, nc), BF16))
        out_specs.append(pl.BlockSpec((rows, nc), lambda i, k=k: (i // k, 0)))
    out = pl.pallas_call(
        functools.partial(_ffn_body, final_norm=final_norm, mixer_out=mixer is not None, n_casts=len(casts)),
        out_shape=tuple(out_shape),
        grid=(steps,),
        in_specs=in_specs,
        out_specs=tuple(out_specs),
        compiler_params=_params("arbitrary"),
        name="ffn_final" if final_norm else ("ffn_mix" if mixer is not None else "ffn"),
    )(*args)
    return out[0], tuple(out[1:])


def _ab_in_body(x_ref, g_ref, w_ref, qkv_ref, pb_ref):
    h = _rms(x_ref[...], g_ref[...]).astype(BF16)
    na = qkv_ref.shape[-1]
    qkv_ref[...] = jnp.dot(h, w_ref[:, :na], preferred_element_type=F32).astype(BF16)
    pb_ref[...] = jnp.dot(h, w_ref[:, na:], preferred_element_type=F32)


def _ab_in(x, g, w):
    t, d = x.shape
    n = w.shape[-1]
    na = 3 * A_WIDTH
    tm = min(PROJ_ROWS, t)
    return pl.pallas_call(
        _ab_in_body,
        out_shape=(jax.ShapeDtypeStruct((t, na), BF16), jax.ShapeDtypeStruct((t, n - na), F32)),
        grid=(t // tm,),
        in_specs=[pl.BlockSpec((tm, d), lambda i: (i, 0)), _const_spec((1, d)), _const_spec(w.shape)],
        out_specs=(pl.BlockSpec((tm, na), lambda i: (i, 0)), pl.BlockSpec((tm, n - na), lambda i: (i, 0))),
        compiler_params=_params("parallel"),
        name="ab_in",
    )(x, g, w)


def _attn_body(*refs, n_kblocks):
    q_ref = refs[0]
    k_refs = refs[1:1 + n_kblocks]
    v_refs = refs[1 + n_kblocks:1 + 2 * n_kblocks]
    trow_ref = refs[1 + 2 * n_kblocks]
    o_ref = refs[2 + 2 * n_kblocks]
    bias_ref = refs[3 + 2 * n_kblocks]
    qb = q_ref.shape[0]
    i = pl.program_id(1)
    n_prev = n_kblocks - 1

    @pl.when(i == 0)
    def _():
        w = bias_ref.shape[2]
        rr = lax.broadcasted_iota(jnp.int32, (qb, w), 0) // CHUNK
        cc = lax.broadcasted_iota(jnp.int32, (qb, w), 1) // CHUNK
        band = jnp.logical_and(cc >= rr, cc <= rr + LEFT_CHUNKS)
        for h in range(A_HEADS):
            t = jnp.broadcast_to(trow_ref[h], (qb, trow_ref.shape[2]))
            toe = pltpu.roll(t, 0, 1, stride=1, stride_axis=0)
            bias_ref[h] = jnp.where(band, toe[:, :w], NEG_INF)

    pens = [jnp.where(i >= n_prev - j, 0.0, NEG_INF).astype(F32) for j in range(n_prev)] + [None]
    lane = lax.broadcasted_iota(jnp.int32, (1, 2 * A_HEAD_DIM), 1)
    first = lane < A_HEAD_DIM
    scale = A_HEAD_DIM ** -0.5
    for pr in range(A_HEADS // 2):
        ls = slice(2 * A_HEAD_DIM * pr, 2 * A_HEAD_DIM * (pr + 1))
        q_pair = q_ref[:, ls]
        ks = [r[:, ls] for r in k_refs]
        vs = [r[:, ls] for r in v_refs]
        outs = []
        for hh in range(2):
            head = 2 * pr + hh
            sel = first if hh == 0 else jnp.logical_not(first)
            qm = jnp.where(sel, scale, 0.0).astype(BF16)
            qh = q_pair * qm
            s = []
            for j in range(n_kblocks):
                sj = lax.dot_general(qh, ks[j], (((1,), (1,)), ((), ())), preferred_element_type=F32)
                sj = sj + bias_ref[head, :, j * qb:(j + 1) * qb]
                if pens[j] is not None:
                    sj = sj + pens[j]
                s.append(sj)
            m = s[0].max(axis=-1, keepdims=True)
            for sj in s[1:]:
                m = jnp.maximum(m, sj.max(axis=-1, keepdims=True))
            l = None
            pv = None
            for j in range(n_kblocks):
                p = jnp.exp(s[j] - m)
                lj = p.sum(axis=-1, keepdims=True)
                pvj = jnp.dot(p.astype(BF16), vs[j], preferred_element_type=F32)
                l = lj if l is None else l + lj
                pv = pvj if pv is None else pv + pvj
            outs.append(pv / l)
        o_ref[:, ls] = jnp.where(first, outs[0], outs[1]).astype(BF16)


def _attn_bias_row(rel_bias, qb):
    nh = rel_bias.shape[0]
    w = ATT_PREV + qb
    period = w + qb
    near = 2 * REL_CLIP + 1
    assert ATT_PREV >= REL_CLIP and qb >= REL_CLIP and period % V7X_LANES == 0
    far_past = jnp.broadcast_to(rel_bias[:, near - 1:], (nh, ATT_PREV - REL_CLIP))
    far_future = jnp.broadcast_to(rel_bias[:, :1], (nh, qb - REL_CLIP))
    wrapped = jnp.broadcast_to(rel_bias[:, near - 1:], (nh, qb - 1))
    t = jnp.concatenate([far_past, rel_bias[:, ::-1], far_future, wrapped], axis=1).astype(F32)
    assert t.shape[1] == period
    return t.reshape(nh, 1, period)


def _attention(qkv, rel_bias):
    b, s, _ = qkv.shape
    qb = min(ATT_ROWS, s)
    assert ATT_PREV % qb == 0 and s % qb == 0
    n_prev = ATT_PREV // qb
    n_kblocks = n_prev + 1
    trow = _attn_bias_row(rel_bias, qb)

    def blk(col, back):
        return pl.BlockSpec((None, qb, A_WIDTH), lambda bi, i: (bi, jnp.maximum(i - back, 0), col))

    in_specs = ([blk(0, 0)] + [blk(1, n_prev - j) for j in range(n_kblocks)]
                + [blk(2, n_prev - j) for j in range(n_kblocks)] + [_const_spec(trow.shape)])
    return pl.pallas_call(
        functools.partial(_attn_body, n_kblocks=n_kblocks),
        out_shape=jax.ShapeDtypeStruct((b, s, A_WIDTH), BF16),
        grid=(b, s // qb),
        in_specs=in_specs,
        out_specs=pl.BlockSpec((None, qb, A_WIDTH), lambda bi, i: (bi, i, 0)),
        scratch_shapes=[pltpu.VMEM((A_HEADS, qb, ATT_PREV + qb), F32)],
        compiler_params=_params("parallel", "arbitrary"),
        name="attention",
    )(*([qkv] * (1 + 2 * n_kblocks)), trow)


def _split3(x):
    h1 = x.astype(BF16)
    r1 = x - h1.astype(F32)
    h2 = r1.astype(BF16)
    h3 = (r1 - h2.astype(F32)).astype(BF16)
    return h1, h2, h3


def _rwkv_body(*refs, has_vmix):
    if has_vmix:
        (pb_ref, vf_ref, mu_ref, w0_ref, wup_ref, a0_ref, aup_ref, gup_ref, kk_ref, ka_ref, rk_ref,
         lnw_ref, lnb_ref, v0_ref, vdn_ref, vup_ref, o_ref, state_ref, carry_ref) = refs
        vraw_ref = None
    else:
        (pb_ref, mu_ref, w0_ref, wup_ref, a0_ref, aup_ref, gup_ref, kk_ref, ka_ref, rk_ref,
         lnw_ref, lnb_ref, o_ref, vraw_ref, state_ref, carry_ref) = refs
    nb, c, ncol = pb_ref.shape
    n = B_HEAD_SIZE
    rows = nb * c

    @pl.when(pl.program_id(0) == 0)
    def _():
        state_ref[...] = jnp.zeros_like(state_ref)
        carry_ref[...] = jnp.zeros_like(carry_ref)

    p = pb_ref[...].reshape(rows, ncol)
    row = lax.broadcasted_iota(jnp.int32, (rows, 1), 0)
    prev = pltpu.roll(p, 1, axis=0)
    for b in range(nb):
        prev = jnp.where(row == b * c, carry_ref[b, 0:1, :], prev)
    for b in range(nb):
        carry_ref[b, 0:1, :] = p[(b + 1) * c - 1:(b + 1) * c, :]
    ps = p + (prev - p) * mu_ref[...]
    r = ps[:, 0:B_WIDTH]
    k = ps[:, B_WIDTH:2 * B_WIDTH]
    v = ps[:, 2 * B_WIDTH:3 * B_WIDTH]
    wad = ps[:, 3 * B_WIDTH:3 * B_WIDTH + W_LORA + A_LORA]
    gd = ps[:, 3 * B_WIDTH + W_LORA + A_LORA:]
    if has_vmix:
        mix = _sigmoid(v0_ref[...] + _mm(_mm(v, vdn_ref[...]), vup_ref[...]))
        v = v + (vf_ref[...].reshape(rows, B_WIDTH) - v) * mix
    else:
        vraw_ref[...] = v.reshape(nb, c, B_WIDTH)
    wlog = -DECAY_SCALE * _sigmoid(w0_ref[...] + _mm(jnp.tanh(wad), wup_ref[...]))
    a = _sigmoid(a0_ref[...] + _mm(wad, aup_ref[...]))
    g = _mm(_sigmoid(gd), gup_ref[...])
    kkf = k * kk_ref[...]
    kka = kkf * a
    kmod = k * (1.0 + (a - 1.0) * ka_ref[...])
    rkr = r * kmod * rk_ref[...]

    ti = lax.broadcasted_iota(jnp.int32, (c, 3 * c), 0)
    tj = lax.broadcasted_iota(jnp.int32, (c, 3 * c), 1) % c
    tri3 = jnp.where(tj <= ti, 1.0, 0.0).astype(BF16)
    lcum, lend = [], []
    for b in range(nb):
        lb = jnp.dot(tri3, jnp.concatenate(_split3(wlog[b * c:(b + 1) * c]), axis=0),
                     preferred_element_type=F32)
        lcum.append(lb)
        lend.append(jnp.broadcast_to(lb[c - 1:c, :], lb.shape))
    lcum = jnp.concatenate(lcum, axis=0)
    lend = jnp.concatenate(lend, axis=0)
    e_neg = jnp.exp(-lcum)
    e_end = jnp.exp(lend - lcum)
    w_end = jnp.exp(lend)

    pair = 2 * n
    npair = B_HEADS // 2
    lo = lax.broadcasted_iota(jnp.int32, (1, 1, pair), 2) < n

    def slabs(q):
        return jnp.stack([q[b * c:(b + 1) * c, p * pair:(p + 1) * pair]
                          for b in range(nb) for p in range(npair)], axis=0)

    def unslabs(q):
        return jnp.concatenate(
            [jnp.concatenate([q[b * npair + p] for p in range(npair)], axis=-1) for b in range(nb)], axis=0)

    def head_sum(q):
        s_lo = jnp.sum(jnp.where(lo, q, 0.0), axis=-1, keepdims=True)
        s_hi = jnp.sum(jnp.where(lo, 0.0, q), axis=-1, keepdims=True)
        return jnp.where(lo, s_lo, s_hi)

    def bd(q):
        qb = q.astype(BF16)
        zero = jnp.zeros_like(qb)
        return jnp.concatenate([jnp.where(lo, qb, zero), jnp.where(lo, zero, qb)], axis=1)

    def bmm(lhs, rhs):
        return jnp.einsum("gij,gjv->giv", lhs.astype(BF16), rhs.astype(BF16), preferred_element_type=F32)

    kk_s = slabs(kkf)
    rs = lax.rsqrt(jnp.maximum(head_sum(kk_s * kk_s), 1e-24))
    x = jnp.concatenate([slabs(-kkf * jnp.exp(lcum - wlog)) * rs, slabs(r * jnp.exp(lcum))], axis=1)
    bt = slabs(kka * e_neg) * rs
    kt = slabs(kmod * e_neg)
    bk_end = jnp.concatenate([slabs(kka * e_end) * rs, slabs(kmod * e_end)], axis=1)
    vh = slabs(v)
    s0 = state_ref[...]
    xb = x.astype(BF16)
    ybd = jnp.concatenate([bd(bt), bd(kt)], axis=1).astype(BF16)
    gm = jnp.einsum("gik,gjk->gij", xb, ybd, preferred_element_type=F32)
    xs = jnp.einsum("gik,gvk->giv", xb, s0.astype(BF16), preferred_element_type=F32)
    ri = lax.broadcasted_iota(jnp.int32, (1, c, 2 * c), 1)
    ci = lax.broadcasted_iota(jnp.int32, (1, c, 2 * c), 2) % c
    strict = ci < ri
    incl = ci <= ri
    m_ab = jnp.where(strict, gm[:, :c, :2 * c], 0.0)
    m_ak = jnp.where(strict, gm[:, :c, 2 * c:], 0.0)
    m_rb = jnp.where(incl, gm[:, c:, :2 * c], 0.0)
    m_rk = jnp.where(incl, gm[:, c:, 2 * c:], 0.0)
    bdv = bd(vh)
    u = xs[:, :c] + bmm(m_ak, bdv)
    pw = m_ab.astype(BF16)
    span = 1
    while True:
        u = u + bmm(pw, bd(u))
        span *= 2
        if span >= c:
            break
        pw = bmm(pw, bd(pw)).astype(BF16)
    o = xs[:, c:] + bmm(jnp.concatenate([m_rb, m_rk], axis=2), jnp.concatenate([bd(u), bdv], axis=1))
    w_end_s = jnp.stack([w_end[b * c:b * c + 1, p * pair:(p + 1) * pair]
                         for b in range(nb) for p in range(npair)], axis=0)
    uv = jnp.concatenate([u, vh], axis=1)
    upd = jnp.einsum("gjv,gjk->gvk", uv.astype(BF16), bk_end.astype(BF16), preferred_element_type=F32)
    same_head = (lax.broadcasted_iota(jnp.int32, (1, pair, pair), 1) < n) == lo
    state_ref[...] = s0 * w_end_s + jnp.where(same_head, upd, 0.0)
    mean = head_sum(o) * (1.0 / n)
    d = o - mean
    var = head_sum(d * d) * (1.0 / n)
    o = d * lax.rsqrt(var + GN_EPS)
    bonus = head_sum(slabs(rkr)) * vh
    out = (unslabs(o) * lnw_ref[...] + lnb_ref[...] + unslabs(bonus)) * g
    o_ref[...] = out.astype(BF16).reshape(nb, c, B_WIDTH)


def _rwkv(pb, v_first, prm, vmix):
    b, s, ncol = pb.shape
    c = min(RWKV_CHUNK, s)
    has_vmix = vmix is not None
    tok = lambda w: pl.BlockSpec((b, c, w), lambda i: (0, i, 0))
    vec = lambda a: a.reshape(1, -1).astype(F32)
    lw = W_LORA + A_LORA
    wup = jnp.zeros((lw, B_WIDTH), F32).at[:W_LORA].set(prm["w_up"]).astype(BF16)
    aup = jnp.zeros((lw, B_WIDTH), F32).at[W_LORA:].set(prm["a_up"]).astype(BF16)
    common = [vec(prm["mu"]), vec(prm["w0"]), wup, vec(prm["a0"]), aup, prm["g_up"].astype(BF16),
              vec(prm["k_k"]), vec(prm["k_a"]), vec(prm["r_k"]), vec(prm["ln_w"]), vec(prm["ln_b"])]
    if has_vmix:
        v0, v_down, v_up = vmix
        vdn = jnp.zeros((B_WIDTH, V7X_LANES), F32).at[:, :V_LORA].set(v_down).astype(BF16)
        vup = jnp.zeros((V7X_LANES, B_WIDTH), F32).at[:V_LORA].set(v_up).astype(BF16)
        args = [pb, v_first] + common + [vec(v0), vdn, vup]
        in_specs = [tok(ncol), tok(B_WIDTH)] + [_const_spec(a.shape) for a in args[2:]]
        out_shape = jax.ShapeDtypeStruct((b, s, B_WIDTH), BF16)
        out_specs = tok(B_WIDTH)
    else:
        args = [pb] + common
        in_specs = [tok(ncol)] + [_const_spec(a.shape) for a in args[1:]]
        out_shape = (jax.ShapeDtypeStruct((b, s, B_WIDTH), BF16), jax.ShapeDtypeStruct((b, s, B_WIDTH), F32))
        out_specs = (tok(B_WIDTH), tok(B_WIDTH))
    return pl.pallas_call(
        functools.partial(_rwkv_body, has_vmix=has_vmix),
        out_shape=out_shape,
        grid=(s // c,),
        in_specs=in_specs,
        out_specs=out_specs,
        scratch_shapes=[pltpu.VMEM((b * B_HEADS // 2, 2 * B_HEAD_SIZE, 2 * B_HEAD_SIZE), F32),
                        pltpu.VMEM((b, V7X_SUBLANES, ncol), F32)],
        compiler_params=_params("arbitrary"),
        name="rwkv_mix" if has_vmix else "rwkv",
    )(*args)


def _gelu_tanh(x):
    return 0.5 * x * (1.0 + jnp.tanh(np.sqrt(2.0 / np.pi).astype(np.float32) * (x + 0.044715 * (x * x * x))))


def _rglru_tile(bi, x_ref, g_ref, win_ref, cw_ref, cb_ref, wax_ref, ba_ref, bx_ref, lam_ref, wout_ref,
                o_ref, xprev_ref, hprev_ref):
    tr = x_ref.shape[1]
    sub = V7X_SUBLANES
    nrow = tr // sub
    ntail = CONV_W - 1
    x = x_ref[bi]
    h = _rms(x, g_ref[...]).astype(BF16)
    i0 = lax.broadcasted_iota(jnp.int32, (tr, tr), 0)
    i1 = lax.broadcasted_iota(jnp.int32, (tr, tr), 1)
    perm = jnp.where(i1 == (i0 % sub) * nrow + i0 // sub, 1.0, 0.0).astype(BF16)
    unperm = jnp.where(i0 == (i1 % sub) * nrow + i1 // sub, 1.0, 0.0).astype(BF16)
    hp = jnp.dot(perm, h, preferred_element_type=F32).astype(BF16)
    gate = _gelu_tanh(jnp.dot(hp, win_ref[:, :D_RNN], preferred_element_type=F32))
    xb3 = jnp.dot(hp, win_ref[:, D_RNN:], preferred_element_type=F32).reshape(nrow, sub, D_RNN)
    seg = lax.broadcasted_iota(jnp.int32, (1, sub, 1), 1)
    prev3 = xprev_ref[bi]
    tails = [jnp.where(seg == 0, pltpu.roll(prev3[q:q + 1], 1, axis=1),
                       pltpu.roll(xb3[nrow - ntail + q:nrow - ntail + q + 1], 1, axis=1)) for q in range(ntail)]
    xprev_ref[bi] = xb3[nrow - ntail:]
    xc3 = xb3 * cw_ref[CONV_W - 1:CONV_W, :] + cb_ref[...]
    for sft in range(1, CONV_W):
        shifted = jnp.concatenate(tails[ntail - sft:] + [xb3[:nrow - sft]], axis=0)
        xc3 = xc3 + shifted * cw_ref[CONV_W - 1 - sft:CONV_W - sft, :]
    xc = xc3.reshape(tr, D_RNN)
    xcb = xc.astype(BF16)
    pw = 2 * C_BLOCK_W
    ra, rx = [], []
    for q in range(C_BLOCKS // 2):
        t = jnp.dot(xcb[:, q * pw:(q + 1) * pw], wax_ref[q], preferred_element_type=F32)
        ra.append(t[:, :pw])
        rx.append(t[:, pw:])
    rgate = _sigmoid(jnp.concatenate(ra, axis=-1) + ba_ref[...])
    igate = _sigmoid(jnp.concatenate(rx, axis=-1) + bx_ref[...])
    z = -lam_ref[...]
    softplus = jnp.maximum(z, 0.0) + jnp.log(1.0 + jnp.exp(-jnp.abs(z)))
    a = jnp.exp(-RG_C * rgate * softplus)
    mult = jnp.sqrt(jnp.maximum(1.0 - a * a, 0.0))
    bb = mult * igate * xc
    a3 = a.reshape(nrow, sub, D_RNN)
    b3 = bb.reshape(nrow, sub, D_RNN)
    acs, bcs = [a3[0]], [b3[0]]
    for p in range(1, nrow):
        bcs.append(a3[p] * bcs[-1] + b3[p])
        acs.append(a3[p] * acs[-1])
    ga, gb = acs[-1], bcs[-1]
    srow = lax.broadcasted_iota(jnp.int32, (sub, 1), 0)
    sft = 1
    while sft < sub:
        keep = srow >= sft
        a_sh = jnp.where(keep, pltpu.roll(ga, sft, axis=0), 1.0)
        b_sh = jnp.where(keep, pltpu.roll(gb, sft, axis=0), 0.0)
        gb = ga * b_sh + gb
        ga = ga * a_sh
        sft *= 2
    hprev = hprev_ref[bi, 0:1, :]
    hend = ga * hprev + gb
    hin = jnp.where(srow == 0, hprev, pltpu.roll(hend, 1, axis=0))
    hprev_ref[bi, 0:1, :] = hend[sub - 1:sub, :]
    hs = jnp.stack([acs[p] * hin + bcs[p] for p in range(nrow)], axis=0).reshape(tr, D_RNN)
    yp = (hs * gate).astype(BF16)
    y = jnp.dot(unperm, yp, preferred_element_type=F32).astype(BF16)
    o_ref[bi] = x + jnp.dot(y, wout_ref[...], preferred_element_type=F32)


def _rglru_body(x_ref, *rest):
    xprev_ref, hprev_ref = rest[-2:]

    @pl.when(pl.program_id(1) == 0)
    def _():
        xprev_ref[...] = jnp.zeros_like(xprev_ref)
        hprev_ref[...] = jnp.zeros_like(hprev_ref)

    for bi in range(x_ref.shape[0]):
        _rglru_tile(bi, x_ref, *rest)


def _rglru(x, g, w_in, conv_w, conv_b, wa, ba, wx, bx, lam, w_out):
    b, s, d = x.shape
    tr = min(RGLRU_ROWS, s)
    pw = 2 * C_BLOCK_W
    wax = jnp.zeros((C_BLOCKS // 2, pw, 2 * pw), F32)
    for half in range(2):
        sl = slice(half * C_BLOCK_W, (half + 1) * C_BLOCK_W)
        wax = wax.at[:, sl, sl].set(wa[half::2])
        wax = wax.at[:, sl, pw + half * C_BLOCK_W:pw + (half + 1) * C_BLOCK_W].set(wx[half::2])
    wax = wax.astype(BF16)
    vec = lambda a: a.reshape(1, -1).astype(F32)
    args = [x, vec(g), w_in, conv_w.astype(F32), vec(conv_b), wax, vec(ba), vec(bx), vec(lam), w_out]
    nb = min(RGLRU_BATCH, b)
    tok = pl.BlockSpec((nb, tr, d), lambda bi, i: (bi, i, 0))
    return pl.pallas_call(
        _rglru_body,
        out_shape=jax.ShapeDtypeStruct((b, s, d), F32),
        grid=(b // nb, s // tr),
        in_specs=[tok] + [_const_spec(a.shape) for a in args[1:]],
        out_specs=tok,
        scratch_shapes=[pltpu.VMEM((nb, CONV_W - 1, V7X_SUBLANES, D_RNN), F32),
                        pltpu.VMEM((nb, V7X_SUBLANES, D_RNN), F32)],
        compiler_params=_params("parallel", "arbitrary"),
        name="rglru",
    )(*args)


def kernel(x, norm_ffn, ffn_w_gate, ffn_w_up, ffn_w_down, norm_mix, ab_w_in, ab_w_out, att_rel_bias, rwkv_mu, rwkv_w0, rwkv_w_up, rwkv_a0, rwkv_a_up, rwkv_g_up, rwkv_k_k, rwkv_k_a, rwkv_r_k, rwkv_ln_w, rwkv_ln_b, rwkv_v0, rwkv_v_down, rwkv_v_up, c_w_in, c_conv_w, c_conv_b, c_wa, c_ba, c_wx, c_bx, c_lambda, c_w_out, norm_final):
    b, s, d = x.shape
    t = b * s
    depth = norm_mix.shape[0]
    ffn_w = (ffn_w_gate.astype(F32), ffn_w_up.astype(F32), ffn_w_down.astype(F32))
    ffn_calls = [(l, j) for l in range(depth) for j in range(2)]
    gfin = norm_final.reshape(1, d).astype(F32)
    wb = tuple(w[0, 0].astype(BF16) for w in ffn_w)

    def ffn(x, wb, lead, mixer=None, extra=()):
        idx = ffn_calls.index(lead)
        last = idx + 1 == len(ffn_calls)
        casts = [] if last else [(w, ffn_calls[idx + 1]) for w in ffn_w]
        x, cast = _ffn(x, norm_ffn[lead].reshape(1, d), wb, gfin, last, mixer, casts + list(extra))
        return x, cast[:len(casts)], cast[len(casts):]

    x = x.astype(F32).reshape(t, d)
    v_first = None
    for l in range(depth):
        mixer = None
        if l % 2 == 0:
            i = l // 2
            x, wb, (w_in, w_out) = ffn(x, wb, (l, 0), extra=[(ab_w_in.astype(F32), (i,)), (ab_w_out.astype(F32), (i,))])
            qkv, pb = _ab_in(x, norm_mix[l].reshape(1, d), w_in)
            oa = _attention(qkv.reshape(b, s, -1), att_rel_bias[i])
            prm = dict(mu=rwkv_mu[i], w0=rwkv_w0[i], w_up=rwkv_w_up[i], a0=rwkv_a0[i], a_up=rwkv_a_up[i],
                       g_up=rwkv_g_up[i], k_k=rwkv_k_k[i], k_a=rwkv_k_a[i], r_k=rwkv_r_k[i],
                       ln_w=rwkv_ln_w[i], ln_b=rwkv_ln_b[i])
            if i == 0:
                ob, v_first = _rwkv(pb.reshape(b, s, -1), None, prm, None)
            else:
                ob = _rwkv(pb.reshape(b, s, -1), v_first, prm,
                           (rwkv_v0[i - 1], rwkv_v_down[i - 1], rwkv_v_up[i - 1]))
            mixer = (oa.reshape(t, -1), ob.reshape(t, -1), w_out)
        else:
            j = l // 2
            x, wb, (cw_in, cw_out) = ffn(x, wb, (l, 0), extra=[(c_w_in.astype(F32), (j,)), (c_w_out.astype(F32), (j,))])
            x = _rglru(x.reshape(b, s, d), norm_mix[l], cw_in, c_conv_w[j], c_conv_b[j], c_wa[j], c_ba[j],
                       c_wx[j], c_bx[j], c_lambda[j], cw_out).reshape(t, d)
        x, wb, _ = ffn(x, wb, (l, 1), mixer)
    return x.reshape(b, s, d)
```

```python
import functools

import jax
import jax.numpy as jnp
import numpy as np
from jax import lax
from jax.experimental import pallas as pl
from jax.experimental.pallas import tpu as pltpu

F32 = jnp.float32
BF16 = jnp.bfloat16

D_MODEL = 1024
DEPTH = 4
CHUNK = 64
LEFT_CHUNKS = 8
A_HEADS = 8
A_HEAD_DIM = 64
A_WIDTH = A_HEADS * A_HEAD_DIM
REL_CLIP = 128
B_HEADS = 8
B_HEAD_SIZE = 64
B_WIDTH = B_HEADS * B_HEAD_SIZE
W_LORA = 64
A_LORA = 64
G_LORA = 128
V_LORA = 32
DECAY_SCALE = 0.606531
GN_EPS = 64e-5
B_COLS = 3 * B_WIDTH + W_LORA + A_LORA + G_LORA
D_RNN = 1280
C_BLOCKS = 10
C_BLOCK_W = D_RNN // C_BLOCKS
CONV_W = 4
RG_C = 8.0
D_FF = 2816
NORM_EPS = 1e-6
NEG_INF = -1e30

V7X_LANES = 128
V7X_SUBLANES = 8
V7X_BF16_SUBLANES = 16
V7X_MXU_DIM = 256
V7X_VMEM_BYTES = 64 * 1024 * 1024

FFN_ROWS = 512
FFN_CHUNKS = (0, 768, 1536, 2304, 2816)
PROJ_ROWS = 1024
ATT_ROWS = 512
ATT_PREV = LEFT_CHUNKS * CHUNK
RWKV_CHUNK = 64
RGLRU_ROWS = 256
RGLRU_BATCH = 1
VMEM_LIMIT = 56 * 1024 * 1024


def _params(*sem):
    return pltpu.CompilerParams(dimension_semantics=sem, vmem_limit_bytes=VMEM_LIMIT)


def _const_spec(shape):
    nd = len(shape)
    return pl.BlockSpec(shape, lambda *_: (0,) * nd, pipeline_mode=pl.Buffered(1))


def _mm(a, b):
    return jnp.dot(a.astype(BF16), b.astype(BF16), preferred_element_type=F32)


def _mm_nt(a, b):
    return lax.dot_general(a.astype(BF16), b.astype(BF16), (((1,), (1,)), ((), ())),
                           preferred_element_type=F32)


def _mm_tn(a, b):
    return lax.dot_general(a.astype(BF16), b.astype(BF16), (((0,), (0,)), ((), ())),
                           preferred_element_type=F32)


def _rms(x, g):
    ms = jnp.mean(x * x, axis=-1, keepdims=True)
    return x * lax.rsqrt(ms + NORM_EPS) * g


def _sigmoid(x):
    return 0.5 * jnp.tanh(0.5 * x) + 0.5


def _slice_rows(total, steps):
    k = 1
    while (total * k) % steps or (total * k // steps) % V7X_BF16_SUBLANES:
        k *= 2
    return total * k // steps, k


def _ffn_body(*refs, final_norm, mixer_out, n_casts):
    refs = list(refs)
    if mixer_out:
        oa_ref, ob_ref, wo_ref = refs[:3]
        refs = refs[3:]
    x_ref, g_ref, wg_ref, wu_ref, wd_ref, gf_ref = refs[:6]
    refs = refs[6:]
    o_ref = refs[n_casts]
    for src, dst in zip(refs[:n_casts], refs[n_casts + 1:]):
        dst[...] = src[...].astype(BF16)
    x = x_ref[...]
    if mixer_out:
        x = x + jnp.dot(oa_ref[...], wo_ref[:A_WIDTH, :], preferred_element_type=F32)
        x = x + jnp.dot(ob_ref[...], wo_ref[A_WIDTH:, :], preferred_element_type=F32)
    h = _rms(x, g_ref[...]).astype(BF16)
    acc = None
    for lo, hi in zip(FFN_CHUNKS[:-1], FFN_CHUNKS[1:]):
        gt = jnp.dot(h, wg_ref[:, lo:hi], preferred_element_type=F32)
        up = jnp.dot(h, wu_ref[:, lo:hi], preferred_element_type=F32)
        act = (gt * _sigmoid(gt) * up).astype(BF16)
        d = jnp.dot(act, wd_ref[lo:hi, :], preferred_element_type=F32)
        acc = d if acc is None else acc + d
    y = x + 0.5 * acc
    if final_norm:
        y = _rms(y, gf_ref[...])
    o_ref[...] = y


def _ffn(x, g, wb, gf, final_norm, mixer=None, casts=()):
    t, d = x.shape
    tm = min(FFN_ROWS, t)
    steps = t // tm
    row = pl.BlockSpec((tm, d), lambda i: (i, 0))
    args = [x, g, *wb, gf]
    in_specs = [row, _const_spec((1, d))] + [_const_spec(w.shape) for w in wb] + [_const_spec((1, d))]
    if mixer is not None:
        oa, ob, wo = mixer
        args = [oa, ob, wo] + args
        in_specs = [pl.BlockSpec((tm, oa.shape[1]), lambda i: (i, 0)), pl.BlockSpec((tm, ob.shape[1]), lambda i: (i, 0)),
                    _const_spec(wo.shape)] + in_specs
    out_shape = [jax.ShapeDtypeStruct((t, d), F32)]
    out_specs = [row]
    for w, lead in casts:
        lead = tuple(lead)
        nr, nc = w.shape[-2:]
        rows, k = _slice_rows(nr, steps)
        args.append(w)
        in_specs.append(pl.BlockSpec((None,) * len(lead) + (rows, nc), lambda i, k=k, lead=lead: lead + (i // k, 0)))
        out_shape.append(jax.ShapeDtypeStruct((nr, nc), BF16))
        out_specs.append(pl.BlockSpec((rows, nc), lambda i, k=k: (i // k, 0)))
    out = pl.pallas_call(
        functools.partial(_ffn_body, final_norm=final_norm, mixer_out=mixer is not None, n_casts=len(casts)),
        out_shape=tuple(out_shape),
        grid=(steps,),
        in_specs=in_specs,
        out_specs=tuple(out_specs),
        compiler_params=_params("arbitrary"),
        name="ffn_final" if final_norm else ("ffn_mix" if mixer is not None else "ffn"),
    )(*args)
    return out[0], tuple(out[1:])


def _ab_in_body(x_ref, g_ref, w_ref, qkv_ref, pb_ref):
    h = _rms(x_ref[...], g_ref[...]).astype(BF16)
    na = qkv_ref.shape[-1]
    qkv_ref[...] = jnp.dot(h, w_ref[:, :na], preferred_element_type=F32).astype(BF16)
    pb_ref[...] = jnp.dot(h, w_ref[:, na:], preferred_element_type=F32)


def _ab_in(x, g, w):
    t, d = x.shape
    n = w.shape[-1]
    na = 3 * A_WIDTH
    tm = min(PROJ_ROWS, t)
    return pl.pallas_call(
        _ab_in_body,
        out_shape=(jax.ShapeDtypeStruct((t, na), BF16), jax.ShapeDtypeStruct((t, n - na), F32)),
        grid=(t // tm,),
        in_specs=[pl.BlockSpec((tm, d), lambda i: (i, 0)), _const_spec((1, d)), _const_spec(w.shape)],
        out_specs=(pl.BlockSpec((tm, na), lambda i: (i, 0)), pl.BlockSpec((tm, n - na), lambda i: (i, 0))),
        compiler_params=_params("parallel"),
        name="ab_in",
    )(x, g, w)


def _attn_body(*refs, n_kblocks):
    q_ref = refs[0]
    k_refs = refs[1:1 + n_kblocks]
    v_refs = refs[1 + n_kblocks:1 + 2 * n_kblocks]
    trow_ref = refs[1 + 2 * n_kblocks]
    o_ref = refs[2 + 2 * n_kblocks]
    bias_ref = refs[3 + 2 * n_kblocks]
    qb = q_ref.shape[0]
    i = pl.program_id(1)
    n_prev = n_kblocks - 1

    @pl.when(i == 0)
    def _():
        w = bias_ref.shape[2]
        rr = lax.broadcasted_iota(jnp.int32, (qb, w), 0) // CHUNK
        cc = lax.broadcasted_iota(jnp.int32, (qb, w), 1) // CHUNK
        band = jnp.logical_and(cc >= rr, cc <= rr + LEFT_CHUNKS)
        for h in range(A_HEADS):
            t = jnp.broadcast_to(trow_ref[h], (qb, trow_ref.shape[2]))
            toe = pltpu.roll(t, 0, 1, stride=1, stride_axis=0)
            bias_ref[h] = jnp.where(band, toe[:, :w], NEG_INF)

    pens = [jnp.where(i >= n_prev - j, 0.0, NEG_INF).astype(F32) for j in range(n_prev)] + [None]
    lane = lax.broadcasted_iota(jnp.int32, (1, 2 * A_HEAD_DIM), 1)
    first = lane < A_HEAD_DIM
    scale = A_HEAD_DIM ** -0.5
    for pr in range(A_HEADS // 2):
        ls = slice(2 * A_HEAD_DIM * pr, 2 * A_HEAD_DIM * (pr + 1))
        q_pair = q_ref[:, ls]
        ks = [r[:, ls] for r in k_refs]
        vs = [r[:, ls] for r in v_refs]
        outs = []
        for hh in range(2):
            head = 2 * pr + hh
            sel = first if hh == 0 else jnp.logical_not(first)
            qm = jnp.where(sel, scale, 0.0).astype(BF16)
            qh = q_pair * qm
            s = []
            for j in range(n_kblocks):
                sj = lax.dot_general(qh, ks[j], (((1,), (1,)), ((), ())), preferred_element_type=F32)
                sj = sj + bias_ref[head, :, j * qb:(j + 1) * qb]
                if pens[j] is not None:
                    sj = sj + pens[j]
                s.append(sj)
            m = s[0].max(axis=-1, keepdims=True)
            for sj in s[1:]:
                m = jnp.maximum(m, sj.max(axis=-1, keepdims=True))
            l = None
            pv = None
            for j in range(n_kblocks):
                p = jnp.exp(s[j] - m)
                lj = p.sum(axis=-1, keepdims=True)
                pvj = jnp.dot(p.astype(BF16), vs[j], preferred_element_type=F32)
                l = lj if l is None else l + lj
                pv = pvj if pv is None else pv + pvj
            outs.append(pv / l)
        o_ref[:, ls] = jnp.where(first, outs[0], outs[1]).astype(BF16)


def _attn_bias_row(rel_bias, qb):
    nh = rel_bias.shape[0]
    w = ATT_PREV + qb
    period = w + qb
    near = 2 * REL_CLIP + 1
    assert ATT_PREV >= REL_CLIP and qb >= REL_CLIP and period % V7X_LANES == 0
    far_past = jnp.broadcast_to(rel_bias[:, near - 1:], (nh, ATT_PREV - REL_CLIP))
    far_future = jnp.broadcast_to(rel_bias[:, :1], (nh, qb - REL_CLIP))
    wrapped = jnp.broadcast_to(rel_bias[:, near - 1:], (nh, qb - 1))
    t = jnp.concatenate([far_past, rel_bias[:, ::-1], far_future, wrapped], axis=1).astype(F32)
    assert t.shape[1] == period
    return t.reshape(nh, 1, period)


def _attention(qkv, rel_bias):
    b, s, _ = qkv.shape
    qb = min(ATT_ROWS, s)
    assert ATT_PREV % qb == 0 and s % qb == 0
    n_prev = ATT_PREV // qb
    n_kblocks = n_prev + 1
    trow = _attn_bias_row(rel_bias, qb)

    def blk(col, back):
        return pl.BlockSpec((None, qb, A_WIDTH), lambda bi, i: (bi, jnp.maximum(i - back, 0), col))

    in_specs = ([blk(0, 0)] + [blk(1, n_prev - j) for j in range(n_kblocks)]
                + [blk(2, n_prev - j) for j in range(n_kblocks)] + [_const_spec(trow.shape)])
    return pl.pallas_call(
        functools.partial(_attn_body, n_kblocks=n_kblocks),
        out_shape=jax.ShapeDtypeStruct((b, s, A_WIDTH), BF16),
        grid=(b, s // qb),
        in_specs=in_specs,
        out_specs=pl.BlockSpec((None, qb, A_WIDTH), lambda bi, i: (bi, i, 0)),
        scratch_shapes=[pltpu.VMEM((A_HEADS, qb, ATT_PREV + qb), F32)],
        compiler_params=_params("parallel", "arbitrary"),
        name="attention",
    )(*([qkv] * (1 + 2 * n_kblocks)), trow)


def _split3(x):
    h1 = x.astype(BF16)
    r1 = x - h1.astype(F32)
    h2 = r1.astype(BF16)
    h3 = (r1 - h2.astype(F32)).astype(BF16)
    return h1, h2, h3


def _rwkv_body(*refs, has_vmix):
    if has_vmix:
        (pb_ref, vf_ref, mu_ref, w0_ref, wup_ref, a0_ref, aup_ref, gup_ref, kk_ref, ka_ref, rk_ref,
         lnw_ref, lnb_ref, v0_ref, vdn_ref, vup_ref, o_ref, state_ref, carry_ref) = refs
        vraw_ref = None
    else:
        (pb_ref, mu_ref, w0_ref, wup_ref, a0_ref, aup_ref, gup_ref, kk_ref, ka_ref, rk_ref,
         lnw_ref, lnb_ref, o_ref, vraw_ref, state_ref, carry_ref) = refs
    nb, c, ncol = pb_ref.shape
    n = B_HEAD_SIZE
    rows = nb * c

    @pl.when(pl.program_id(0) == 0)
    def _():
        state_ref[...] = jnp.zeros_like(state_ref)
        carry_ref[...] = jnp.zeros_like(carry_ref)

    p = pb_ref[...].reshape(rows, ncol)
    row = lax.broadcasted_iota(jnp.int32, (rows, 1), 0)
    prev = pltpu.roll(p, 1, axis=0)
    for b in range(nb):
        prev = jnp.where(row == b * c, carry_ref[b, 0:1, :], prev)
    for b in range(nb):
        carry_ref[b, 0:1, :] = p[(b + 1) * c - 1:(b + 1) * c, :]
    ps = p + (prev - p) * mu_ref[...]
    r = ps[:, 0:B_WIDTH]
    k = ps[:, B_WIDTH:2 * B_WIDTH]
    v = ps[:, 2 * B_WIDTH:3 * B_WIDTH]
    wad = ps[:, 3 * B_WIDTH:3 * B_WIDTH + W_LORA + A_LORA]
    gd = ps[:, 3 * B_WIDTH + W_LORA + A_LORA:]
    if has_vmix:
        mix = _sigmoid(v0_ref[...] + _mm(_mm(v, vdn_ref[...]), vup_ref[...]))
        v = v + (vf_ref[...].reshape(rows, B_WIDTH) - v) * mix
    else:
        vraw_ref[...] = v.reshape(nb, c, B_WIDTH)
    wlog = -DECAY_SCALE * _sigmoid(w0_ref[...] + _mm(jnp.tanh(wad), wup_ref[...]))
    a = _sigmoid(a0_ref[...] + _mm(wad, aup_ref[...]))
    g = _mm(_sigmoid(gd), gup_ref[...])
    kkf = k * kk_ref[...]
    kka = kkf * a
    kmod = k * (1.0 + (a - 1.0) * ka_ref[...])
    rkr = r * kmod * rk_ref[...]

    ti = lax.broadcasted_iota(jnp.int32, (c, 3 * c), 0)
    tj = lax.broadcasted_iota(jnp.int32, (c, 3 * c), 1) % c
    tri3 = jnp.where(tj <= ti, 1.0, 0.0).astype(BF16)
    lcum, lend = [], []
    for b in range(nb):
        lb = jnp.dot(tri3, jnp.concatenate(_split3(wlog[b * c:(b + 1) * c]), axis=0),
                     preferred_element_type=F32)
        lcum.append(lb)
        lend.append(jnp.broadcast_to(lb[c - 1:c, :], lb.shape))
    lcum = jnp.concatenate(lcum, axis=0)
    lend = jnp.concatenate(lend, axis=0)
    e_neg = jnp.exp(-lcum)
    e_end = jnp.exp(lend - lcum)
    w_end = jnp.exp(lend)

    pair = 2 * n
    npair = B_HEADS // 2
    lo = lax.broadcasted_iota(jnp.int32, (1, 1, pair), 2) < n

    def slabs(q):
        return jnp.stack([q[b * c:(b + 1) * c, p * pair:(p + 1) * pair]
                          for b in range(nb) for p in range(npair)], axis=0)

    def unslabs(q):
        return jnp.concatenate(
            [jnp.concatenate([q[b * npair + p] for p in range(npair)], axis=-1) for b in range(nb)], axis=0)

    def head_sum(q):
        s_lo = jnp.sum(jnp.where(lo, q, 0.0), axis=-1, keepdims=True)
        s_hi = jnp.sum(jnp.where(lo, 0.0, q), axis=-1, keepdims=True)
        return jnp.where(lo, s_lo, s_hi)

    def bd(q):
        qb = q.astype(BF16)
        zero = jnp.zeros_like(qb)
        return jnp.concatenate([jnp.where(lo, qb, zero), jnp.where(lo, zero, qb)], axis=1)

    def bmm(lhs, rhs):
        return jnp.einsum("gij,gjv->giv", lhs.astype(BF16), rhs.astype(BF16), preferred_element_type=F32)

    kk_s = slabs(kkf)
    rs = lax.rsqrt(jnp.maximum(head_sum(kk_s * kk_s), 1e-24))
    x = jnp.concatenate([slabs(-kkf * jnp.exp(lcum - wlog)) * rs, slabs(r * jnp.exp(lcum))], axis=1)
    bt = slabs(kka * e_neg) * rs
    kt = slabs(kmod * e_neg)
    bk_end = jnp.concatenate([slabs(kka * e_end) * rs, slabs(kmod * e_end)], axis=1)
    vh = slabs(v)
    s0 = state_ref[...]
    xb = x.astype(BF16)
    ybd = jnp.concatenate([bd(bt), bd(kt)], axis=1).astype(BF16)
    gm = jnp.einsum("gik,gjk->gij", xb, ybd, preferred_element_type=F32)
    xs = jnp.einsum("gik,gvk->giv", xb, s0.astype(BF16), preferred_element_type=F32)
    ri = lax.broadcasted_iota(jnp.int32, (1, c, 2 * c), 1)
    ci = lax.broadcasted_iota(jnp.int32, (1, c, 2 * c), 2) % c
    strict = ci < ri
    incl = ci <= ri
    m_ab = jnp.where(strict, gm[:, :c, :2 * c], 0.0)
    m_ak = jnp.where(strict, gm[:, :c, 2 * c:], 0.0)
    m_rb = jnp.where(incl, gm[:, c:, :2 * c], 0.0)
    m_rk = jnp.where(incl, gm[:, c:, 2 * c:], 0.0)
    bdv = bd(vh)
    u = xs[:, :c] + bmm(m_ak, bdv)
    pw = m_ab.astype(BF16)
    span = 1
    while True:
        u = u + bmm(pw, bd(u))
        span *= 2
        if span >= c:
            break
        pw = bmm(pw, bd(pw)).astype(BF16)
    o = xs[:, c:] + bmm(jnp.concatenate([m_rb, m_rk], axis=2), jnp.concatenate([bd(u), bdv], axis=1))
    w_end_s = jnp.stack([w_end[b * c:b * c + 1, p * pair:(p + 1) * pair]
                         for b in range(nb) for p in range(npair)], axis=0)
    uv = jnp.concatenate([u, vh], axis=1)
    upd = jnp.einsum("gjv,gjk->gvk", uv.astype(BF16), bk_end.astype(BF16), preferred_element_type=F32)
    same_head = (lax.broadcasted_iota(jnp.int32, (1, pair, pair), 1) < n) == lo
    state_ref[...] = s0 * w_end_s + jnp.where(same_head, upd, 0.0)
    mean = head_sum(o) * (1.0 / n)
    d = o - mean
    var = head_sum(d * d) * (1.0 / n)
    o = d * lax.rsqrt(var + GN_EPS)
    bonus = head_sum(slabs(rkr)) * vh
    out = (unslabs(o) * lnw_ref[...] + lnb_ref[...] + unslabs(bonus)) * g
    o_ref[...] = out.astype(BF16).reshape(nb, c, B_WIDTH)


def _rwkv(pb, v_first, prm, vmix):
    b, s, ncol = pb.shape
    c = min(RWKV_CHUNK, s)
    has_vmix = vmix is not None
    tok = lambda w: pl.BlockSpec((b, c, w), lambda i: (0, i, 0))
    vec = lambda a: a.reshape(1, -1).astype(F32)
    lw = W_LORA + A_LORA
    wup = jnp.zeros((lw, B_WIDTH), F32).at[:W_LORA].set(prm["w_up"]).astype(BF16)
    aup = jnp.zeros((lw, B_WIDTH), F32).at[W_LORA:].set(prm["a_up"]).astype(BF16)
    common = [vec(prm["mu"]), vec(prm["w0"]), wup, vec(prm["a0"]), aup, prm["g_up"].astype(BF16),
              vec(prm["k_k"]), vec(prm["k_a"]), vec(prm["r_k"]), vec(prm["ln_w"]), vec(prm["ln_b"])]
    if has_vmix:
        v0, v_down, v_up = vmix
        vdn = jnp.zeros((B_WIDTH, V7X_LANES), F32).at[:, :V_LORA].set(v_down).astype(BF16)
        vup = jnp.zeros((V7X_LANES, B_WIDTH), F32).at[:V_LORA].set(v_up).astype(BF16)
        args = [pb, v_first] + common + [vec(v0), vdn, vup]
        in_specs = [tok(ncol), tok(B_WIDTH)] + [_const_spec(a.shape) for a in args[2:]]
        out_shape = jax.ShapeDtypeStruct((b, s, B_WIDTH), BF16)
        out_specs = tok(B_WIDTH)
    else:
        args = [pb] + common
        in_specs = [tok(ncol)] + [_const_spec(a.shape) for a in args[1:]]
        out_shape = (jax.ShapeDtypeStruct((b, s, B_WIDTH), BF16), jax.ShapeDtypeStruct((b, s, B_WIDTH), F32))
        out_specs = (tok(B_WIDTH), tok(B_WIDTH))
    return pl.pallas_call(
        functools.partial(_rwkv_body, has_vmix=has_vmix),
        out_shape=out_shape,
        grid=(s // c,),
        in_specs=in_specs,
        out_specs=out_specs,
        scratch_shapes=[pltpu.VMEM((b * B_HEADS // 2, 2 * B_HEAD_SIZE, 2 * B_HEAD_SIZE), F32),
                        pltpu.VMEM((b, V7X_SUBLANES, ncol), F32)],
        compiler_params=_params("arbitrary"),
        name="rwkv_mix" if has_vmix else "rwkv",
    )(*args)


def _gelu_tanh(x):
    return 0.5 * x * (1.0 + jnp.tanh(np.sqrt(2.0 / np.pi).astype(np.float32) * (x + 0.044715 * (x * x * x))))


def _rglru_tile(bi, x_ref, g_ref, win_ref, cw_ref, cb_ref, wax_ref, ba_ref, bx_ref, lam_ref, wout_ref,
                o_ref, xprev_ref, hprev_ref):
    tr = x_ref.shape[1]
    sub = V7X_SUBLANES
    nrow = tr // sub
    ntail = CONV_W - 1
    x = x_ref[bi]
    h = _rms(x, g_ref[...]).astype(BF16)
    i0 = lax.broadcasted_iota(jnp.int32, (tr, tr), 0)
    i1 = lax.broadcasted_iota(jnp.int32, (tr, tr), 1)
    perm = jnp.where(i1 == (i0 % sub) * nrow + i0 // sub, 1.0, 0.0).astype(BF16)
    unperm = jnp.where(i0 == (i1 % sub) * nrow + i1 // sub, 1.0, 0.0).astype(BF16)
    hp = jnp.dot(perm, h, preferred_element_type=F32).astype(BF16)
    gate = _gelu_tanh(jnp.dot(hp, win_ref[:, :D_RNN], preferred_element_type=F32))
    xb3 = jnp.dot(hp, win_ref[:, D_RNN:], preferred_element_type=F32).reshape(nrow, sub, D_RNN)
    seg = lax.broadcasted_iota(jnp.int32, (1, sub, 1), 1)
    prev3 = xprev_ref[bi]
    tails = [jnp.where(seg == 0, pltpu.roll(prev3[q:q + 1], 1, axis=1),
                       pltpu.roll(xb3[nrow - ntail + q:nrow - ntail + q + 1], 1, axis=1)) for q in range(ntail)]
    xprev_ref[bi] = xb3[nrow - ntail:]
    xc3 = xb3 * cw_ref[CONV_W - 1:CONV_W, :] + cb_ref[...]
    for sft in range(1, CONV_W):
        shifted = jnp.concatenate(tails[ntail - sft:] + [xb3[:nrow - sft]], axis=0)
        xc3 = xc3 + shifted * cw_ref[CONV_W - 1 - sft:CONV_W - sft, :]
    xc = xc3.reshape(tr, D_RNN)
    xcb = xc.astype(BF16)
    pw = 2 * C_BLOCK_W
    ra, rx = [], []
    for q in range(C_BLOCKS // 2):
        t = jnp.dot(xcb[:, q * pw:(q + 1) * pw], wax_ref[q], preferred_element_type=F32)
        ra.append(t[:, :pw])
        rx.append(t[:, pw:])
    rgate = _sigmoid(jnp.concatenate(ra, axis=-1) + ba_ref[...])
    igate = _sigmoid(jnp.concatenate(rx, axis=-1) + bx_ref[...])
    z = -lam_ref[...]
    softplus = jnp.maximum(z, 0.0) + jnp.log(1.0 + jnp.exp(-jnp.abs(z)))
    a = jnp.exp(-RG_C * rgate * softplus)
    mult = jnp.sqrt(jnp.maximum(1.0 - a * a, 0.0))
    bb = mult * igate * xc
    a3 = a.reshape(nrow, sub, D_RNN)
    b3 = bb.reshape(nrow, sub, D_RNN)
    acs, bcs = [a3[0]], [b3[0]]
    for p in range(1, nrow):
        bcs.append(a3[p] * bcs[-1] + b3[p])
        acs.append(a3[p] * acs[-1])
    ga, gb = acs[-1], bcs[-1]
    srow = lax.broadcasted_iota(jnp.int32, (sub, 1), 0)
    sft = 1
    while sft < sub:
        keep = srow >= sft
        a_sh = jnp.where(keep, pltpu.roll(ga, sft, axis=0), 1.0)
        b_sh = jnp.where(keep, pltpu.roll(gb, sft, axis=0), 0.0)
        gb = ga * b_sh + gb
        ga = ga * a_sh
        sft *= 2
    hprev = hprev_ref[bi, 0:1, :]
    hend = ga * hprev + gb
    hin = jnp.where(srow == 0, hprev, pltpu.roll(hend, 1, axis=0))
    hprev_ref[bi, 0:1, :] = hend[sub - 1:sub, :]
    hs = jnp.stack([acs[p] * hin + bcs[p] for p in range(nrow)], axis=0).reshape(tr, D_RNN)
    yp = (hs * gate).astype(BF16)
    y = jnp.dot(unperm, yp, preferred_element_type=F32).astype(BF16)
    o_ref[bi] = x + jnp.dot(y, wout_ref[...], preferred_element_type=F32)


def _rglru_body(x_ref, *rest):
    xprev_ref, hprev_ref = rest[-2:]

    @pl.when(pl.program_id(1) == 0)
    def _():
        xprev_ref[...] = jnp.zeros_like(xprev_ref)
        hprev_ref[...] = jnp.zeros_like(hprev_ref)

    for bi in range(x_ref.shape[0]):
        _rglru_tile(bi, x_ref, *rest)


def _rglru(x, g, w_in, conv_w, conv_b, wa, ba, wx, bx, lam, w_out):
    b, s, d = x.shape
    tr = min(RGLRU_ROWS, s)
    zero = jnp.zeros((C_BLOCKS // 2, C_BLOCK_W, C_BLOCK_W), wa.dtype)
    even = jnp.concatenate([wa[0::2], zero, wx[0::2], zero], axis=2)
    odd = jnp.concatenate([zero, wa[1::2], zero, wx[1::2]], axis=2)
    wax = jnp.concatenate([even, odd], axis=1).astype(BF16)
    vec = lambda a: a.reshape(1, -1).astype(F32)
    args = [x, vec(g), w_in, conv_w.astype(F32), vec(conv_b), wax, vec(ba), vec(bx), vec(lam), w_out]
    nb = min(RGLRU_BATCH, b)
    tok = pl.BlockSpec((nb, tr, d), lambda bi, i: (bi, i, 0))
    return pl.pallas_call(
        _rglru_body,
        out_shape=jax.ShapeDtypeStruct((b, s, d), F32),
        grid=(b // nb, s // tr),
        in_specs=[tok] + [_const_spec(a.shape) for a in args[1:]],
        out_specs=tok,
        scratch_shapes=[pltpu.VMEM((nb, CONV_W - 1, V7X_SUBLANES, D_RNN), F32),
                        pltpu.VMEM((nb, V7X_SUBLANES, D_RNN), F32)],
        compiler_params=_params("parallel", "arbitrary"),
        name="rglru",
    )(*args)


def kernel(x, norm_ffn, ffn_w_gate, ffn_w_up, ffn_w_down, norm_mix, ab_w_in, ab_w_out, att_rel_bias, rwkv_mu, rwkv_w0, rwkv_w_up, rwkv_a0, rwkv_a_up, rwkv_g_up, rwkv_k_k, rwkv_k_a, rwkv_r_k, rwkv_ln_w, rwkv_ln_b, rwkv_v0, rwkv_v_down, rwkv_v_up, c_w_in, c_conv_w, c_conv_b, c_wa, c_ba, c_wx, c_bx, c_lambda, c_w_out, norm_final):
    b, s, d = x.shape
    t = b * s
    depth = norm_mix.shape[0]
    ffn_w = (ffn_w_gate.astype(F32), ffn_w_up.astype(F32), ffn_w_down.astype(F32))
    ffn_calls = [(l, j) for l in range(depth) for j in range(2)]
    gfin = norm_final.reshape(1, d).astype(F32)
    wb = tuple(w[0, 0].astype(BF16) for w in ffn_w)

    def ffn(x, wb, lead, mixer=None, extra=()):
        idx = ffn_calls.index(lead)
        last = idx + 1 == len(ffn_calls)
        casts = [] if last else [(w, ffn_calls[idx + 1]) for w in ffn_w]
        x, cast = _ffn(x, norm_ffn[lead].reshape(1, d), wb, gfin, last, mixer, casts + list(extra))
        return x, cast[:len(casts)], cast[len(casts):]

    x = x.astype(F32).reshape(t, d)
    v_first = None
    for l in range(depth):
        mixer = None
        if l % 2 == 0:
            i = l // 2
            x, wb, (w_in, w_out) = ffn(x, wb, (l, 0), extra=[(ab_w_in.astype(F32), (i,)), (ab_w_out.astype(F32), (i,))])
            qkv, pb = _ab_in(x, norm_mix[l].reshape(1, d), w_in)
            oa = _attention(qkv.reshape(b, s, -1), att_rel_bias[i])
            prm = dict(mu=rwkv_mu[i], w0=rwkv_w0[i], w_up=rwkv_w_up[i], a0=rwkv_a0[i], a_up=rwkv_a_up[i],
                       g_up=rwkv_g_up[i], k_k=rwkv_k_k[i], k_a=rwkv_k_a[i], r_k=rwkv_r_k[i],
                       ln_w=rwkv_ln_w[i], ln_b=rwkv_ln_b[i])
            if i == 0:
                ob, v_first = _rwkv(pb.reshape(b, s, -1), None, prm, None)
            else:
                ob = _rwkv(pb.reshape(b, s, -1), v_first, prm,
                           (rwkv_v0[i - 1], rwkv_v_down[i - 1], rwkv_v_up[i - 1]))
            mixer = (oa.reshape(t, -1), ob.reshape(t, -1), w_out)
        else:
            j = l // 2
            x, wb, (cw_in, cw_out) = ffn(x, wb, (l, 0), extra=[(c_w_in.astype(F32), (j,)), (c_w_out.astype(F32), (j,))])
            x = _rglru(x.reshape(b, s, d), norm_mix[l], cw_in, c_conv_w[j], c_conv_b[j], c_wa[j], c_ba[j],
                       c_wx[j], c_bx[j], c_lambda[j], cw_out).reshape(t, d)
        x, wb, _ = ffn(x, wb, (l, 1), mixer)
    return x.reshape(b, s, d)
```

```python
import functools

import jax
import jax.numpy as jnp
import numpy as np
from jax import lax
from jax.experimental import pallas as pl
from jax.experimental.pallas import tpu as pltpu

F32 = jnp.float32
BF16 = jnp.bfloat16

D_MODEL = 1024
DEPTH = 4
CHUNK = 64
LEFT_CHUNKS = 8
A_HEADS = 8
A_HEAD_DIM = 64
A_WIDTH = A_HEADS * A_HEAD_DIM
REL_CLIP = 128
B_HEADS = 8
B_HEAD_SIZE = 64
B_WIDTH = B_HEADS * B_HEAD_SIZE
W_LORA = 64
A_LORA = 64
G_LORA = 128
V_LORA = 32
DECAY_SCALE = 0.606531
GN_EPS = 64e-5
B_COLS = 3 * B_WIDTH + W_LORA + A_LORA + G_LORA
D_RNN = 1280
C_BLOCKS = 10
C_BLOCK_W = D_RNN // C_BLOCKS
CONV_W = 4
RG_C = 8.0
D_FF = 2816
NORM_EPS = 1e-6
NEG_INF = -1e30

V7X_LANES = 128
V7X_SUBLANES = 8
V7X_BF16_SUBLANES = 16
V7X_MXU_DIM = 256
V7X_VMEM_BYTES = 64 * 1024 * 1024

FFN_ROWS = 512
FFN_CHUNKS = (0, 768, 1536, 2304, 2816)
PROJ_ROWS = 1024
ATT_ROWS = 512
ATT_PREV = LEFT_CHUNKS * CHUNK
RWKV_CHUNK = 64
RWKV_STEP_CHUNKS = 4
RGLRU_ROWS = 256
RGLRU_BATCH = 4
VMEM_LIMIT = 56 * 1024 * 1024


def _params(*sem):
    return pltpu.CompilerParams(dimension_semantics=sem, vmem_limit_bytes=VMEM_LIMIT)


def _const_spec(shape):
    nd = len(shape)
    return pl.BlockSpec(shape, lambda *_: (0,) * nd, pipeline_mode=pl.Buffered(1))


def _mm(a, b):
    return jnp.dot(a.astype(BF16), b.astype(BF16), preferred_element_type=F32)


def _mm_nt(a, b):
    return lax.dot_general(a.astype(BF16), b.astype(BF16), (((1,), (1,)), ((), ())),
                           preferred_element_type=F32)


def _mm_tn(a, b):
    return lax.dot_general(a.astype(BF16), b.astype(BF16), (((0,), (0,)), ((), ())),
                           preferred_element_type=F32)


def _rms(x, g):
    ms = jnp.mean(x * x, axis=-1, keepdims=True)
    return x * lax.rsqrt(ms + NORM_EPS) * g


def _sigmoid(x):
    return 0.5 * jnp.tanh(0.5 * x) + 0.5


def _slice_rows(total, steps):
    k = 1
    while (total * k) % steps or (total * k // steps) % V7X_BF16_SUBLANES:
        k *= 2
    return total * k // steps, k


def _ffn_body(*refs, final_norm, mixer_out, n_casts):
    refs = list(refs)
    if mixer_out:
        oa_ref, ob_ref, wo_ref = refs[:3]
        refs = refs[3:]
    x_ref, g_ref, wg_ref, wu_ref, wd_ref, gf_ref = refs[:6]
    refs = refs[6:]
    o_ref = refs[n_casts]
    for src, dst in zip(refs[:n_casts], refs[n_casts + 1:]):
        dst[...] = src[...].astype(BF16)
    x = x_ref[...]
    if mixer_out:
        x = x + jnp.dot(oa_ref[...], wo_ref[:A_WIDTH, :], preferred_element_type=F32)
        x = x + jnp.dot(ob_ref[...], wo_ref[A_WIDTH:, :], preferred_element_type=F32)
    h = _rms(x, g_ref[...]).astype(BF16)
    acc = None
    for lo, hi in zip(FFN_CHUNKS[:-1], FFN_CHUNKS[1:]):
        gt = jnp.dot(h, wg_ref[:, lo:hi], preferred_element_type=F32)
        up = jnp.dot(h, wu_ref[:, lo:hi], preferred_element_type=F32)
        act = (gt * _sigmoid(gt) * up).astype(BF16)
        d = jnp.dot(act, wd_ref[lo:hi, :], preferred_element_type=F32)
        acc = d if acc is None else acc + d
    y = x + 0.5 * acc
    if final_norm:
        y = _rms(y, gf_ref[...])
    o_ref[...] = y


def _ffn(x, g, wb, gf, final_norm, mixer=None, casts=()):
    t, d = x.shape
    tm = min(FFN_ROWS, t)
    steps = t // tm
    row = pl.BlockSpec((tm, d), lambda i: (i, 0))
    args = [x, g, *wb, gf]
    in_specs = [row, _const_spec((1, d))] + [_const_spec(w.shape) for w in wb] + [_const_spec((1, d))]
    if mixer is not None:
        oa, ob, wo = mixer
        args = [oa, ob, wo] + args
        in_specs = [pl.BlockSpec((tm, oa.shape[1]), lambda i: (i, 0)), pl.BlockSpec((tm, ob.shape[1]), lambda i: (i, 0)),
                    _const_spec(wo.shape)] + in_specs
    out_shape = [jax.ShapeDtypeStruct((t, d), F32)]
    out_specs = [row]
    for w, lead in casts:
        lead = tuple(lead)
        nr, nc = w.shape[-2:]
        rows, k = _slice_rows(nr, steps)
        args.append(w)
        in_specs.append(pl.BlockSpec((None,) * len(lead) + (rows, nc), lambda i, k=k, lead=lead: lead + (i // k, 0)))
        out_shape.append(jax.ShapeDtypeStruct((nr, nc), BF16))
        out_specs.append(pl.BlockSpec((rows, nc), lambda i, k=k: (i // k, 0)))
    out = pl.pallas_call(
        functools.partial(_ffn_body, final_norm=final_norm, mixer_out=mixer is not None, n_casts=len(casts)),
        out_shape=tuple(out_shape),
        grid=(steps,),
        in_specs=in_specs,
        out_specs=tuple(out_specs),
        compiler_params=_params("arbitrary"),
        name="ffn_final" if final_norm else ("ffn_mix" if mixer is not None else "ffn"),
    )(*args)
    return out[0], tuple(out[1:])


def _ab_in_body(x_ref, g_ref, w_ref, qkv_ref, pb_ref):
    h = _rms(x_ref[...], g_ref[...]).astype(BF16)
    na = qkv_ref.shape[-1]
    qkv_ref[...] = jnp.dot(h, w_ref[:, :na], preferred_element_type=F32).astype(BF16)
    pb_ref[...] = jnp.dot(h, w_ref[:, na:], preferred_element_type=F32)


def _ab_in(x, g, w):
    t, d = x.shape
    n = w.shape[-1]
    na = 3 * A_WIDTH
    tm = min(PROJ_ROWS, t)
    return pl.pallas_call(
        _ab_in_body,
        out_shape=(jax.ShapeDtypeStruct((t, na), BF16), jax.ShapeDtypeStruct((t, n - na), F32)),
        grid=(t // tm,),
        in_specs=[pl.BlockSpec((tm, d), lambda i: (i, 0)), _const_spec((1, d)), _const_spec(w.shape)],
        out_specs=(pl.BlockSpec((tm, na), lambda i: (i, 0)), pl.BlockSpec((tm, n - na), lambda i: (i, 0))),
        compiler_params=_params("parallel"),
        name="ab_in",
    )(x, g, w)


def _attn_body(*refs, n_kblocks):
    q_ref = refs[0]
    k_refs = refs[1:1 + n_kblocks]
    v_refs = refs[1 + n_kblocks:1 + 2 * n_kblocks]
    trow_ref = refs[1 + 2 * n_kblocks]
    o_ref = refs[2 + 2 * n_kblocks]
    bias_ref = refs[3 + 2 * n_kblocks]
    qb = q_ref.shape[0]
    i = pl.program_id(1)
    n_prev = n_kblocks - 1

    @pl.when(i == 0)
    def _():
        w = bias_ref.shape[2]
        rr = lax.broadcasted_iota(jnp.int32, (qb, w), 0) // CHUNK
        cc = lax.broadcasted_iota(jnp.int32, (qb, w), 1) // CHUNK
        band = jnp.logical_and(cc >= rr, cc <= rr + LEFT_CHUNKS)
        for h in range(A_HEADS):
            t = jnp.broadcast_to(trow_ref[h], (qb, trow_ref.shape[2]))
            toe = pltpu.roll(t, 0, 1, stride=1, stride_axis=0)
            bias_ref[h] = jnp.where(band, toe[:, :w], NEG_INF)

    pens = [jnp.where(i >= n_prev - j, 0.0, NEG_INF).astype(F32) for j in range(n_prev)] + [None]
    lane = lax.broadcasted_iota(jnp.int32, (1, 2 * A_HEAD_DIM), 1)
    first = lane < A_HEAD_DIM
    scale = A_HEAD_DIM ** -0.5
    for pr in range(A_HEADS // 2):
        ls = slice(2 * A_HEAD_DIM * pr, 2 * A_HEAD_DIM * (pr + 1))
        q_pair = q_ref[:, ls]
        ks = [r[:, ls] for r in k_refs]
        vs = [r[:, ls] for r in v_refs]
        outs = []
        for hh in range(2):
            head = 2 * pr + hh
            sel = first if hh == 0 else jnp.logical_not(first)
            qm = jnp.where(sel, scale, 0.0).astype(BF16)
            qh = q_pair * qm
            s = []
            for j in range(n_kblocks):
                sj = lax.dot_general(qh, ks[j], (((1,), (1,)), ((), ())), preferred_element_type=F32)
                sj = sj + bias_ref[head, :, j * qb:(j + 1) * qb]
                if pens[j] is not None:
                    sj = sj + pens[j]
                s.append(sj)
            m = s[0].max(axis=-1, keepdims=True)
            for sj in s[1:]:
                m = jnp.maximum(m, sj.max(axis=-1, keepdims=True))
            l = None
            pv = None
            for j in range(n_kblocks):
                p = jnp.exp(s[j] - m)
                lj = p.sum(axis=-1, keepdims=True)
                pvj = jnp.dot(p.astype(BF16), vs[j], preferred_element_type=F32)
                l = lj if l is None else l + lj
                pv = pvj if pv is None else pv + pvj
            outs.append(pv / l)
        o_ref[:, ls] = jnp.where(first, outs[0], outs[1]).astype(BF16)


def _attn_bias_row(rel_bias, qb):
    nh = rel_bias.shape[0]
    w = ATT_PREV + qb
    period = w + qb
    near = 2 * REL_CLIP + 1
    assert ATT_PREV >= REL_CLIP and qb >= REL_CLIP and period % V7X_LANES == 0
    far_past = jnp.broadcast_to(rel_bias[:, near - 1:], (nh, ATT_PREV - REL_CLIP))
    far_future = jnp.broadcast_to(rel_bias[:, :1], (nh, qb - REL_CLIP))
    wrapped = jnp.broadcast_to(rel_bias[:, near - 1:], (nh, qb - 1))
    t = jnp.concatenate([far_past, rel_bias[:, ::-1], far_future, wrapped], axis=1).astype(F32)
    assert t.shape[1] == period
    return t.reshape(nh, 1, period)


def _attention(qkv, rel_bias):
    b, s, _ = qkv.shape
    qb = min(ATT_ROWS, s)
    assert ATT_PREV % qb == 0 and s % qb == 0
    n_prev = ATT_PREV // qb
    n_kblocks = n_prev + 1
    trow = _attn_bias_row(rel_bias, qb)

    def blk(col, back):
        return pl.BlockSpec((None, qb, A_WIDTH), lambda bi, i: (bi, jnp.maximum(i - back, 0), col))

    in_specs = ([blk(0, 0)] + [blk(1, n_prev - j) for j in range(n_kblocks)]
                + [blk(2, n_prev - j) for j in range(n_kblocks)] + [_const_spec(trow.shape)])
    return pl.pallas_call(
        functools.partial(_attn_body, n_kblocks=n_kblocks),
        out_shape=jax.ShapeDtypeStruct((b, s, A_WIDTH), BF16),
        grid=(b, s // qb),
        in_specs=in_specs,
        out_specs=pl.BlockSpec((None, qb, A_WIDTH), lambda bi, i: (bi, i, 0)),
        scratch_shapes=[pltpu.VMEM((A_HEADS, qb, ATT_PREV + qb), F32)],
        compiler_params=_params("parallel", "arbitrary"),
        name="attention",
    )(*([qkv] * (1 + 2 * n_kblocks)), trow)


def _split3(x):
    h1 = x.astype(BF16)
    r1 = x - h1.astype(F32)
    h2 = r1.astype(BF16)
    h3 = (r1 - h2.astype(F32)).astype(BF16)
    return h1, h2, h3


def _rwkv_body(*refs, has_vmix, c):
    pb_ref = refs[0]
    state_ref, carry_ref = refs[-2:]

    @pl.when(pl.program_id(0) == 0)
    def _():
        state_ref[...] = jnp.zeros_like(state_ref)
        carry_ref[...] = jnp.zeros_like(carry_ref)

    for ci in range(pb_ref.shape[1] // c):
        _rwkv_chunk(slice(ci * c, (ci + 1) * c), refs, has_vmix)


def _rwkv_chunk(ts, refs, has_vmix):
    if has_vmix:
        (pb_ref, vf_ref, mu_ref, w0_ref, wup_ref, a0_ref, aup_ref, gup_ref, kk_ref, ka_ref, rk_ref,
         lnw_ref, lnb_ref, v0_ref, vdn_ref, vup_ref, o_ref, state_ref, carry_ref) = refs
        vraw_ref = None
    else:
        (pb_ref, mu_ref, w0_ref, wup_ref, a0_ref, aup_ref, gup_ref, kk_ref, ka_ref, rk_ref,
         lnw_ref, lnb_ref, o_ref, vraw_ref, state_ref, carry_ref) = refs
    nb, _, ncol = pb_ref.shape
    c = ts.stop - ts.start
    n = B_HEAD_SIZE
    rows = nb * c
    p = pb_ref[:, ts, :].reshape(rows, ncol)
    row = lax.broadcasted_iota(jnp.int32, (rows, 1), 0)
    prev = pltpu.roll(p, 1, axis=0)
    for b in range(nb):
        prev = jnp.where(row == b * c, carry_ref[b, 0:1, :], prev)
    for b in range(nb):
        carry_ref[b, 0:1, :] = p[(b + 1) * c - 1:(b + 1) * c, :]
    ps = p + (prev - p) * mu_ref[...]
    r = ps[:, 0:B_WIDTH]
    k = ps[:, B_WIDTH:2 * B_WIDTH]
    v = ps[:, 2 * B_WIDTH:3 * B_WIDTH]
    wad = ps[:, 3 * B_WIDTH:3 * B_WIDTH + W_LORA + A_LORA]
    gd = ps[:, 3 * B_WIDTH + W_LORA + A_LORA:]
    if has_vmix:
        mix = _sigmoid(v0_ref[...] + _mm(_mm(v, vdn_ref[...]), vup_ref[...]))
        v = v + (vf_ref[:, ts, :].reshape(rows, B_WIDTH) - v) * mix
    else:
        vraw_ref[:, ts, :] = v.reshape(nb, c, B_WIDTH)
    wlog = -DECAY_SCALE * _sigmoid(w0_ref[...] + _mm(jnp.tanh(wad), wup_ref[...]))
    a = _sigmoid(a0_ref[...] + _mm(wad, aup_ref[...]))
    g = _mm(_sigmoid(gd), gup_ref[...])
    kkf = k * kk_ref[...]
    kka = kkf * a
    kmod = k * (1.0 + (a - 1.0) * ka_ref[...])
    rkr = r * kmod * rk_ref[...]

    ti = lax.broadcasted_iota(jnp.int32, (c, 3 * c), 0)
    tj = lax.broadcasted_iota(jnp.int32, (c, 3 * c), 1) % c
    tri3 = jnp.where(tj <= ti, 1.0, 0.0).astype(BF16)
    lcum, lend = [], []
    for b in range(nb):
        lb = jnp.dot(tri3, jnp.concatenate(_split3(wlog[b * c:(b + 1) * c]), axis=0),
                     preferred_element_type=F32)
        lcum.append(lb)
        lend.append(jnp.broadcast_to(lb[c - 1:c, :], lb.shape))
    lcum = jnp.concatenate(lcum, axis=0)
    lend = jnp.concatenate(lend, axis=0)
    e_neg = jnp.exp(-lcum)
    e_end = jnp.exp(lend - lcum)
    w_end = jnp.exp(lend)

    pair = 2 * n
    npair = B_HEADS // 2
    lo = lax.broadcasted_iota(jnp.int32, (1, 1, pair), 2) < n

    def slabs(q):
        return jnp.stack([q[b * c:(b + 1) * c, p * pair:(p + 1) * pair]
                          for b in range(nb) for p in range(npair)], axis=0)

    def unslabs(q):
        return jnp.concatenate(
            [jnp.concatenate([q[b * npair + p] for p in range(npair)], axis=-1) for b in range(nb)], axis=0)

    def head_sum(q):
        s_lo = jnp.sum(jnp.where(lo, q, 0.0), axis=-1, keepdims=True)
        s_hi = jnp.sum(jnp.where(lo, 0.0, q), axis=-1, keepdims=True)
        return jnp.where(lo, s_lo, s_hi)

    def bd(q):
        qb = q.astype(BF16)
        zero = jnp.zeros_like(qb)
        return jnp.concatenate([jnp.where(lo, qb, zero), jnp.where(lo, zero, qb)], axis=1)

    def bmm(lhs, rhs):
        return jnp.einsum("gij,gjv->giv", lhs.astype(BF16), rhs.astype(BF16), preferred_element_type=F32)

    kk_s = slabs(kkf)
    rs = lax.rsqrt(jnp.maximum(head_sum(kk_s * kk_s), 1e-24))
    x = jnp.concatenate([slabs(-kkf * jnp.exp(lcum - wlog)) * rs, slabs(r * jnp.exp(lcum))], axis=1)
    bt = slabs(kka * e_neg) * rs
    kt = slabs(kmod * e_neg)
    bk_end = jnp.concatenate([slabs(kka * e_end) * rs, slabs(kmod * e_end)], axis=1)
    vh = slabs(v)
    s0 = state_ref[...]
    xb = x.astype(BF16)
    ybd = jnp.concatenate([bd(bt), bd(kt)], axis=1).astype(BF16)
    gm = jnp.einsum("gik,gjk->gij", xb, ybd, preferred_element_type=F32)
    xs = jnp.einsum("gik,gvk->giv", xb, s0.astype(BF16), preferred_element_type=F32)
    ri = lax.broadcasted_iota(jnp.int32, (1, c, 2 * c), 1)
    ci = lax.broadcasted_iota(jnp.int32, (1, c, 2 * c), 2) % c
    strict = ci < ri
    incl = ci <= ri
    m_ab = jnp.where(strict, gm[:, :c, :2 * c], 0.0)
    m_ak = jnp.where(strict, gm[:, :c, 2 * c:], 0.0)
    m_rb = jnp.where(incl, gm[:, c:, :2 * c], 0.0)
    m_rk = jnp.where(incl, gm[:, c:, 2 * c:], 0.0)
    bdv = bd(vh)
    u = xs[:, :c] + bmm(m_ak, bdv)
    pw = m_ab.astype(BF16)
    span = 1
    while True:
        u = u + bmm(pw, bd(u))
        span *= 2
        if span >= c:
            break
        pw = bmm(pw, bd(pw)).astype(BF16)
    o = xs[:, c:] + bmm(jnp.concatenate([m_rb, m_rk], axis=2), jnp.concatenate([bd(u), bdv], axis=1))
    w_end_s = jnp.stack([w_end[b * c:b * c + 1, p * pair:(p + 1) * pair]
                         for b in range(nb) for p in range(npair)], axis=0)
    uv = jnp.concatenate([u, vh], axis=1)
    upd = jnp.einsum("gjv,gjk->gvk", uv.astype(BF16), bk_end.astype(BF16), preferred_element_type=F32)
    same_head = (lax.broadcasted_iota(jnp.int32, (1, pair, pair), 1) < n) == lo
    state_ref[...] = s0 * w_end_s + jnp.where(same_head, upd, 0.0)
    mean = head_sum(o) * (1.0 / n)
    d = o - mean
    var = head_sum(d * d) * (1.0 / n)
    o = d * lax.rsqrt(var + GN_EPS)
    bonus = head_sum(slabs(rkr)) * vh
    out = (unslabs(o) * lnw_ref[...] + lnb_ref[...] + unslabs(bonus)) * g
    o_ref[:, ts, :] = out.astype(BF16).reshape(nb, c, B_WIDTH)


def _rwkv(pb, v_first, prm, vmix):
    b, s, ncol = pb.shape
    c = min(RWKV_CHUNK, s)
    ts = min(RWKV_STEP_CHUNKS * c, s)
    has_vmix = vmix is not None
    tok = lambda w: pl.BlockSpec((b, ts, w), lambda i: (0, i, 0))
    vec = lambda a: a.reshape(1, -1).astype(F32)
    lw = W_LORA + A_LORA
    wup = jnp.zeros((lw, B_WIDTH), F32).at[:W_LORA].set(prm["w_up"]).astype(BF16)
    aup = jnp.zeros((lw, B_WIDTH), F32).at[W_LORA:].set(prm["a_up"]).astype(BF16)
    common = [vec(prm["mu"]), vec(prm["w0"]), wup, vec(prm["a0"]), aup, prm["g_up"].astype(BF16),
              vec(prm["k_k"]), vec(prm["k_a"]), vec(prm["r_k"]), vec(prm["ln_w"]), vec(prm["ln_b"])]
    if has_vmix:
        v0, v_down, v_up = vmix
        vdn = jnp.zeros((B_WIDTH, V7X_LANES), F32).at[:, :V_LORA].set(v_down).astype(BF16)
        vup = jnp.zeros((V7X_LANES, B_WIDTH), F32).at[:V_LORA].set(v_up).astype(BF16)
        args = [pb, v_first] + common + [vec(v0), vdn, vup]
        in_specs = [tok(ncol), tok(B_WIDTH)] + [_const_spec(a.shape) for a in args[2:]]
        out_shape = jax.ShapeDtypeStruct((b, s, B_WIDTH), BF16)
        out_specs = tok(B_WIDTH)
    else:
        args = [pb] + common
        in_specs = [tok(ncol)] + [_const_spec(a.shape) for a in args[1:]]
        out_shape = (jax.ShapeDtypeStruct((b, s, B_WIDTH), BF16), jax.ShapeDtypeStruct((b, s, B_WIDTH), F32))
        out_specs = (tok(B_WIDTH), tok(B_WIDTH))
    return pl.pallas_call(
        functools.partial(_rwkv_body, has_vmix=has_vmix, c=c),
        out_shape=out_shape,
        grid=(s // ts,),
        in_specs=in_specs,
        out_specs=out_specs,
        scratch_shapes=[pltpu.VMEM((b * B_HEADS // 2, 2 * B_HEAD_SIZE, 2 * B_HEAD_SIZE), F32),
                        pltpu.VMEM((b, V7X_SUBLANES, ncol), F32)],
        compiler_params=_params("arbitrary"),
        name="rwkv_mix" if has_vmix else "rwkv",
    )(*args)


def _gelu_tanh(x):
    return 0.5 * x * (1.0 + jnp.tanh(np.sqrt(2.0 / np.pi).astype(np.float32) * (x + 0.044715 * (x * x * x))))


def _rglru_tile(bi, x_ref, g_ref, win_ref, cw_ref, cb_ref, wax_ref, ba_ref, bx_ref, lam_ref, wout_ref,
                o_ref, xprev_ref, hprev_ref):
    tr = x_ref.shape[1]
    sub = V7X_SUBLANES
    nrow = tr // sub
    ntail = CONV_W - 1
    x = x_ref[bi]
    h = _rms(x, g_ref[...]).astype(BF16)
    i0 = lax.broadcasted_iota(jnp.int32, (tr, tr), 0)
    i1 = lax.broadcasted_iota(jnp.int32, (tr, tr), 1)
    perm = jnp.where(i1 == (i0 % sub) * nrow + i0 // sub, 1.0, 0.0).astype(BF16)
    unperm = jnp.where(i0 == (i1 % sub) * nrow + i1 // sub, 1.0, 0.0).astype(BF16)
    hp = jnp.dot(perm, h, preferred_element_type=F32).astype(BF16)
    gate = _gelu_tanh(jnp.dot(hp, win_ref[:, :D_RNN], preferred_element_type=F32))
    xb3 = jnp.dot(hp, win_ref[:, D_RNN:], preferred_element_type=F32).reshape(nrow, sub, D_RNN)
    seg = lax.broadcasted_iota(jnp.int32, (1, sub, 1), 1)
    prev3 = xprev_ref[bi]
    tails = [jnp.where(seg == 0, pltpu.roll(prev3[q:q + 1], 1, axis=1),
                       pltpu.roll(xb3[nrow - ntail + q:nrow - ntail + q + 1], 1, axis=1)) for q in range(ntail)]
    xprev_ref[bi] = xb3[nrow - ntail:]
    xc3 = xb3 * cw_ref[CONV_W - 1:CONV_W, :] + cb_ref[...]
    for sft in range(1, CONV_W):
        shifted = jnp.concatenate(tails[ntail - sft:] + [xb3[:nrow - sft]], axis=0)
        xc3 = xc3 + shifted * cw_ref[CONV_W - 1 - sft:CONV_W - sft, :]
    xc = xc3.reshape(tr, D_RNN)
    xcb = xc.astype(BF16)
    pw = 2 * C_BLOCK_W
    ra, rx = [], []
    for q in range(C_BLOCKS // 2):
        t = jnp.dot(xcb[:, q * pw:(q + 1) * pw], wax_ref[q], preferred_element_type=F32)
        ra.append(t[:, :pw])
        rx.append(t[:, pw:])
    rgate = _sigmoid(jnp.concatenate(ra, axis=-1) + ba_ref[...])
    igate = _sigmoid(jnp.concatenate(rx, axis=-1) + bx_ref[...])
    z = -lam_ref[...]
    softplus = jnp.maximum(z, 0.0) + jnp.log(1.0 + jnp.exp(-jnp.abs(z)))
    a = jnp.exp(-RG_C * rgate * softplus)
    mult = jnp.sqrt(jnp.maximum(1.0 - a * a, 0.0))
    bb = mult * igate * xc
    a3 = a.reshape(nrow, sub, D_RNN)
    b3 = bb.reshape(nrow, sub, D_RNN)
    acs, bcs = [a3[0]], [b3[0]]
    for p in range(1, nrow):
        bcs.append(a3[p] * bcs[-1] + b3[p])
        acs.append(a3[p] * acs[-1])
    ga, gb = acs[-1], bcs[-1]
    srow = lax.broadcasted_iota(jnp.int32, (sub, 1), 0)
    sft = 1
    while sft < sub:
        keep = srow >= sft
        a_sh = jnp.where(keep, pltpu.roll(ga, sft, axis=0), 1.0)
        b_sh = jnp.where(keep, pltpu.roll(gb, sft, axis=0), 0.0)
        gb = ga * b_sh + gb
        ga = ga * a_sh
        sft *= 2
    hprev = hprev_ref[bi, 0:1, :]
    hend = ga * hprev + gb
    hin = jnp.where(srow == 0, hprev, pltpu.roll(hend, 1, axis=0))
    hprev_ref[bi, 0:1, :] = hend[sub - 1:sub, :]
    hs = jnp.stack([acs[p] * hin + bcs[p] for p in range(nrow)], axis=0).reshape(tr, D_RNN)
    yp = (hs * gate).astype(BF16)
    y = jnp.dot(unperm, yp, preferred_element_type=F32).astype(BF16)
    o_ref[bi] = x + jnp.dot(y, wout_ref[...], preferred_element_type=F32)


def _rglru_body(x_ref, *rest):
    xprev_ref, hprev_ref = rest[-2:]

    @pl.when(pl.program_id(1) == 0)
    def _():
        xprev_ref[...] = jnp.zeros_like(xprev_ref)
        hprev_ref[...] = jnp.zeros_like(hprev_ref)

    for bi in range(x_ref.shape[0]):
        _rglru_tile(bi, x_ref, *rest)


def _rglru(x, g, w_in, conv_w, conv_b, wa, ba, wx, bx, lam, w_out):
    b, s, d = x.shape
    tr = min(RGLRU_ROWS, s)
    zero = jnp.zeros((C_BLOCKS // 2, C_BLOCK_W, C_BLOCK_W), wa.dtype)
    even = jnp.concatenate([wa[0::2], zero, wx[0::2], zero], axis=2)
    odd = jnp.concatenate([zero, wa[1::2], zero, wx[1::2]], axis=2)
    wax = jnp.concatenate([even, odd], axis=1).astype(BF16)
    vec = lambda a: a.reshape(1, -1).astype(F32)
    args = [x, vec(g), w_in, conv_w.astype(F32), vec(conv_b), wax, vec(ba), vec(bx), vec(lam), w_out]
    nb = min(RGLRU_BATCH, b)
    tok = pl.BlockSpec((nb, tr, d), lambda bi, i: (bi, i, 0))
    return pl.pallas_call(
        _rglru_body,
        out_shape=jax.ShapeDtypeStruct((b, s, d), F32),
        grid=(b // nb, s // tr),
        in_specs=[tok] + [_const_spec(a.shape) for a in args[1:]],
        out_specs=tok,
        scratch_shapes=[pltpu.VMEM((nb, CONV_W - 1, V7X_SUBLANES, D_RNN), F32),
                        pltpu.VMEM((nb, V7X_SUBLANES, D_RNN), F32)],
        compiler_params=_params("parallel", "arbitrary"),
        name="rglru",
    )(*args)


def kernel(x, norm_ffn, ffn_w_gate, ffn_w_up, ffn_w_down, norm_mix, ab_w_in, ab_w_out, att_rel_bias, rwkv_mu, rwkv_w0, rwkv_w_up, rwkv_a0, rwkv_a_up, rwkv_g_up, rwkv_k_k, rwkv_k_a, rwkv_r_k, rwkv_ln_w, rwkv_ln_b, rwkv_v0, rwkv_v_down, rwkv_v_up, c_w_in, c_conv_w, c_conv_b, c_wa, c_ba, c_wx, c_bx, c_lambda, c_w_out, norm_final):
    b, s, d = x.shape
    t = b * s
    depth = norm_mix.shape[0]
    ffn_w = (ffn_w_gate.astype(F32), ffn_w_up.astype(F32), ffn_w_down.astype(F32))
    ffn_calls = [(l, j) for l in range(depth) for j in range(2)]
    gfin = norm_final.reshape(1, d).astype(F32)
    wb = tuple(w[0, 0].astype(BF16) for w in ffn_w)

    def ffn(x, wb, lead, mixer=None, extra=()):
        idx = ffn_calls.index(lead)
        last = idx + 1 == len(ffn_calls)
        casts = [] if last else [(w, ffn_calls[idx + 1]) for w in ffn_w]
        x, cast = _ffn(x, norm_ffn[lead].reshape(1, d), wb, gfin, last, mixer, casts + list(extra))
        return x, cast[:len(casts)], cast[len(casts):]

    x = x.astype(F32).reshape(t, d)
    v_first = None
    for l in range(depth):
        mixer = None
        if l % 2 == 0:
            i = l // 2
            x, wb, (w_in, w_out) = ffn(x, wb, (l, 0), extra=[(ab_w_in.astype(F32), (i,)), (ab_w_out.astype(F32), (i,))])
            qkv, pb = _ab_in(x, norm_mix[l].reshape(1, d), w_in)
            oa = _attention(qkv.reshape(b, s, -1), att_rel_bias[i])
            prm = dict(mu=rwkv_mu[i], w0=rwkv_w0[i], w_up=rwkv_w_up[i], a0=rwkv_a0[i], a_up=rwkv_a_up[i],
                       g_up=rwkv_g_up[i], k_k=rwkv_k_k[i], k_a=rwkv_k_a[i], r_k=rwkv_r_k[i],
                       ln_w=rwkv_ln_w[i], ln_b=rwkv_ln_b[i])
            if i == 0:
                ob, v_first = _rwkv(pb.reshape(b, s, -1), None, prm, None)
            else:
                ob = _rwkv(pb.reshape(b, s, -1), v_first, prm,
                           (rwkv_v0[i - 1], rwkv_v_down[i - 1], rwkv_v_up[i - 1]))
            mixer = (oa.reshape(t, -1), ob.reshape(t, -1), w_out)
        else:
            j = l // 2
            x, wb, (cw_in, cw_out) = ffn(x, wb, (l, 0), extra=[(c_w_in.astype(F32), (j,)), (c_w_out.astype(F32), (j,))])
            x = _rglru(x.reshape(b, s, d), norm_mix[l], cw_in, c_conv_w[j], c_conv_b[j], c_wa[j], c_ba[j],
                       c_wx[j], c_bx[j], c_lambda[j], cw_out).reshape(t, d)
        x, wb, _ = ffn(x, wb, (l, 1), mixer)
    return x.reshape(b, s, d)
```

```python
import functools

import jax
import jax.numpy as jnp
import numpy as np
from jax import lax
from jax.experimental import pallas as pl
from jax.experimental.pallas import tpu as pltpu

F32 = jnp.float32
BF16 = jnp.bfloat16

D_MODEL = 1024
DEPTH = 4
CHUNK = 64
LEFT_CHUNKS = 8
A_HEADS = 8
A_HEAD_DIM = 64
A_WIDTH = A_HEADS * A_HEAD_DIM
REL_CLIP = 128
B_HEADS = 8
B_HEAD_SIZE = 64
B_WIDTH = B_HEADS * B_HEAD_SIZE
W_LORA = 64
A_LORA = 64
G_LORA = 128
V_LORA = 32
DECAY_SCALE = 0.606531
GN_EPS = 64e-5
B_COLS = 3 * B_WIDTH + W_LORA + A_LORA + G_LORA
D_RNN = 1280
C_BLOCKS = 10
C_BLOCK_W = D_RNN // C_BLOCKS
CONV_W = 4
RG_C = 8.0
D_FF = 2816
NORM_EPS = 1e-6
NEG_INF = -1e30

V7X_LANES = 128
V7X_SUBLANES = 8
V7X_BF16_SUBLANES = 16
V7X_MXU_DIM = 256
V7X_VMEM_BYTES = 64 * 1024 * 1024

FFN_ROWS = 1024
FFN_CHUNKS = (0, 768, 1536, 2304, 2816)
PROJ_ROWS = 1024
ATT_ROWS = 512
ATT_PREV = LEFT_CHUNKS * CHUNK
RWKV_CHUNK = 64
RWKV_STEP_CHUNKS = 4
RGLRU_ROWS = 256
RGLRU_BATCH = 4
VMEM_LIMIT = 60 * 1024 * 1024


def _params(*sem):
    return pltpu.CompilerParams(dimension_semantics=sem, vmem_limit_bytes=VMEM_LIMIT)


def _const_spec(shape):
    nd = len(shape)
    return pl.BlockSpec(shape, lambda *_: (0,) * nd, pipeline_mode=pl.Buffered(1))


def _mm(a, b):
    return jnp.dot(a.astype(BF16), b.astype(BF16), preferred_element_type=F32)


def _mm_nt(a, b):
    return lax.dot_general(a.astype(BF16), b.astype(BF16), (((1,), (1,)), ((), ())),
                           preferred_element_type=F32)


def _mm_tn(a, b):
    return lax.dot_general(a.astype(BF16), b.astype(BF16), (((0,), (0,)), ((), ())),
                           preferred_element_type=F32)


def _rms(x, g):
    ms = jnp.mean(x * x, axis=-1, keepdims=True)
    return x * lax.rsqrt(ms + NORM_EPS) * g


def _sigmoid(x):
    return 0.5 * jnp.tanh(0.5 * x) + 0.5


def _slice_rows(total, steps):
    k = 1
    while (total * k) % steps or (total * k // steps) % V7X_BF16_SUBLANES:
        k *= 2
    return total * k // steps, k


def _ffn_body(*refs, final_norm, mixer_out, n_casts):
    refs = list(refs)
    if mixer_out:
        oa_ref, ob_ref, wo_ref = refs[:3]
        refs = refs[3:]
    x_ref, g_ref, wg_ref, wu_ref, wd_ref, gf_ref = refs[:6]
    refs = refs[6:]
    o_ref = refs[n_casts]
    for src, dst in zip(refs[:n_casts], refs[n_casts + 1:]):
        dst[...] = src[...].astype(BF16)
    x = x_ref[...]
    if mixer_out:
        x = x + jnp.dot(oa_ref[...], wo_ref[:A_WIDTH, :], preferred_element_type=F32)
        x = x + jnp.dot(ob_ref[...], wo_ref[A_WIDTH:, :], preferred_element_type=F32)
    h = _rms(x, g_ref[...]).astype(BF16)
    acc = None
    for lo, hi in zip(FFN_CHUNKS[:-1], FFN_CHUNKS[1:]):
        gt = jnp.dot(h, wg_ref[:, lo:hi], preferred_element_type=F32)
        up = jnp.dot(h, wu_ref[:, lo:hi], preferred_element_type=F32)
        act = (gt * _sigmoid(gt) * up).astype(BF16)
        d = jnp.dot(act, wd_ref[lo:hi, :], preferred_element_type=F32)
        acc = d if acc is None else acc + d
    y = x + 0.5 * acc
    if final_norm:
        y = _rms(y, gf_ref[...])
    o_ref[...] = y


def _ffn(x, g, wb, gf, final_norm, mixer=None, casts=()):
    t, d = x.shape
    tm = min(FFN_ROWS, t)
    steps = t // tm
    row = pl.BlockSpec((tm, d), lambda i: (i, 0))
    args = [x, g, *wb, gf]
    in_specs = [row, _const_spec((1, d))] + [_const_spec(w.shape) for w in wb] + [_const_spec((1, d))]
    if mixer is not None:
        oa, ob, wo = mixer
        args = [oa, ob, wo] + args
        in_specs = [pl.BlockSpec((tm, oa.shape[1]), lambda i: (i, 0)), pl.BlockSpec((tm, ob.shape[1]), lambda i: (i, 0)),
                    _const_spec(wo.shape)] + in_specs
    out_shape = [jax.ShapeDtypeStruct((t, d), F32)]
    out_specs = [row]
    for w, lead in casts:
        lead = tuple(lead)
        nr, nc = w.shape[-2:]
        rows, k = _slice_rows(nr, steps)
        args.append(w)
        in_specs.append(pl.BlockSpec((None,) * len(lead) + (rows, nc), lambda i, k=k, lead=lead: lead + (i // k, 0)))
        out_shape.append(jax.ShapeDtypeStruct((nr, nc), BF16))
        out_specs.append(pl.BlockSpec((rows, nc), lambda i, k=k: (i // k, 0)))
    out = pl.pallas_call(
        functools.partial(_ffn_body, final_norm=final_norm, mixer_out=mixer is not None, n_casts=len(casts)),
        out_shape=tuple(out_shape),
        grid=(steps,),
        in_specs=in_specs,
        out_specs=tuple(out_specs),
        compiler_params=_params("arbitrary"),
        name="ffn_final" if final_norm else ("ffn_mix" if mixer is not None else "ffn"),
    )(*args)
    return out[0], tuple(out[1:])


def _ab_in_body(x_ref, g_ref, w_ref, qkv_ref, pb_ref):
    h = _rms(x_ref[...], g_ref[...]).astype(BF16)
    na = qkv_ref.shape[-1]
    qkv_ref[...] = jnp.dot(h, w_ref[:, :na], preferred_element_type=F32).astype(BF16)
    pb_ref[...] = jnp.dot(h, w_ref[:, na:], preferred_element_type=F32)


def _ab_in(x, g, w):
    t, d = x.shape
    n = w.shape[-1]
    na = 3 * A_WIDTH
    tm = min(PROJ_ROWS, t)
    return pl.pallas_call(
        _ab_in_body,
        out_shape=(jax.ShapeDtypeStruct((t, na), BF16), jax.ShapeDtypeStruct((t, n - na), F32)),
        grid=(t // tm,),
        in_specs=[pl.BlockSpec((tm, d), lambda i: (i, 0)), _const_spec((1, d)), _const_spec(w.shape)],
        out_specs=(pl.BlockSpec((tm, na), lambda i: (i, 0)), pl.BlockSpec((tm, n - na), lambda i: (i, 0))),
        compiler_params=_params("parallel"),
        name="ab_in",
    )(x, g, w)


def _attn_body(*refs, n_kblocks):
    q_ref = refs[0]
    k_refs = refs[1:1 + n_kblocks]
    v_refs = refs[1 + n_kblocks:1 + 2 * n_kblocks]
    trow_ref = refs[1 + 2 * n_kblocks]
    o_ref = refs[2 + 2 * n_kblocks]
    bias_ref = refs[3 + 2 * n_kblocks]
    qb = q_ref.shape[0]
    i = pl.program_id(1)
    n_prev = n_kblocks - 1

    @pl.when(i == 0)
    def _():
        w = bias_ref.shape[2]
        rr = lax.broadcasted_iota(jnp.int32, (qb, w), 0) // CHUNK
        cc = lax.broadcasted_iota(jnp.int32, (qb, w), 1) // CHUNK
        band = jnp.logical_and(cc >= rr, cc <= rr + LEFT_CHUNKS)
        for h in range(A_HEADS):
            t = jnp.broadcast_to(trow_ref[h], (qb, trow_ref.shape[2]))
            toe = pltpu.roll(t, 0, 1, stride=1, stride_axis=0)
            bias_ref[h] = jnp.where(band, toe[:, :w], NEG_INF)

    pens = [jnp.where(i >= n_prev - j, 0.0, NEG_INF).astype(F32) for j in range(n_prev)] + [None]
    lane = lax.broadcasted_iota(jnp.int32, (1, 2 * A_HEAD_DIM), 1)
    first = lane < A_HEAD_DIM
    scale = A_HEAD_DIM ** -0.5
    for pr in range(A_HEADS // 2):
        ls = slice(2 * A_HEAD_DIM * pr, 2 * A_HEAD_DIM * (pr + 1))
        q_pair = q_ref[:, ls]
        ks = [r[:, ls] for r in k_refs]
        vs = [r[:, ls] for r in v_refs]
        outs = []
        for hh in range(2):
            head = 2 * pr + hh
            sel = first if hh == 0 else jnp.logical_not(first)
            qm = jnp.where(sel, scale, 0.0).astype(BF16)
            qh = q_pair * qm
            s = []
            for j in range(n_kblocks):
                sj = lax.dot_general(qh, ks[j], (((1,), (1,)), ((), ())), preferred_element_type=F32)
                sj = sj + bias_ref[head, :, j * qb:(j + 1) * qb]
                if pens[j] is not None:
                    sj = sj + pens[j]
                s.append(sj)
            m = s[0].max(axis=-1, keepdims=True)
            for sj in s[1:]:
                m = jnp.maximum(m, sj.max(axis=-1, keepdims=True))
            l = None
            pv = None
            for j in range(n_kblocks):
                p = jnp.exp(s[j] - m)
                lj = p.sum(axis=-1, keepdims=True)
                pvj = jnp.dot(p.astype(BF16), vs[j], preferred_element_type=F32)
                l = lj if l is None else l + lj
                pv = pvj if pv is None else pv + pvj
            outs.append(pv / l)
        o_ref[:, ls] = jnp.where(first, outs[0], outs[1]).astype(BF16)


def _attn_bias_row(rel_bias, qb):
    nh = rel_bias.shape[0]
    w = ATT_PREV + qb
    period = w + qb
    near = 2 * REL_CLIP + 1
    assert ATT_PREV >= REL_CLIP and qb >= REL_CLIP and period % V7X_LANES == 0
    far_past = jnp.broadcast_to(rel_bias[:, near - 1:], (nh, ATT_PREV - REL_CLIP))
    far_future = jnp.broadcast_to(rel_bias[:, :1], (nh, qb - REL_CLIP))
    wrapped = jnp.broadcast_to(rel_bias[:, near - 1:], (nh, qb - 1))
    t = jnp.concatenate([far_past, rel_bias[:, ::-1], far_future, wrapped], axis=1).astype(F32)
    assert t.shape[1] == period
    return t.reshape(nh, 1, period)


def _attention(qkv, rel_bias):
    b, s, _ = qkv.shape
    qb = min(ATT_ROWS, s)
    assert ATT_PREV % qb == 0 and s % qb == 0
    n_prev = ATT_PREV // qb
    n_kblocks = n_prev + 1
    trow = _attn_bias_row(rel_bias, qb)

    def blk(col, back):
        return pl.BlockSpec((None, qb, A_WIDTH), lambda bi, i: (bi, jnp.maximum(i - back, 0), col))

    in_specs = ([blk(0, 0)] + [blk(1, n_prev - j) for j in range(n_kblocks)]
                + [blk(2, n_prev - j) for j in range(n_kblocks)] + [_const_spec(trow.shape)])
    return pl.pallas_call(
        functools.partial(_attn_body, n_kblocks=n_kblocks),
        out_shape=jax.ShapeDtypeStruct((b, s, A_WIDTH), BF16),
        grid=(b, s // qb),
        in_specs=in_specs,
        out_specs=pl.BlockSpec((None, qb, A_WIDTH), lambda bi, i: (bi, i, 0)),
        scratch_shapes=[pltpu.VMEM((A_HEADS, qb, ATT_PREV + qb), F32)],
        compiler_params=_params("parallel", "arbitrary"),
        name="attention",
    )(*([qkv] * (1 + 2 * n_kblocks)), trow)


def _split3(x):
    h1 = x.astype(BF16)
    r1 = x - h1.astype(F32)
    h2 = r1.astype(BF16)
    h3 = (r1 - h2.astype(F32)).astype(BF16)
    return h1, h2, h3


def _rwkv_body(*refs, has_vmix, c):
    pb_ref = refs[0]
    state_ref, carry_ref = refs[-2:]

    @pl.when(pl.program_id(0) == 0)
    def _():
        state_ref[...] = jnp.zeros_like(state_ref)
        carry_ref[...] = jnp.zeros_like(carry_ref)

    for ci in range(pb_ref.shape[1] // c):
        _rwkv_chunk(slice(ci * c, (ci + 1) * c), refs, has_vmix)


def _rwkv_chunk(ts, refs, has_vmix):
    if has_vmix:
        (pb_ref, vf_ref, mu_ref, w0_ref, wup_ref, a0_ref, aup_ref, gup_ref, kk_ref, ka_ref, rk_ref,
         lnw_ref, lnb_ref, v0_ref, vdn_ref, vup_ref, o_ref, state_ref, carry_ref) = refs
        vraw_ref = None
    else:
        (pb_ref, mu_ref, w0_ref, wup_ref, a0_ref, aup_ref, gup_ref, kk_ref, ka_ref, rk_ref,
         lnw_ref, lnb_ref, o_ref, vraw_ref, state_ref, carry_ref) = refs
    nb, _, ncol = pb_ref.shape
    c = ts.stop - ts.start
    n = B_HEAD_SIZE
    rows = nb * c
    p = pb_ref[:, ts, :].reshape(rows, ncol)
    row = lax.broadcasted_iota(jnp.int32, (rows, 1), 0)
    prev = pltpu.roll(p, 1, axis=0)
    for b in range(nb):
        prev = jnp.where(row == b * c, carry_ref[b, 0:1, :], prev)
    for b in range(nb):
        carry_ref[b, 0:1, :] = p[(b + 1) * c - 1:(b + 1) * c, :]
    ps = p + (prev - p) * mu_ref[...]
    r = ps[:, 0:B_WIDTH]
    k = ps[:, B_WIDTH:2 * B_WIDTH]
    v = ps[:, 2 * B_WIDTH:3 * B_WIDTH]
    wad = ps[:, 3 * B_WIDTH:3 * B_WIDTH + W_LORA + A_LORA]
    gd = ps[:, 3 * B_WIDTH + W_LORA + A_LORA:]
    if has_vmix:
        mix = _sigmoid(v0_ref[...] + _mm(_mm(v, vdn_ref[...]), vup_ref[...]))
        v = v + (vf_ref[:, ts, :].reshape(rows, B_WIDTH) - v) * mix
    else:
        vraw_ref[:, ts, :] = v.reshape(nb, c, B_WIDTH)
    wlog = -DECAY_SCALE * _sigmoid(w0_ref[...] + _mm(jnp.tanh(wad), wup_ref[...]))
    a = _sigmoid(a0_ref[...] + _mm(wad, aup_ref[...]))
    g = _mm(_sigmoid(gd), gup_ref[...])
    kkf = k * kk_ref[...]
    kka = kkf * a
    kmod = k * (1.0 + (a - 1.0) * ka_ref[...])
    rkr = r * kmod * rk_ref[...]

    ti = lax.broadcasted_iota(jnp.int32, (c, 3 * c), 0)
    tj = lax.broadcasted_iota(jnp.int32, (c, 3 * c), 1) % c
    tri3 = jnp.where(tj <= ti, 1.0, 0.0).astype(BF16)
    lcum, lend = [], []
    for b in range(nb):
        lb = jnp.dot(tri3, jnp.concatenate(_split3(wlog[b * c:(b + 1) * c]), axis=0),
                     preferred_element_type=F32)
        lcum.append(lb)
        lend.append(jnp.broadcast_to(lb[c - 1:c, :], lb.shape))
    lcum = jnp.concatenate(lcum, axis=0)
    lend = jnp.concatenate(lend, axis=0)
    e_neg = jnp.exp(-lcum)
    e_end = jnp.exp(lend - lcum)
    w_end = jnp.exp(lend)

    pair = 2 * n
    npair = B_HEADS // 2
    lo = lax.broadcasted_iota(jnp.int32, (1, 1, pair), 2) < n

    def slabs(q):
        return jnp.stack([q[b * c:(b + 1) * c, p * pair:(p + 1) * pair]
                          for b in range(nb) for p in range(npair)], axis=0)

    def unslabs(q):
        return jnp.concatenate(
            [jnp.concatenate([q[b * npair + p] for p in range(npair)], axis=-1) for b in range(nb)], axis=0)

    def head_sum(q):
        s_lo = jnp.sum(jnp.where(lo, q, 0.0), axis=-1, keepdims=True)
        s_hi = jnp.sum(jnp.where(lo, 0.0, q), axis=-1, keepdims=True)
        return jnp.where(lo, s_lo, s_hi)

    def bd(q):
        qb = q.astype(BF16)
        zero = jnp.zeros_like(qb)
        return jnp.concatenate([jnp.where(lo, qb, zero), jnp.where(lo, zero, qb)], axis=1)

    def bmm(lhs, rhs):
        return jnp.einsum("gij,gjv->giv", lhs.astype(BF16), rhs.astype(BF16), preferred_element_type=F32)

    kk_s = slabs(kkf)
    rs = lax.rsqrt(jnp.maximum(head_sum(kk_s * kk_s), 1e-24))
    x = jnp.concatenate([slabs(-kkf * jnp.exp(lcum - wlog)) * rs, slabs(r * jnp.exp(lcum))], axis=1)
    bt = slabs(kka * e_neg) * rs
    kt = slabs(kmod * e_neg)
    bk_end = jnp.concatenate([slabs(kka * e_end) * rs, slabs(kmod * e_end)], axis=1)
    vh = slabs(v)
    s0 = state_ref[...]
    xb = x.astype(BF16)
    ybd = jnp.concatenate([bd(bt), bd(kt)], axis=1).astype(BF16)
    gm = jnp.einsum("gik,gjk->gij", xb, ybd, preferred_element_type=F32)
    xs = jnp.einsum("gik,gvk->giv", xb, s0.astype(BF16), preferred_element_type=F32)
    ri = lax.broadcasted_iota(jnp.int32, (1, c, 2 * c), 1)
    ci = lax.broadcasted_iota(jnp.int32, (1, c, 2 * c), 2) % c
    strict = ci < ri
    incl = ci <= ri
    m_ab = jnp.where(strict, gm[:, :c, :2 * c], 0.0)
    m_ak = jnp.where(strict, gm[:, :c, 2 * c:], 0.0)
    m_rb = jnp.where(incl, gm[:, c:, :2 * c], 0.0)
    m_rk = jnp.where(incl, gm[:, c:, 2 * c:], 0.0)
    bdv = bd(vh)
    u = xs[:, :c] + bmm(m_ak, bdv)
    pw = m_ab.astype(BF16)
    span = 1
    while True:
        u = u + bmm(pw, bd(u))
        span *= 2
        if span >= c:
            break
        pw = bmm(pw, bd(pw)).astype(BF16)
    o = xs[:, c:] + bmm(jnp.concatenate([m_rb, m_rk], axis=2), jnp.concatenate([bd(u), bdv], axis=1))
    w_end_s = jnp.stack([w_end[b * c:b * c + 1, p * pair:(p + 1) * pair]
                         for b in range(nb) for p in range(npair)], axis=0)
    uv = jnp.concatenate([u, vh], axis=1)
    upd = jnp.einsum("gjv,gjk->gvk", uv.astype(BF16), bk_end.astype(BF16), preferred_element_type=F32)
    same_head = (lax.broadcasted_iota(jnp.int32, (1, pair, pair), 1) < n) == lo
    state_ref[...] = s0 * w_end_s + jnp.where(same_head, upd, 0.0)
    mean = head_sum(o) * (1.0 / n)
    d = o - mean
    var = head_sum(d * d) * (1.0 / n)
    o = d * lax.rsqrt(var + GN_EPS)
    bonus = head_sum(slabs(rkr)) * vh
    out = (unslabs(o) * lnw_ref[...] + lnb_ref[...] + unslabs(bonus)) * g
    o_ref[:, ts, :] = out.astype(BF16).reshape(nb, c, B_WIDTH)


def _rwkv(pb, v_first, prm, vmix):
    b, s, ncol = pb.shape
    c = min(RWKV_CHUNK, s)
    ts = min(RWKV_STEP_CHUNKS * c, s)
    has_vmix = vmix is not None
    tok = lambda w: pl.BlockSpec((b, ts, w), lambda i: (0, i, 0))
    vec = lambda a: a.reshape(1, -1).astype(F32)
    lw = W_LORA + A_LORA
    wup = jnp.zeros((lw, B_WIDTH), F32).at[:W_LORA].set(prm["w_up"]).astype(BF16)
    aup = jnp.zeros((lw, B_WIDTH), F32).at[W_LORA:].set(prm["a_up"]).astype(BF16)
    common = [vec(prm["mu"]), vec(prm["w0"]), wup, vec(prm["a0"]), aup, prm["g_up"].astype(BF16),
              vec(prm["k_k"]), vec(prm["k_a"]), vec(prm["r_k"]), vec(prm["ln_w"]), vec(prm["ln_b"])]
    if has_vmix:
        v0, v_down, v_up = vmix
        vdn = jnp.zeros((B_WIDTH, V7X_LANES), F32).at[:, :V_LORA].set(v_down).astype(BF16)
        vup = jnp.zeros((V7X_LANES, B_WIDTH), F32).at[:V_LORA].set(v_up).astype(BF16)
        args = [pb, v_first] + common + [vec(v0), vdn, vup]
        in_specs = [tok(ncol), tok(B_WIDTH)] + [_const_spec(a.shape) for a in args[2:]]
        out_shape = jax.ShapeDtypeStruct((b, s, B_WIDTH), BF16)
        out_specs = tok(B_WIDTH)
    else:
        args = [pb] + common
        in_specs = [tok(ncol)] + [_const_spec(a.shape) for a in args[1:]]
        out_shape = (jax.ShapeDtypeStruct((b, s, B_WIDTH), BF16), jax.ShapeDtypeStruct((b, s, B_WIDTH), F32))
        out_specs = (tok(B_WIDTH), tok(B_WIDTH))
    return pl.pallas_call(
        functools.partial(_rwkv_body, has_vmix=has_vmix, c=c),
        out_shape=out_shape,
        grid=(s // ts,),
        in_specs=in_specs,
        out_specs=out_specs,
        scratch_shapes=[pltpu.VMEM((b * B_HEADS // 2, 2 * B_HEAD_SIZE, 2 * B_HEAD_SIZE), F32),
                        pltpu.VMEM((b, V7X_SUBLANES, ncol), F32)],
        compiler_params=_params("arbitrary"),
        name="rwkv_mix" if has_vmix else "rwkv",
    )(*args)


def _gelu_tanh(x):
    return 0.5 * x * (1.0 + jnp.tanh(np.sqrt(2.0 / np.pi).astype(np.float32) * (x + 0.044715 * (x * x * x))))


def _rglru_tile(bi, x_ref, g_ref, win_ref, cw_ref, cb_ref, wax_ref, ba_ref, bx_ref, lam_ref, wout_ref,
                o_ref, xprev_ref, hprev_ref):
    tr = x_ref.shape[1]
    sub = V7X_SUBLANES
    nrow = tr // sub
    ntail = CONV_W - 1
    x = x_ref[bi]
    h = _rms(x, g_ref[...]).astype(BF16)
    i0 = lax.broadcasted_iota(jnp.int32, (tr, tr), 0)
    i1 = lax.broadcasted_iota(jnp.int32, (tr, tr), 1)
    perm = jnp.where(i1 == (i0 % sub) * nrow + i0 // sub, 1.0, 0.0).astype(BF16)
    unperm = jnp.where(i0 == (i1 % sub) * nrow + i1 // sub, 1.0, 0.0).astype(BF16)
    hp = jnp.dot(perm, h, preferred_element_type=F32).astype(BF16)
    gate = _gelu_tanh(jnp.dot(hp, win_ref[:, :D_RNN], preferred_element_type=F32))
    xb3 = jnp.dot(hp, win_ref[:, D_RNN:], preferred_element_type=F32).reshape(nrow, sub, D_RNN)
    seg = lax.broadcasted_iota(jnp.int32, (1, sub, 1), 1)
    prev3 = xprev_ref[bi]
    tails = [jnp.where(seg == 0, pltpu.roll(prev3[q:q + 1], 1, axis=1),
                       pltpu.roll(xb3[nrow - ntail + q:nrow - ntail + q + 1], 1, axis=1)) for q in range(ntail)]
    xprev_ref[bi] = xb3[nrow - ntail:]
    xc3 = xb3 * cw_ref[CONV_W - 1:CONV_W, :] + cb_ref[...]
    for sft in range(1, CONV_W):
        shifted = jnp.concatenate(tails[ntail - sft:] + [xb3[:nrow - sft]], axis=0)
        xc3 = xc3 + shifted * cw_ref[CONV_W - 1 - sft:CONV_W - sft, :]
    xc = xc3.reshape(tr, D_RNN)
    xcb = xc.astype(BF16)
    pw = 2 * C_BLOCK_W
    ra, rx = [], []
    for q in range(C_BLOCKS // 2):
        t = jnp.dot(xcb[:, q * pw:(q + 1) * pw], wax_ref[q], preferred_element_type=F32)
        ra.append(t[:, :pw])
        rx.append(t[:, pw:])
    rgate = _sigmoid(jnp.concatenate(ra, axis=-1) + ba_ref[...])
    igate = _sigmoid(jnp.concatenate(rx, axis=-1) + bx_ref[...])
    z = -lam_ref[...]
    softplus = jnp.maximum(z, 0.0) + jnp.log(1.0 + jnp.exp(-jnp.abs(z)))
    a = jnp.exp(-RG_C * rgate * softplus)
    mult = jnp.sqrt(jnp.maximum(1.0 - a * a, 0.0))
    bb = mult * igate * xc
    a3 = a.reshape(nrow, sub, D_RNN)
    b3 = bb.reshape(nrow, sub, D_RNN)
    acs, bcs = [a3[0]], [b3[0]]
    for p in range(1, nrow):
        bcs.append(a3[p] * bcs[-1] + b3[p])
        acs.append(a3[p] * acs[-1])
    ga, gb = acs[-1], bcs[-1]
    srow = lax.broadcasted_iota(jnp.int32, (sub, 1), 0)
    sft = 1
    while sft < sub:
        keep = srow >= sft
        a_sh = jnp.where(keep, pltpu.roll(ga, sft, axis=0), 1.0)
        b_sh = jnp.where(keep, pltpu.roll(gb, sft, axis=0), 0.0)
        gb = ga * b_sh + gb
        ga = ga * a_sh
        sft *= 2
    hprev = hprev_ref[bi, 0:1, :]
    hend = ga * hprev + gb
    hin = jnp.where(srow == 0, hprev, pltpu.roll(hend, 1, axis=0))
    hprev_ref[bi, 0:1, :] = hend[sub - 1:sub, :]
    hs = jnp.stack([acs[p] * hin + bcs[p] for p in range(nrow)], axis=0).reshape(tr, D_RNN)
    yp = (hs * gate).astype(BF16)
    y = jnp.dot(unperm, yp, preferred_element_type=F32).astype(BF16)
    o_ref[bi] = x + jnp.dot(y, wout_ref[...], preferred_element_type=F32)


def _rglru_body(x_ref, *rest):
    xprev_ref, hprev_ref = rest[-2:]

    @pl.when(pl.program_id(1) == 0)
    def _():
        xprev_ref[...] = jnp.zeros_like(xprev_ref)
        hprev_ref[...] = jnp.zeros_like(hprev_ref)

    for bi in range(x_ref.shape[0]):
        _rglru_tile(bi, x_ref, *rest)


def _rglru(x, g, w_in, conv_w, conv_b, wa, ba, wx, bx, lam, w_out):
    b, s, d = x.shape
    tr = min(RGLRU_ROWS, s)
    zero = jnp.zeros((C_BLOCKS // 2, C_BLOCK_W, C_BLOCK_W), wa.dtype)
    even = jnp.concatenate([wa[0::2], zero, wx[0::2], zero], axis=2)
    odd = jnp.concatenate([zero, wa[1::2], zero, wx[1::2]], axis=2)
    wax = jnp.concatenate([even, odd], axis=1).astype(BF16)
    vec = lambda a: a.reshape(1, -1).astype(F32)
    args = [x, vec(g), w_in, conv_w.astype(F32), vec(conv_b), wax, vec(ba), vec(bx), vec(lam), w_out]
    nb = min(RGLRU_BATCH, b)
    tok = pl.BlockSpec((nb, tr, d), lambda bi, i: (bi, i, 0))
    return pl.pallas_call(
        _rglru_body,
        out_shape=jax.ShapeDtypeStruct((b, s, d), F32),
        grid=(b // nb, s // tr),
        in_specs=[tok] + [_const_spec(a.shape) for a in args[1:]],
        out_specs=tok,
        scratch_shapes=[pltpu.VMEM((nb, CONV_W - 1, V7X_SUBLANES, D_RNN), F32),
                        pltpu.VMEM((nb, V7X_SUBLANES, D_RNN), F32)],
        compiler_params=_params("parallel", "arbitrary"),
        name="rglru",
    )(*args)


def kernel(x, norm_ffn, ffn_w_gate, ffn_w_up, ffn_w_down, norm_mix, ab_w_in, ab_w_out, att_rel_bias, rwkv_mu, rwkv_w0, rwkv_w_up, rwkv_a0, rwkv_a_up, rwkv_g_up, rwkv_k_k, rwkv_k_a, rwkv_r_k, rwkv_ln_w, rwkv_ln_b, rwkv_v0, rwkv_v_down, rwkv_v_up, c_w_in, c_conv_w, c_conv_b, c_wa, c_ba, c_wx, c_bx, c_lambda, c_w_out, norm_final):
    b, s, d = x.shape
    t = b * s
    depth = norm_mix.shape[0]
    ffn_w = (ffn_w_gate.astype(F32), ffn_w_up.astype(F32), ffn_w_down.astype(F32))
    ffn_calls = [(l, j) for l in range(depth) for j in range(2)]
    gfin = norm_final.reshape(1, d).astype(F32)
    wb = tuple(w[0, 0].astype(BF16) for w in ffn_w)

    def ffn(x, wb, lead, mixer=None, extra=()):
        idx = ffn_calls.index(lead)
        last = idx + 1 == len(ffn_calls)
        casts = [] if last else [(w, ffn_calls[idx + 1]) for w in ffn_w]
        x, cast = _ffn(x, norm_ffn[lead].reshape(1, d), wb, gfin, last, mixer, casts + list(extra))
        return x, cast[:len(casts)], cast[len(casts):]

    x = x.astype(F32).reshape(t, d)
    v_first = None
    for l in range(depth):
        mixer = None
        if l % 2 == 0:
            i = l // 2
            x, wb, (w_in, w_out) = ffn(x, wb, (l, 0), extra=[(ab_w_in.astype(F32), (i,)), (ab_w_out.astype(F32), (i,))])
            qkv, pb = _ab_in(x, norm_mix[l].reshape(1, d), w_in)
            oa = _attention(qkv.reshape(b, s, -1), att_rel_bias[i])
            prm = dict(mu=rwkv_mu[i], w0=rwkv_w0[i], w_up=rwkv_w_up[i], a0=rwkv_a0[i], a_up=rwkv_a_up[i],
                       g_up=rwkv_g_up[i], k_k=rwkv_k_k[i], k_a=rwkv_k_a[i], r_k=rwkv_r_k[i],
                       ln_w=rwkv_ln_w[i], ln_b=rwkv_ln_b[i])
            if i == 0:
                ob, v_first = _rwkv(pb.reshape(b, s, -1), None, prm, None)
            else:
                ob = _rwkv(pb.reshape(b, s, -1), v_first, prm,
                           (rwkv_v0[i - 1], rwkv_v_down[i - 1], rwkv_v_up[i - 1]))
            mixer = (oa.reshape(t, -1), ob.reshape(t, -1), w_out)
        else:
            j = l // 2
            x, wb, (cw_in, cw_out) = ffn(x, wb, (l, 0), extra=[(c_w_in.astype(F32), (j,)), (c_w_out.astype(F32), (j,))])
            x = _rglru(x.reshape(b, s, d), norm_mix[l], cw_in, c_conv_w[j], c_conv_b[j], c_wa[j], c_ba[j],
                       c_wx[j], c_bx[j], c_lambda[j], cw_out).reshape(t, d)
        x, wb, _ = ffn(x, wb, (l, 1), mixer)
    return x.reshape(b, s, d)
```

```python
import functools

import jax
import jax.numpy as jnp
import numpy as np
from jax import lax
from jax.experimental import pallas as pl
from jax.experimental.pallas import tpu as pltpu

F32 = jnp.float32
BF16 = jnp.bfloat16

D_MODEL = 1024
DEPTH = 4
CHUNK = 64
LEFT_CHUNKS = 8
A_HEADS = 8
A_HEAD_DIM = 64
A_WIDTH = A_HEADS * A_HEAD_DIM
REL_CLIP = 128
B_HEADS = 8
B_HEAD_SIZE = 64
B_WIDTH = B_HEADS * B_HEAD_SIZE
W_LORA = 64
A_LORA = 64
G_LORA = 128
V_LORA = 32
DECAY_SCALE = 0.606531
GN_EPS = 64e-5
B_COLS = 3 * B_WIDTH + W_LORA + A_LORA + G_LORA
D_RNN = 1280
C_BLOCKS = 10
C_BLOCK_W = D_RNN // C_BLOCKS
CONV_W = 4
RG_C = 8.0
D_FF = 2816
NORM_EPS = 1e-6
NEG_INF = -1e30

V7X_LANES = 128
V7X_SUBLANES = 8
V7X_BF16_SUBLANES = 16
V7X_MXU_DIM = 256
V7X_VMEM_BYTES = 64 * 1024 * 1024

FFN_ROWS = 1024
FFN_CHUNKS = (0, 768, 1536, 2304, 2816)
PROJ_ROWS = 1024
ATT_ROWS = 512
ATT_PREV = LEFT_CHUNKS * CHUNK
RWKV_CHUNK = 64
RWKV_STEP_CHUNKS = 4
RGLRU_ROWS = 256
RGLRU_BATCH = 4
VMEM_LIMIT = 60 * 1024 * 1024


def _params(*sem):
    return pltpu.CompilerParams(dimension_semantics=sem, vmem_limit_bytes=VMEM_LIMIT)


def _const_spec(shape):
    nd = len(shape)
    return pl.BlockSpec(shape, lambda *_: (0,) * nd, pipeline_mode=pl.Buffered(1))


def _mm(a, b):
    return jnp.dot(a.astype(BF16), b.astype(BF16), preferred_element_type=F32)


def _rms(x, g):
    ms = jnp.mean(x * x, axis=-1, keepdims=True)
    return x * lax.rsqrt(ms + NORM_EPS) * g


def _sigmoid(x):
    return 0.5 * jnp.tanh(0.5 * x) + 0.5


def _slice_rows(total, steps):
    k = 1
    while (total * k) % steps or (total * k // steps) % V7X_BF16_SUBLANES:
        k *= 2
    return total * k // steps, k


def _ffn_body(*refs, final_norm, mixer_out, n_casts):
    refs = list(refs)
    if mixer_out:
        oa_ref, ob_ref, wo_ref = refs[:3]
        refs = refs[3:]
    x_ref, g_ref, wg_ref, wu_ref, wd_ref, gf_ref = refs[:6]
    refs = refs[6:]
    o_ref = refs[n_casts]
    for src, dst in zip(refs[:n_casts], refs[n_casts + 1:]):
        dst[...] = src[...].astype(BF16)
    x = x_ref[...]
    if mixer_out:
        x = x + jnp.dot(oa_ref[...], wo_ref[:A_WIDTH, :], preferred_element_type=F32)
        x = x + jnp.dot(ob_ref[...], wo_ref[A_WIDTH:, :], preferred_element_type=F32)
    h = _rms(x, g_ref[...]).astype(BF16)
    acc = None
    for lo, hi in zip(FFN_CHUNKS[:-1], FFN_CHUNKS[1:]):
        gt = jnp.dot(h, wg_ref[:, lo:hi], preferred_element_type=F32)
        up = jnp.dot(h, wu_ref[:, lo:hi], preferred_element_type=F32)
        act = (gt * _sigmoid(gt) * up).astype(BF16)
        d = jnp.dot(act, wd_ref[lo:hi, :], preferred_element_type=F32)
        acc = d if acc is None else acc + d
    y = x + 0.5 * acc
    if final_norm:
        y = _rms(y, gf_ref[...])
    o_ref[...] = y


def _ffn(x, g, wb, gf, final_norm, mixer=None, casts=()):
    t, d = x.shape
    tm = min(FFN_ROWS, t)
    steps = t // tm
    row = pl.BlockSpec((tm, d), lambda i: (i, 0))
    args = [x, g, *wb, gf]
    in_specs = [row, _const_spec((1, d))] + [_const_spec(w.shape) for w in wb] + [_const_spec((1, d))]
    if mixer is not None:
        oa, ob, wo = mixer
        args = [oa, ob, wo] + args
        in_specs = [pl.BlockSpec((tm, oa.shape[1]), lambda i: (i, 0)), pl.BlockSpec((tm, ob.shape[1]), lambda i: (i, 0)),
                    _const_spec(wo.shape)] + in_specs
    out_shape = [jax.ShapeDtypeStruct((t, d), F32)]
    out_specs = [row]
    for w, lead in casts:
        lead = tuple(lead)
        nr, nc = w.shape[-2:]
        rows, k = _slice_rows(nr, steps)
        args.append(w)
        in_specs.append(pl.BlockSpec((None,) * len(lead) + (rows, nc), lambda i, k=k, lead=lead: lead + (i // k, 0)))
        out_shape.append(jax.ShapeDtypeStruct((nr, nc), BF16))
        out_specs.append(pl.BlockSpec((rows, nc), lambda i, k=k: (i // k, 0)))
    out = pl.pallas_call(
        functools.partial(_ffn_body, final_norm=final_norm, mixer_out=mixer is not None, n_casts=len(casts)),
        out_shape=tuple(out_shape),
        grid=(steps,),
        in_specs=in_specs,
        out_specs=tuple(out_specs),
        compiler_params=_params("arbitrary"),
        name="ffn_final" if final_norm else ("ffn_mix" if mixer is not None else "ffn"),
    )(*args)
    return out[0], tuple(out[1:])


def _ab_in_body(x_ref, g_ref, w_ref, qkv_ref, pb_ref):
    h = _rms(x_ref[...], g_ref[...]).astype(BF16)
    na = qkv_ref.shape[-1]
    qkv_ref[...] = jnp.dot(h, w_ref[:, :na], preferred_element_type=F32).astype(BF16)
    pb_ref[...] = jnp.dot(h, w_ref[:, na:], preferred_element_type=F32)


def _ab_in(x, g, w):
    t, d = x.shape
    n = w.shape[-1]
    na = 3 * A_WIDTH
    tm = min(PROJ_ROWS, t)
    return pl.pallas_call(
        _ab_in_body,
        out_shape=(jax.ShapeDtypeStruct((t, na), BF16), jax.ShapeDtypeStruct((t, n - na), F32)),
        grid=(t // tm,),
        in_specs=[pl.BlockSpec((tm, d), lambda i: (i, 0)), _const_spec((1, d)), _const_spec(w.shape)],
        out_specs=(pl.BlockSpec((tm, na), lambda i: (i, 0)), pl.BlockSpec((tm, n - na), lambda i: (i, 0))),
        compiler_params=_params("parallel"),
        name="ab_in",
    )(x, g, w)


def _attn_body(*refs, n_kblocks):
    q_ref = refs[0]
    k_refs = refs[1:1 + n_kblocks]
    v_refs = refs[1 + n_kblocks:1 + 2 * n_kblocks]
    trow_ref = refs[1 + 2 * n_kblocks]
    o_ref = refs[2 + 2 * n_kblocks]
    bias_ref = refs[3 + 2 * n_kblocks]
    qb = q_ref.shape[0]
    i = pl.program_id(1)
    n_prev = n_kblocks - 1

    @pl.when(i == 0)
    def _():
        w = bias_ref.shape[2]
        rr = lax.broadcasted_iota(jnp.int32, (qb, w), 0) // CHUNK
        cc = lax.broadcasted_iota(jnp.int32, (qb, w), 1) // CHUNK
        band = jnp.logical_and(cc >= rr, cc <= rr + LEFT_CHUNKS)
        for h in range(A_HEADS):
            t = jnp.broadcast_to(trow_ref[h], (qb, trow_ref.shape[2]))
            toe = pltpu.roll(t, 0, 1, stride=1, stride_axis=0)
            bias_ref[h] = jnp.where(band, toe[:, :w], NEG_INF)

    pens = [jnp.where(i >= n_prev - j, 0.0, NEG_INF).astype(F32) for j in range(n_prev)] + [None]
    lane = lax.broadcasted_iota(jnp.int32, (1, 2 * A_HEAD_DIM), 1)
    first = lane < A_HEAD_DIM
    scale = A_HEAD_DIM ** -0.5
    for pr in range(A_HEADS // 2):
        ls = slice(2 * A_HEAD_DIM * pr, 2 * A_HEAD_DIM * (pr + 1))
        q_pair = q_ref[:, ls]
        ks = [r[:, ls] for r in k_refs]
        vs = [r[:, ls] for r in v_refs]
        outs = []
        for hh in range(2):
            head = 2 * pr + hh
            sel = first if hh == 0 else jnp.logical_not(first)
            qm = jnp.where(sel, scale, 0.0).astype(BF16)
            qh = q_pair * qm
            s = []
            for j in range(n_kblocks):
                sj = lax.dot_general(qh, ks[j], (((1,), (1,)), ((), ())), preferred_element_type=F32)
                sj = sj + bias_ref[head, :, j * qb:(j + 1) * qb]
                if pens[j] is not None:
                    sj = sj + pens[j]
                s.append(sj)
            m = s[0].max(axis=-1, keepdims=True)
            for sj in s[1:]:
                m = jnp.maximum(m, sj.max(axis=-1, keepdims=True))
            l = None
            pv = None
            for j in range(n_kblocks):
                p = jnp.exp(s[j] - m)
                lj = p.sum(axis=-1, keepdims=True)
                pvj = jnp.dot(p.astype(BF16), vs[j], preferred_element_type=F32)
                l = lj if l is None else l + lj
                pv = pvj if pv is None else pv + pvj
            outs.append(pv / l)
        o_ref[:, ls] = jnp.where(first, outs[0], outs[1]).astype(BF16)


def _attn_bias_row(rel_bias, qb):
    nh = rel_bias.shape[0]
    w = ATT_PREV + qb
    period = w + qb
    near = 2 * REL_CLIP + 1
    assert ATT_PREV >= REL_CLIP and qb >= REL_CLIP and period % V7X_LANES == 0
    far_past = jnp.broadcast_to(rel_bias[:, near - 1:], (nh, ATT_PREV - REL_CLIP))
    far_future = jnp.broadcast_to(rel_bias[:, :1], (nh, qb - REL_CLIP))
    wrapped = jnp.broadcast_to(rel_bias[:, near - 1:], (nh, qb - 1))
    t = jnp.concatenate([far_past, rel_bias[:, ::-1], far_future, wrapped], axis=1).astype(F32)
    assert t.shape[1] == period
    return t.reshape(nh, 1, period)


def _attention(qkv, rel_bias):
    b, s, _ = qkv.shape
    qb = min(ATT_ROWS, s)
    assert ATT_PREV % qb == 0 and s % qb == 0
    n_prev = ATT_PREV // qb
    n_kblocks = n_prev + 1
    trow = _attn_bias_row(rel_bias, qb)

    def blk(col, back):
        return pl.BlockSpec((None, qb, A_WIDTH), lambda bi, i: (bi, jnp.maximum(i - back, 0), col))

    in_specs = ([blk(0, 0)] + [blk(1, n_prev - j) for j in range(n_kblocks)]
                + [blk(2, n_prev - j) for j in range(n_kblocks)] + [_const_spec(trow.shape)])
    return pl.pallas_call(
        functools.partial(_attn_body, n_kblocks=n_kblocks),
        out_shape=jax.ShapeDtypeStruct((b, s, A_WIDTH), BF16),
        grid=(b, s // qb),
        in_specs=in_specs,
        out_specs=pl.BlockSpec((None, qb, A_WIDTH), lambda bi, i: (bi, i, 0)),
        scratch_shapes=[pltpu.VMEM((A_HEADS, qb, ATT_PREV + qb), F32)],
        compiler_params=_params("parallel", "arbitrary"),
        name="attention",
    )(*([qkv] * (1 + 2 * n_kblocks)), trow)


def _split3(x):
    h1 = x.astype(BF16)
    r1 = x - h1.astype(F32)
    h2 = r1.astype(BF16)
    h3 = (r1 - h2.astype(F32)).astype(BF16)
    return h1, h2, h3


def _rwkv_body(*refs, has_vmix, c):
    pb_ref = refs[0]
    state_ref, carry_ref = refs[-2:]

    @pl.when(pl.program_id(0) == 0)
    def _():
        state_ref[...] = jnp.zeros_like(state_ref)
        carry_ref[...] = jnp.zeros_like(carry_ref)

    for ci in range(pb_ref.shape[1] // c):
        _rwkv_chunk(slice(ci * c, (ci + 1) * c), refs, has_vmix)


def _rwkv_chunk(ts, refs, has_vmix):
    if has_vmix:
        (pb_ref, vf_ref, mu_ref, w0_ref, wup_ref, a0_ref, aup_ref, gup_ref, kk_ref, ka_ref, rk_ref,
         lnw_ref, lnb_ref, v0_ref, vdn_ref, vup_ref, o_ref, state_ref, carry_ref) = refs
        vraw_ref = None
    else:
        (pb_ref, mu_ref, w0_ref, wup_ref, a0_ref, aup_ref, gup_ref, kk_ref, ka_ref, rk_ref,
         lnw_ref, lnb_ref, o_ref, vraw_ref, state_ref, carry_ref) = refs
    nb, _, ncol = pb_ref.shape
    c = ts.stop - ts.start
    n = B_HEAD_SIZE
    rows = nb * c
    p = pb_ref[:, ts, :].reshape(rows, ncol)
    row = lax.broadcasted_iota(jnp.int32, (rows, 1), 0)
    prev = pltpu.roll(p, 1, axis=0)
    for b in range(nb):
        prev = jnp.where(row == b * c, carry_ref[b, 0:1, :], prev)
    for b in range(nb):
        carry_ref[b, 0:1, :] = p[(b + 1) * c - 1:(b + 1) * c, :]
    ps = p + (prev - p) * mu_ref[...]
    r = ps[:, 0:B_WIDTH]
    k = ps[:, B_WIDTH:2 * B_WIDTH]
    v = ps[:, 2 * B_WIDTH:3 * B_WIDTH]
    wad = ps[:, 3 * B_WIDTH:3 * B_WIDTH + W_LORA + A_LORA]
    gd = ps[:, 3 * B_WIDTH + W_LORA + A_LORA:]
    if has_vmix:
        mix = _sigmoid(v0_ref[...] + _mm(_mm(v, vdn_ref[...]), vup_ref[...]))
        v = v + (vf_ref[:, ts, :].reshape(rows, B_WIDTH) - v) * mix
    else:
        vraw_ref[:, ts, :] = v.reshape(nb, c, B_WIDTH)
    wlog = -DECAY_SCALE * _sigmoid(w0_ref[...] + _mm(jnp.tanh(wad), wup_ref[...]))
    a = _sigmoid(a0_ref[...] + _mm(wad, aup_ref[...]))
    g = _mm(_sigmoid(gd), gup_ref[...])
    kkf = k * kk_ref[...]
    kka = kkf * a
    kmod = k * (1.0 + (a - 1.0) * ka_ref[...])
    rkr = r * kmod * rk_ref[...]

    ti = lax.broadcasted_iota(jnp.int32, (c, 3 * c), 0)
    tj = lax.broadcasted_iota(jnp.int32, (c, 3 * c), 1) % c
    tri3 = jnp.where(tj <= ti, 1.0, 0.0).astype(BF16)
    lcum, lend = [], []
    for b in range(nb):
        lb = jnp.dot(tri3, jnp.concatenate(_split3(wlog[b * c:(b + 1) * c]), axis=0),
                     preferred_element_type=F32)
        lcum.append(lb)
        lend.append(jnp.broadcast_to(lb[c - 1:c, :], lb.shape))
    lcum = jnp.concatenate(lcum, axis=0)
    lend = jnp.concatenate(lend, axis=0)
    e_neg = jnp.exp(-lcum)
    e_end = jnp.exp(lend - lcum)
    w_end = jnp.exp(lend)

    pair = 2 * n
    npair = B_HEADS // 2
    lo = lax.broadcasted_iota(jnp.int32, (1, 1, pair), 2) < n

    def slabs(q):
        return jnp.stack([q[b * c:(b + 1) * c, p * pair:(p + 1) * pair]
                          for b in range(nb) for p in range(npair)], axis=0)

    def unslabs(q):
        return jnp.concatenate(
            [jnp.concatenate([q[b * npair + p] for p in range(npair)], axis=-1) for b in range(nb)], axis=0)

    def head_sum(q):
        s_lo = jnp.sum(jnp.where(lo, q, 0.0), axis=-1, keepdims=True)
        s_hi = jnp.sum(jnp.where(lo, 0.0, q), axis=-1, keepdims=True)
        return jnp.where(lo, s_lo, s_hi)

    def bd(q):
        qb = q.astype(BF16)
        zero = jnp.zeros_like(qb)
        return jnp.concatenate([jnp.where(lo, qb, zero), jnp.where(lo, zero, qb)], axis=1)

    def bmm(lhs, rhs):
        return jnp.einsum("gij,gjv->giv", lhs.astype(BF16), rhs.astype(BF16), preferred_element_type=F32)

    kk_s = slabs(kkf)
    rs = lax.rsqrt(jnp.maximum(head_sum(kk_s * kk_s), 1e-24))
    x = jnp.concatenate([slabs(-kkf * jnp.exp(lcum - wlog)) * rs, slabs(r * jnp.exp(lcum))], axis=1)
    bt = slabs(kka * e_neg) * rs
    kt = slabs(kmod * e_neg)
    bk_end = jnp.concatenate([slabs(kka * e_end) * rs, slabs(kmod * e_end)], axis=1)
    vh = slabs(v)
    s0 = state_ref[...]
    xb = x.astype(BF16)
    ybd = jnp.concatenate([bd(bt), bd(kt)], axis=1).astype(BF16)
    gm = jnp.einsum("gik,gjk->gij", xb, ybd, preferred_element_type=F32)
    xs = jnp.einsum("gik,gvk->giv", xb, s0.astype(BF16), preferred_element_type=F32)
    ri = lax.broadcasted_iota(jnp.int32, (1, c, 2 * c), 1)
    ci = lax.broadcasted_iota(jnp.int32, (1, c, 2 * c), 2) % c
    strict = ci < ri
    incl = ci <= ri
    m_ab = jnp.where(strict, gm[:, :c, :2 * c], 0.0)
    m_ak = jnp.where(strict, gm[:, :c, 2 * c:], 0.0)
    m_rb = jnp.where(incl, gm[:, c:, :2 * c], 0.0)
    m_rk = jnp.where(incl, gm[:, c:, 2 * c:], 0.0)
    bdv = bd(vh)
    u = xs[:, :c] + bmm(m_ak, bdv)
    pw = m_ab.astype(BF16)
    span = 1
    while True:
        u = u + bmm(pw, bd(u))
        span *= 2
        if span >= c:
            break
        pw = bmm(pw, bd(pw)).astype(BF16)
    o = xs[:, c:] + bmm(jnp.concatenate([m_rb, m_rk], axis=2), jnp.concatenate([bd(u), bdv], axis=1))
    w_end_s = jnp.stack([w_end[b * c:b * c + 1, p * pair:(p + 1) * pair]
                         for b in range(nb) for p in range(npair)], axis=0)
    uv = jnp.concatenate([u, vh], axis=1)
    upd = jnp.einsum("gjv,gjk->gvk", uv.astype(BF16), bk_end.astype(BF16), preferred_element_type=F32)
    same_head = (lax.broadcasted_iota(jnp.int32, (1, pair, pair), 1) < n) == lo
    state_ref[...] = s0 * w_end_s + jnp.where(same_head, upd, 0.0)
    mean = head_sum(o) * (1.0 / n)
    d = o - mean
    var = head_sum(d * d) * (1.0 / n)
    o = d * lax.rsqrt(var + GN_EPS)
    bonus = head_sum(slabs(rkr)) * vh
    out = (unslabs(o) * lnw_ref[...] + lnb_ref[...] + unslabs(bonus)) * g
    o_ref[:, ts, :] = out.astype(BF16).reshape(nb, c, B_WIDTH)


def _rwkv(pb, v_first, prm, vmix):
    b, s, ncol = pb.shape
    c = min(RWKV_CHUNK, s)
    ts = min(RWKV_STEP_CHUNKS * c, s)
    has_vmix = vmix is not None
    tok = lambda w: pl.BlockSpec((b, ts, w), lambda i: (0, i, 0))
    vec = lambda a: a.reshape(1, -1).astype(F32)
    lw = W_LORA + A_LORA
    wup = jnp.zeros((lw, B_WIDTH), F32).at[:W_LORA].set(prm["w_up"]).astype(BF16)
    aup = jnp.zeros((lw, B_WIDTH), F32).at[W_LORA:].set(prm["a_up"]).astype(BF16)
    common = [vec(prm["mu"]), vec(prm["w0"]), wup, vec(prm["a0"]), aup, prm["g_up"].astype(BF16),
              vec(prm["k_k"]), vec(prm["k_a"]), vec(prm["r_k"]), vec(prm["ln_w"]), vec(prm["ln_b"])]
    if has_vmix:
        v0, v_down, v_up = vmix
        vdn = jnp.zeros((B_WIDTH, V7X_LANES), F32).at[:, :V_LORA].set(v_down).astype(BF16)
        vup = jnp.zeros((V7X_LANES, B_WIDTH), F32).at[:V_LORA].set(v_up).astype(BF16)
        args = [pb, v_first] + common + [vec(v0), vdn, vup]
        in_specs = [tok(ncol), tok(B_WIDTH)] + [_const_spec(a.shape) for a in args[2:]]
        out_shape = jax.ShapeDtypeStruct((b, s, B_WIDTH), BF16)
        out_specs = tok(B_WIDTH)
    else:
        args = [pb] + common
        in_specs = [tok(ncol)] + [_const_spec(a.shape) for a in args[1:]]
        out_shape = (jax.ShapeDtypeStruct((b, s, B_WIDTH), BF16), jax.ShapeDtypeStruct((b, s, B_WIDTH), F32))
        out_specs = (tok(B_WIDTH), tok(B_WIDTH))
    return pl.pallas_call(
        functools.partial(_rwkv_body, has_vmix=has_vmix, c=c),
        out_shape=out_shape,
        grid=(s // ts,),
        in_specs=in_specs,
        out_specs=out_specs,
        scratch_shapes=[pltpu.VMEM((b * B_HEADS // 2, 2 * B_HEAD_SIZE, 2 * B_HEAD_SIZE), F32),
                        pltpu.VMEM((b, V7X_SUBLANES, ncol), F32)],
        compiler_params=_params("arbitrary"),
        name="rwkv_mix" if has_vmix else "rwkv",
    )(*args)


def _gelu_tanh(x):
    return 0.5 * x * (1.0 + jnp.tanh(np.sqrt(2.0 / np.pi).astype(np.float32) * (x + 0.044715 * (x * x * x))))


def _rglru_tile(bi, x_ref, g_ref, win_ref, cw_ref, cb_ref, wax_ref, ba_ref, bx_ref, lam_ref, wout_ref,
                o_ref, xprev_ref, hprev_ref):
    tr = x_ref.shape[1]
    sub = V7X_SUBLANES
    nrow = tr // sub
    ntail = CONV_W - 1
    x = x_ref[bi]
    h = _rms(x, g_ref[...]).astype(BF16)
    i0 = lax.broadcasted_iota(jnp.int32, (tr, tr), 0)
    i1 = lax.broadcasted_iota(jnp.int32, (tr, tr), 1)
    perm = jnp.where(i1 == (i0 % sub) * nrow + i0 // sub, 1.0, 0.0).astype(BF16)
    unperm = jnp.where(i0 == (i1 % sub) * nrow + i1 // sub, 1.0, 0.0).astype(BF16)
    hp = jnp.dot(perm, h, preferred_element_type=F32).astype(BF16)
    gate = _gelu_tanh(jnp.dot(hp, win_ref[:, :D_RNN], preferred_element_type=F32))
    xb3 = jnp.dot(hp, win_ref[:, D_RNN:], preferred_element_type=F32).reshape(nrow, sub, D_RNN)
    seg = lax.broadcasted_iota(jnp.int32, (1, sub, 1), 1)
    prev3 = xprev_ref[bi]
    tails = [jnp.where(seg == 0, pltpu.roll(prev3[q:q + 1], 1, axis=1),
                       pltpu.roll(xb3[nrow - ntail + q:nrow - ntail + q + 1], 1, axis=1)) for q in range(ntail)]
    xprev_ref[bi] = xb3[nrow - ntail:]
    xc3 = xb3 * cw_ref[CONV_W - 1:CONV_W, :] + cb_ref[...]
    for sft in range(1, CONV_W):
        shifted = jnp.concatenate(tails[ntail - sft:] + [xb3[:nrow - sft]], axis=0)
        xc3 = xc3 + shifted * cw_ref[CONV_W - 1 - sft:CONV_W - sft, :]
    xc = xc3.reshape(tr, D_RNN)
    xcb = xc.astype(BF16)
    pw = 2 * C_BLOCK_W
    ra, rx = [], []
    for q in range(C_BLOCKS // 2):
        t = jnp.dot(xcb[:, q * pw:(q + 1) * pw], wax_ref[q], preferred_element_type=F32)
        ra.append(t[:, :pw])
        rx.append(t[:, pw:])
    rgate = _sigmoid(jnp.concatenate(ra, axis=-1) + ba_ref[...])
    igate = _sigmoid(jnp.concatenate(rx, axis=-1) + bx_ref[...])
    z = -lam_ref[...]
    softplus = jnp.maximum(z, 0.0) + jnp.log(1.0 + jnp.exp(-jnp.abs(z)))
    a = jnp.exp(-RG_C * rgate * softplus)
    mult = jnp.sqrt(jnp.maximum(1.0 - a * a, 0.0))
    bb = mult * igate * xc
    a3 = a.reshape(nrow, sub, D_RNN)
    b3 = bb.reshape(nrow, sub, D_RNN)
    acs, bcs = [a3[0]], [b3[0]]
    for p in range(1, nrow):
        bcs.append(a3[p] * bcs[-1] + b3[p])
        acs.append(a3[p] * acs[-1])
    ga, gb = acs[-1], bcs[-1]
    srow = lax.broadcasted_iota(jnp.int32, (sub, 1), 0)
    sft = 1
    while sft < sub:
        keep = srow >= sft
        a_sh = jnp.where(keep, pltpu.roll(ga, sft, axis=0), 1.0)
        b_sh = jnp.where(keep, pltpu.roll(gb, sft, axis=0), 0.0)
        gb = ga * b_sh + gb
        ga = ga * a_sh
        sft *= 2
    hprev = hprev_ref[bi, 0:1, :]
    hend = ga * hprev + gb
    hin = jnp.where(srow == 0, hprev, pltpu.roll(hend, 1, axis=0))
    hprev_ref[bi, 0:1, :] = hend[sub - 1:sub, :]
    hs = jnp.stack([acs[p] * hin + bcs[p] for p in range(nrow)], axis=0).reshape(tr, D_RNN)
    yp = (hs * gate).astype(BF16)
    y = jnp.dot(unperm, yp, preferred_element_type=F32).astype(BF16)
    o_ref[bi] = x + jnp.dot(y, wout_ref[...], preferred_element_type=F32)


def _rglru_body(x_ref, *rest):
    xprev_ref, hprev_ref = rest[-2:]

    @pl.when(pl.program_id(1) == 0)
    def _():
        xprev_ref[...] = jnp.zeros_like(xprev_ref)
        hprev_ref[...] = jnp.zeros_like(hprev_ref)

    for bi in range(x_ref.shape[0]):
        _rglru_tile(bi, x_ref, *rest)


def _rglru(x, g, w_in, conv_w, conv_b, wa, ba, wx, bx, lam, w_out):
    b, s, d = x.shape
    tr = min(RGLRU_ROWS, s)
    zero = jnp.zeros((C_BLOCKS // 2, C_BLOCK_W, C_BLOCK_W), wa.dtype)
    even = jnp.concatenate([wa[0::2], zero, wx[0::2], zero], axis=2)
    odd = jnp.concatenate([zero, wa[1::2], zero, wx[1::2]], axis=2)
    wax = jnp.concatenate([even, odd], axis=1).astype(BF16)
    vec = lambda a: a.reshape(1, -1).astype(F32)
    args = [x, vec(g), w_in, conv_w.astype(F32), vec(conv_b), wax, vec(ba), vec(bx), vec(lam), w_out]
    nb = min(RGLRU_BATCH, b)
    tok = pl.BlockSpec((nb, tr, d), lambda bi, i: (bi, i, 0))
    return pl.pallas_call(
        _rglru_body,
        out_shape=jax.ShapeDtypeStruct((b, s, d), F32),
        grid=(b // nb, s // tr),
        in_specs=[tok] + [_const_spec(a.shape) for a in args[1:]],
        out_specs=tok,
        scratch_shapes=[pltpu.VMEM((nb, CONV_W - 1, V7X_SUBLANES, D_RNN), F32),
                        pltpu.VMEM((nb, V7X_SUBLANES, D_RNN), F32)],
        compiler_params=_params("parallel", "arbitrary"),
        name="rglru",
    )(*args)


def kernel(x, norm_ffn, ffn_w_gate, ffn_w_up, ffn_w_down, norm_mix, ab_w_in, ab_w_out, att_rel_bias, rwkv_mu, rwkv_w0, rwkv_w_up, rwkv_a0, rwkv_a_up, rwkv_g_up, rwkv_k_k, rwkv_k_a, rwkv_r_k, rwkv_ln_w, rwkv_ln_b, rwkv_v0, rwkv_v_down, rwkv_v_up, c_w_in, c_conv_w, c_conv_b, c_wa, c_ba, c_wx, c_bx, c_lambda, c_w_out, norm_final):
    b, s, d = x.shape
    t = b * s
    depth = norm_mix.shape[0]
    ffn_w = (ffn_w_gate.astype(F32), ffn_w_up.astype(F32), ffn_w_down.astype(F32))
    ffn_calls = [(l, j) for l in range(depth) for j in range(2)]
    gfin = norm_final.reshape(1, d).astype(F32)
    wb = tuple(w[0, 0].astype(BF16) for w in ffn_w)

    def ffn(x, wb, lead, mixer=None, extra=()):
        idx = ffn_calls.index(lead)
        last = idx + 1 == len(ffn_calls)
        casts = [] if last else [(w, ffn_calls[idx + 1]) for w in ffn_w]
        x, cast = _ffn(x, norm_ffn[lead].reshape(1, d), wb, gfin, last, mixer, casts + list(extra))
        return x, cast[:len(casts)], cast[len(casts):]

    x = x.astype(F32).reshape(t, d)
    v_first = None
    for l in range(depth):
        mixer = None
        if l % 2 == 0:
            i = l // 2
            x, wb, (w_in, w_out) = ffn(x, wb, (l, 0), extra=[(ab_w_in.astype(F32), (i,)), (ab_w_out.astype(F32), (i,))])
            qkv, pb = _ab_in(x, norm_mix[l].reshape(1, d), w_in)
            oa = _attention(qkv.reshape(b, s, -1), att_rel_bias[i])
            prm = dict(mu=rwkv_mu[i], w0=rwkv_w0[i], w_up=rwkv_w_up[i], a0=rwkv_a0[i], a_up=rwkv_a_up[i],
                       g_up=rwkv_g_up[i], k_k=rwkv_k_k[i], k_a=rwkv_k_a[i], r_k=rwkv_r_k[i],
                       ln_w=rwkv_ln_w[i], ln_b=rwkv_ln_b[i])
            if i == 0:
                ob, v_first = _rwkv(pb.reshape(b, s, -1), None, prm, None)
            else:
                ob = _rwkv(pb.reshape(b, s, -1), v_first, prm,
                           (rwkv_v0[i - 1], rwkv_v_down[i - 1], rwkv_v_up[i - 1]))
            mixer = (oa.reshape(t, -1), ob.reshape(t, -1), w_out)
        else:
            j = l // 2
            x, wb, (cw_in, cw_out) = ffn(x, wb, (l, 0), extra=[(c_w_in.astype(F32), (j,)), (c_w_out.astype(F32), (j,))])
            x = _rglru(x.reshape(b, s, d), norm_mix[l], cw_in, c_conv_w[j], c_conv_b[j], c_wa[j], c_ba[j],
                       c_wx[j], c_bx[j], c_lambda[j], cw_out).reshape(t, d)
        x, wb, _ = ffn(x, wb, (l, 1), mixer)
    return x.reshape(b, s, d)
```

```python
import functools

import jax
import jax.numpy as jnp
import numpy as np
from jax import lax
from jax.experimental import pallas as pl
from jax.experimental.pallas import tpu as pltpu

F32 = jnp.float32
BF16 = jnp.bfloat16

D_MODEL = 1024
DEPTH = 4
CHUNK = 64
LEFT_CHUNKS = 8
A_HEADS = 8
A_HEAD_DIM = 64
A_WIDTH = A_HEADS * A_HEAD_DIM
REL_CLIP = 128
B_HEADS = 8
B_HEAD_SIZE = 64
B_WIDTH = B_HEADS * B_HEAD_SIZE
W_LORA = 64
A_LORA = 64
G_LORA = 128
V_LORA = 32
DECAY_SCALE = 0.606531
GN_EPS = 64e-5
B_COLS = 3 * B_WIDTH + W_LORA + A_LORA + G_LORA
D_RNN = 1280
C_BLOCKS = 10
C_BLOCK_W = D_RNN // C_BLOCKS
CONV_W = 4
RG_C = 8.0
D_FF = 2816
NORM_EPS = 1e-6
NEG_INF = -1e30

V7X_LANES = 128
V7X_SUBLANES = 8
V7X_BF16_SUBLANES = 16
V7X_MXU_DIM = 256
V7X_VMEM_BYTES = 64 * 1024 * 1024

FFN_ROWS = 1024
FFN_CHUNKS = (0, 768, 1536, 2304, 2816)
PROJ_ROWS = 1024
ATT_ROWS = 512
ATT_PREV = LEFT_CHUNKS * CHUNK
RWKV_CHUNK = 64
RWKV_STEP_CHUNKS = 4
RGLRU_ROWS = 256
RGLRU_BATCH = 4
VMEM_LIMIT = 60 * 1024 * 1024


def _params(*sem):
    return pltpu.CompilerParams(dimension_semantics=sem, vmem_limit_bytes=VMEM_LIMIT)


def _const_spec(shape):
    nd = len(shape)
    return pl.BlockSpec(shape, lambda *_: (0,) * nd, pipeline_mode=pl.Buffered(1))


def _mm(a, b):
    return jnp.dot(a.astype(BF16), b.astype(BF16), preferred_element_type=F32)


def _rms(x, g):
    ms = jnp.mean(x * x, axis=-1, keepdims=True)
    return x * lax.rsqrt(ms + NORM_EPS) * g


def _sigmoid(x):
    return 0.5 * jnp.tanh(0.5 * x) + 0.5


def _slice_rows(total, steps):
    k = 1
    while (total * k) % steps or (total * k // steps) % V7X_BF16_SUBLANES:
        k *= 2
    return total * k // steps, k


def _ffn_body(*refs, final_norm, mixer_out, n_casts):
    refs = list(refs)
    if mixer_out:
        oa_ref, ob_ref, wo_ref = refs[:3]
        refs = refs[3:]
    x_ref, g_ref, wg_ref, wu_ref, wd_ref, gf_ref = refs[:6]
    refs = refs[6:]
    o_ref = refs[n_casts]
    for src, dst in zip(refs[:n_casts], refs[n_casts + 1:]):
        dst[...] = src[...].astype(BF16)
    x = x_ref[...]
    if mixer_out:
        x = x + jnp.dot(oa_ref[...], wo_ref[:A_WIDTH, :], preferred_element_type=F32)
        x = x + jnp.dot(ob_ref[...], wo_ref[A_WIDTH:, :], preferred_element_type=F32)
    h = _rms(x, g_ref[...]).astype(BF16)
    acc = None
    for lo, hi in zip(FFN_CHUNKS[:-1], FFN_CHUNKS[1:]):
        gt = jnp.dot(h, wg_ref[:, lo:hi], preferred_element_type=F32)
        up = jnp.dot(h, wu_ref[:, lo:hi], preferred_element_type=F32)
        act = (gt * _sigmoid(gt) * up).astype(BF16)
        d = jnp.dot(act, wd_ref[lo:hi, :], preferred_element_type=F32)
        acc = d if acc is None else acc + d
    y = x + 0.5 * acc
    if final_norm:
        y = _rms(y, gf_ref[...])
    o_ref[...] = y


def _ffn(x, g, wb, gf, final_norm, mixer=None, casts=()):
    t, d = x.shape
    tm = min(FFN_ROWS, t)
    steps = t // tm
    row = pl.BlockSpec((tm, d), lambda i: (i, 0))
    args = [x, g, *wb, gf]
    in_specs = [row, _const_spec((1, d))] + [_const_spec(w.shape) for w in wb] + [_const_spec((1, d))]
    if mixer is not None:
        oa, ob, wo = mixer
        args = [oa, ob, wo] + args
        in_specs = [pl.BlockSpec((tm, oa.shape[1]), lambda i: (i, 0)), pl.BlockSpec((tm, ob.shape[1]), lambda i: (i, 0)),
                    _const_spec(wo.shape)] + in_specs
    out_shape = [jax.ShapeDtypeStruct((t, d), F32)]
    out_specs = [row]
    for w, lead in casts:
        lead = tuple(lead)
        nr, nc = w.shape[-2:]
        rows, k = _slice_rows(nr, steps)
        args.append(w)
        in_specs.append(pl.BlockSpec((None,) * len(lead) + (rows, nc), lambda i, k=k, lead=lead: lead + (i // k, 0)))
        out_shape.append(jax.ShapeDtypeStruct((nr, nc), BF16))
        out_specs.append(pl.BlockSpec((rows, nc), lambda i, k=k: (i // k, 0)))
    out = pl.pallas_call(
        functools.partial(_ffn_body, final_norm=final_norm, mixer_out=mixer is not None, n_casts=len(casts)),
        out_shape=tuple(out_shape),
        grid=(steps,),
        in_specs=in_specs,
        out_specs=tuple(out_specs),
        compiler_params=_params("arbitrary"),
        name="ffn_final" if final_norm else ("ffn_mix" if mixer is not None else "ffn"),
    )(*args)
    return out[0], tuple(out[1:])


def _ab_in_body(x_ref, g_ref, w_ref, qkv_ref, pb_ref):
    h = _rms(x_ref[...], g_ref[...]).astype(BF16)
    na = qkv_ref.shape[-1]
    qkv_ref[...] = jnp.dot(h, w_ref[:, :na], preferred_element_type=F32).astype(BF16)
    pb_ref[...] = jnp.dot(h, w_ref[:, na:], preferred_element_type=F32)


def _ab_in(x, g, w):
    t, d = x.shape
    n = w.shape[-1]
    na = 3 * A_WIDTH
    tm = min(PROJ_ROWS, t)
    return pl.pallas_call(
        _ab_in_body,
        out_shape=(jax.ShapeDtypeStruct((t, na), BF16), jax.ShapeDtypeStruct((t, n - na), F32)),
        grid=(t // tm,),
        in_specs=[pl.BlockSpec((tm, d), lambda i: (i, 0)), _const_spec((1, d)), _const_spec(w.shape)],
        out_specs=(pl.BlockSpec((tm, na), lambda i: (i, 0)), pl.BlockSpec((tm, n - na), lambda i: (i, 0))),
        compiler_params=_params("parallel"),
        name="ab_in",
    )(x, g, w)


def _attn_body(*refs, n_kblocks):
    q_ref = refs[0]
    k_refs = refs[1:1 + n_kblocks]
    v_refs = refs[1 + n_kblocks:1 + 2 * n_kblocks]
    trow_ref = refs[1 + 2 * n_kblocks]
    o_ref = refs[2 + 2 * n_kblocks]
    bias_ref = refs[3 + 2 * n_kblocks]
    qb = q_ref.shape[0]
    i = pl.program_id(1)
    n_prev = n_kblocks - 1

    @pl.when(i == 0)
    def _():
        w = bias_ref.shape[2]
        rr = lax.broadcasted_iota(jnp.int32, (qb, w), 0) // CHUNK
        cc = lax.broadcasted_iota(jnp.int32, (qb, w), 1) // CHUNK
        band = jnp.logical_and(cc >= rr, cc <= rr + LEFT_CHUNKS)
        for h in range(A_HEADS):
            t = jnp.broadcast_to(trow_ref[h], (qb, trow_ref.shape[2]))
            toe = pltpu.roll(t, 0, 1, stride=1, stride_axis=0)
            bias_ref[h] = jnp.where(band, toe[:, :w], NEG_INF)

    for nvalid in range(n_prev + 1):
        js = tuple(range(n_prev - nvalid, n_kblocks))
        pl.when(i == nvalid if nvalid < n_prev else i >= n_prev)(
            functools.partial(_attn_heads, q_ref, k_refs, v_refs, bias_ref, o_ref, js))


def _attn_heads(q_ref, k_refs, v_refs, bias_ref, o_ref, js):
    qb = q_ref.shape[0]
    lane = lax.broadcasted_iota(jnp.int32, (1, 2 * A_HEAD_DIM), 1)
    first = lane < A_HEAD_DIM
    scale = A_HEAD_DIM ** -0.5
    for pr in range(A_HEADS // 2):
        ls = slice(2 * A_HEAD_DIM * pr, 2 * A_HEAD_DIM * (pr + 1))
        q_pair = q_ref[:, ls]
        ks = [r[:, ls] for r in k_refs]
        vs = [r[:, ls] for r in v_refs]
        outs = []
        for hh in range(2):
            head = 2 * pr + hh
            sel = first if hh == 0 else jnp.logical_not(first)
            qm = jnp.where(sel, scale, 0.0).astype(BF16)
            qh = q_pair * qm
            s = []
            for j in js:
                sj = lax.dot_general(qh, ks[j], (((1,), (1,)), ((), ())), preferred_element_type=F32)
                s.append(sj + bias_ref[head, :, j * qb:(j + 1) * qb])
            m = s[0].max(axis=-1, keepdims=True)
            for sj in s[1:]:
                m = jnp.maximum(m, sj.max(axis=-1, keepdims=True))
            l = None
            pv = None
            for sj, j in zip(s, js):
                p = jnp.exp(sj - m)
                lj = p.sum(axis=-1, keepdims=True)
                pvj = jnp.dot(p.astype(BF16), vs[j], preferred_element_type=F32)
                l = lj if l is None else l + lj
                pv = pvj if pv is None else pv + pvj
            outs.append(pv / l)
        o_ref[:, ls] = jnp.where(first, outs[0], outs[1]).astype(BF16)


def _attn_bias_row(rel_bias, qb):
    nh = rel_bias.shape[0]
    w = ATT_PREV + qb
    period = w + qb
    near = 2 * REL_CLIP + 1
    assert ATT_PREV >= REL_CLIP and qb >= REL_CLIP and period % V7X_LANES == 0
    far_past = jnp.broadcast_to(rel_bias[:, near - 1:], (nh, ATT_PREV - REL_CLIP))
    far_future = jnp.broadcast_to(rel_bias[:, :1], (nh, qb - REL_CLIP))
    wrapped = jnp.broadcast_to(rel_bias[:, near - 1:], (nh, qb - 1))
    t = jnp.concatenate([far_past, rel_bias[:, ::-1], far_future, wrapped], axis=1).astype(F32)
    assert t.shape[1] == period
    return t.reshape(nh, 1, period)


def _attention(qkv, rel_bias):
    b, s, _ = qkv.shape
    qb = min(ATT_ROWS, s)
    assert ATT_PREV % qb == 0 and s % qb == 0
    n_prev = ATT_PREV // qb
    n_kblocks = n_prev + 1
    trow = _attn_bias_row(rel_bias, qb)

    def blk(col, back):
        return pl.BlockSpec((None, qb, A_WIDTH), lambda bi, i: (bi, jnp.maximum(i - back, 0), col))

    in_specs = ([blk(0, 0)] + [blk(1, n_prev - j) for j in range(n_kblocks)]
                + [blk(2, n_prev - j) for j in range(n_kblocks)] + [_const_spec(trow.shape)])
    return pl.pallas_call(
        functools.partial(_attn_body, n_kblocks=n_kblocks),
        out_shape=jax.ShapeDtypeStruct((b, s, A_WIDTH), BF16),
        grid=(b, s // qb),
        in_specs=in_specs,
        out_specs=pl.BlockSpec((None, qb, A_WIDTH), lambda bi, i: (bi, i, 0)),
        scratch_shapes=[pltpu.VMEM((A_HEADS, qb, ATT_PREV + qb), F32)],
        compiler_params=_params("parallel", "arbitrary"),
        name="attention",
    )(*([qkv] * (1 + 2 * n_kblocks)), trow)


def _split3(x):
    h1 = x.astype(BF16)
    r1 = x - h1.astype(F32)
    h2 = r1.astype(BF16)
    h3 = (r1 - h2.astype(F32)).astype(BF16)
    return h1, h2, h3


def _rwkv_body(*refs, has_vmix, c):
    pb_ref = refs[0]
    state_ref, carry_ref = refs[-2:]

    @pl.when(pl.program_id(0) == 0)
    def _():
        state_ref[...] = jnp.zeros_like(state_ref)
        carry_ref[...] = jnp.zeros_like(carry_ref)

    for ci in range(pb_ref.shape[1] // c):
        _rwkv_chunk(slice(ci * c, (ci + 1) * c), refs, has_vmix)


def _rwkv_chunk(ts, refs, has_vmix):
    if has_vmix:
        (pb_ref, vf_ref, mu_ref, w0_ref, wup_ref, a0_ref, aup_ref, gup_ref, kk_ref, ka_ref, rk_ref,
         lnw_ref, lnb_ref, v0_ref, vdn_ref, vup_ref, o_ref, state_ref, carry_ref) = refs
        vraw_ref = None
    else:
        (pb_ref, mu_ref, w0_ref, wup_ref, a0_ref, aup_ref, gup_ref, kk_ref, ka_ref, rk_ref,
         lnw_ref, lnb_ref, o_ref, vraw_ref, state_ref, carry_ref) = refs
    nb, _, ncol = pb_ref.shape
    c = ts.stop - ts.start
    n = B_HEAD_SIZE
    rows = nb * c
    p = pb_ref[:, ts, :].reshape(rows, ncol)
    row = lax.broadcasted_iota(jnp.int32, (rows, 1), 0)
    prev = pltpu.roll(p, 1, axis=0)
    for b in range(nb):
        prev = jnp.where(row == b * c, carry_ref[b, 0:1, :], prev)
    for b in range(nb):
        carry_ref[b, 0:1, :] = p[(b + 1) * c - 1:(b + 1) * c, :]
    ps = p + (prev - p) * mu_ref[...]
    r = ps[:, 0:B_WIDTH]
    k = ps[:, B_WIDTH:2 * B_WIDTH]
    v = ps[:, 2 * B_WIDTH:3 * B_WIDTH]
    wad = ps[:, 3 * B_WIDTH:3 * B_WIDTH + W_LORA + A_LORA]
    gd = ps[:, 3 * B_WIDTH + W_LORA + A_LORA:]
    if has_vmix:
        mix = _sigmoid(v0_ref[...] + _mm(_mm(v, vdn_ref[...]), vup_ref[...]))
        v = v + (vf_ref[:, ts, :].reshape(rows, B_WIDTH) - v) * mix
    else:
        vraw_ref[:, ts, :] = v.reshape(nb, c, B_WIDTH)
    wlog = -DECAY_SCALE * _sigmoid(w0_ref[...] + _mm(jnp.tanh(wad), wup_ref[...]))
    a = _sigmoid(a0_ref[...] + _mm(wad, aup_ref[...]))
    g = _mm(_sigmoid(gd), gup_ref[...])
    kkf = k * kk_ref[...]
    kka = kkf * a
    kmod = k * (1.0 + (a - 1.0) * ka_ref[...])
    rkr = r * kmod * rk_ref[...]

    ti = lax.broadcasted_iota(jnp.int32, (c, 3 * c), 0)
    tj = lax.broadcasted_iota(jnp.int32, (c, 3 * c), 1) % c
    tri3 = jnp.where(tj <= ti, 1.0, 0.0).astype(BF16)
    lcum, lend = [], []
    for b in range(nb):
        lb = jnp.dot(tri3, jnp.concatenate(_split3(wlog[b * c:(b + 1) * c]), axis=0),
                     preferred_element_type=F32)
        lcum.append(lb)
        lend.append(jnp.broadcast_to(lb[c - 1:c, :], lb.shape))
    lcum = jnp.concatenate(lcum, axis=0)
    lend = jnp.concatenate(lend, axis=0)
    e_neg = jnp.exp(-lcum)
    e_end = jnp.exp(lend - lcum)
    w_end = jnp.exp(lend)

    pair = 2 * n
    npair = B_HEADS // 2
    lo = lax.broadcasted_iota(jnp.int32, (1, 1, pair), 2) < n

    def slabs(q):
        return jnp.stack([q[b * c:(b + 1) * c, p * pair:(p + 1) * pair]
                          for b in range(nb) for p in range(npair)], axis=0)

    def unslabs(q):
        return jnp.concatenate(
            [jnp.concatenate([q[b * npair + p] for p in range(npair)], axis=-1) for b in range(nb)], axis=0)

    def head_sum(q):
        s_lo = jnp.sum(jnp.where(lo, q, 0.0), axis=-1, keepdims=True)
        s_hi = jnp.sum(jnp.where(lo, 0.0, q), axis=-1, keepdims=True)
        return jnp.where(lo, s_lo, s_hi)

    def bd(q):
        qb = q.astype(BF16)
        zero = jnp.zeros_like(qb)
        return jnp.concatenate([jnp.where(lo, qb, zero), jnp.where(lo, zero, qb)], axis=1)

    def bmm(lhs, rhs):
        return jnp.einsum("gij,gjv->giv", lhs.astype(BF16), rhs.astype(BF16), preferred_element_type=F32)

    kk_s = slabs(kkf)
    rs = lax.rsqrt(jnp.maximum(head_sum(kk_s * kk_s), 1e-24))
    x = jnp.concatenate([slabs(-kkf * jnp.exp(lcum - wlog)) * rs, slabs(r * jnp.exp(lcum))], axis=1)
    bt = slabs(kka * e_neg) * rs
    kt = slabs(kmod * e_neg)
    bk_end = jnp.concatenate([slabs(kka * e_end) * rs, slabs(kmod * e_end)], axis=1)
    vh = slabs(v)
    s0 = state_ref[...]
    xb = x.astype(BF16)
    ybd = jnp.concatenate([bd(bt), bd(kt)], axis=1).astype(BF16)
    gm = jnp.einsum("gik,gjk->gij", xb, ybd, preferred_element_type=F32)
    xs = jnp.einsum("gik,gvk->giv", xb, s0.astype(BF16), preferred_element_type=F32)
    ri = lax.broadcasted_iota(jnp.int32, (1, c, 2 * c), 1)
    ci = lax.broadcasted_iota(jnp.int32, (1, c, 2 * c), 2) % c
    strict = ci < ri
    incl = ci <= ri
    m_ab = jnp.where(strict, gm[:, :c, :2 * c], 0.0)
    m_ak = jnp.where(strict, gm[:, :c, 2 * c:], 0.0)
    m_rb = jnp.where(incl, gm[:, c:, :2 * c], 0.0)
    m_rk = jnp.where(incl, gm[:, c:, 2 * c:], 0.0)
    bdv = bd(vh)
    u = xs[:, :c] + bmm(m_ak, bdv)
    pw = m_ab.astype(BF16)
    span = 1
    while True:
        u = u + bmm(pw, bd(u))
        span *= 2
        if span >= c:
            break
        pw = bmm(pw, bd(pw)).astype(BF16)
    o = xs[:, c:] + bmm(jnp.concatenate([m_rb, m_rk], axis=2), jnp.concatenate([bd(u), bdv], axis=1))
    w_end_s = jnp.stack([w_end[b * c:b * c + 1, p * pair:(p + 1) * pair]
                         for b in range(nb) for p in range(npair)], axis=0)
    uv = jnp.concatenate([u, vh], axis=1)
    upd = jnp.einsum("gjv,gjk->gvk", uv.astype(BF16), bk_end.astype(BF16), preferred_element_type=F32)
    same_head = (lax.broadcasted_iota(jnp.int32, (1, pair, pair), 1) < n) == lo
    state_ref[...] = s0 * w_end_s + jnp.where(same_head, upd, 0.0)
    mean = head_sum(o) * (1.0 / n)
    d = o - mean
    var = head_sum(d * d) * (1.0 / n)
    o = d * lax.rsqrt(var + GN_EPS)
    bonus = head_sum(slabs(rkr)) * vh
    out = (unslabs(o) * lnw_ref[...] + lnb_ref[...] + unslabs(bonus)) * g
    o_ref[:, ts, :] = out.astype(BF16).reshape(nb, c, B_WIDTH)


def _rwkv(pb, v_first, prm, vmix):
    b, s, ncol = pb.shape
    c = min(RWKV_CHUNK, s)
    ts = min(RWKV_STEP_CHUNKS * c, s)
    has_vmix = vmix is not None
    tok = lambda w: pl.BlockSpec((b, ts, w), lambda i: (0, i, 0))
    vec = lambda a: a.reshape(1, -1).astype(F32)
    lw = W_LORA + A_LORA
    wup = jnp.zeros((lw, B_WIDTH), F32).at[:W_LORA].set(prm["w_up"]).astype(BF16)
    aup = jnp.zeros((lw, B_WIDTH), F32).at[W_LORA:].set(prm["a_up"]).astype(BF16)
    common = [vec(prm["mu"]), vec(prm["w0"]), wup, vec(prm["a0"]), aup, prm["g_up"].astype(BF16),
              vec(prm["k_k"]), vec(prm["k_a"]), vec(prm["r_k"]), vec(prm["ln_w"]), vec(prm["ln_b"])]
    if has_vmix:
        v0, v_down, v_up = vmix
        vdn = jnp.zeros((B_WIDTH, V7X_LANES), F32).at[:, :V_LORA].set(v_down).astype(BF16)
        vup = jnp.zeros((V7X_LANES, B_WIDTH), F32).at[:V_LORA].set(v_up).astype(BF16)
        args = [pb, v_first] + common + [vec(v0), vdn, vup]
        in_specs = [tok(ncol), tok(B_WIDTH)] + [_const_spec(a.shape) for a in args[2:]]
        out_shape = jax.ShapeDtypeStruct((b, s, B_WIDTH), BF16)
        out_specs = tok(B_WIDTH)
    else:
        args = [pb] + common
        in_specs = [tok(ncol)] + [_const_spec(a.shape) for a in args[1:]]
        out_shape = (jax.ShapeDtypeStruct((b, s, B_WIDTH), BF16), jax.ShapeDtypeStruct((b, s, B_WIDTH), F32))
        out_specs = (tok(B_WIDTH), tok(B_WIDTH))
    return pl.pallas_call(
        functools.partial(_rwkv_body, has_vmix=has_vmix, c=c),
        out_shape=out_shape,
        grid=(s // ts,),
        in_specs=in_specs,
        out_specs=out_specs,
        scratch_shapes=[pltpu.VMEM((b * B_HEADS // 2, 2 * B_HEAD_SIZE, 2 * B_HEAD_SIZE), F32),
                        pltpu.VMEM((b, V7X_SUBLANES, ncol), F32)],
        compiler_params=_params("arbitrary"),
        name="rwkv_mix" if has_vmix else "rwkv",
    )(*args)


def _gelu_tanh(x):
    return 0.5 * x * (1.0 + jnp.tanh(np.sqrt(2.0 / np.pi).astype(np.float32) * (x + 0.044715 * (x * x * x))))


def _rglru_tile(bi, x_ref, g_ref, win_ref, cw_ref, cb_ref, wax_ref, ba_ref, bx_ref, lam_ref, wout_ref,
                o_ref, xprev_ref, hprev_ref):
    tr = x_ref.shape[1]
    sub = V7X_SUBLANES
    nrow = tr // sub
    ntail = CONV_W - 1
    x = x_ref[bi]
    h = _rms(x, g_ref[...]).astype(BF16)
    i0 = lax.broadcasted_iota(jnp.int32, (tr, tr), 0)
    i1 = lax.broadcasted_iota(jnp.int32, (tr, tr), 1)
    perm = jnp.where(i1 == (i0 % sub) * nrow + i0 // sub, 1.0, 0.0).astype(BF16)
    unperm = jnp.where(i0 == (i1 % sub) * nrow + i1 // sub, 1.0, 0.0).astype(BF16)
    hp = jnp.dot(perm, h, preferred_element_type=F32).astype(BF16)
    gate = _gelu_tanh(jnp.dot(hp, win_ref[:, :D_RNN], preferred_element_type=F32))
    xb3 = jnp.dot(hp, win_ref[:, D_RNN:], preferred_element_type=F32).reshape(nrow, sub, D_RNN)
    seg = lax.broadcasted_iota(jnp.int32, (1, sub, 1), 1)
    prev3 = xprev_ref[bi]
    tails = [jnp.where(seg == 0, pltpu.roll(prev3[q:q + 1], 1, axis=1),
                       pltpu.roll(xb3[nrow - ntail + q:nrow - ntail + q + 1], 1, axis=1)) for q in range(ntail)]
    xprev_ref[bi] = xb3[nrow - ntail:]
    xc3 = xb3 * cw_ref[CONV_W - 1:CONV_W, :] + cb_ref[...]
    for sft in range(1, CONV_W):
        shifted = jnp.concatenate(tails[ntail - sft:] + [xb3[:nrow - sft]], axis=0)
        xc3 = xc3 + shifted * cw_ref[CONV_W - 1 - sft:CONV_W - sft, :]
    xc = xc3.reshape(tr, D_RNN)
    xcb = xc.astype(BF16)
    pw = 2 * C_BLOCK_W
    ra, rx = [], []
    for q in range(C_BLOCKS // 2):
        t = jnp.dot(xcb[:, q * pw:(q + 1) * pw], wax_ref[q], preferred_element_type=F32)
        ra.append(t[:, :pw])
        rx.append(t[:, pw:])
    rgate = _sigmoid(jnp.concatenate(ra, axis=-1) + ba_ref[...])
    igate = _sigmoid(jnp.concatenate(rx, axis=-1) + bx_ref[...])
    z = -lam_ref[...]
    softplus = jnp.maximum(z, 0.0) + jnp.log(1.0 + jnp.exp(-jnp.abs(z)))
    a = jnp.exp(-RG_C * rgate * softplus)
    mult = jnp.sqrt(jnp.maximum(1.0 - a * a, 0.0))
    bb = mult * igate * xc
    a3 = a.reshape(nrow, sub, D_RNN)
    b3 = bb.reshape(nrow, sub, D_RNN)
    acs, bcs = [a3[0]], [b3[0]]
    for p in range(1, nrow):
        bcs.append(a3[p] * bcs[-1] + b3[p])
        acs.append(a3[p] * acs[-1])
    ga, gb = acs[-1], bcs[-1]
    srow = lax.broadcasted_iota(jnp.int32, (sub, 1), 0)
    sft = 1
    while sft < sub:
        keep = srow >= sft
        a_sh = jnp.where(keep, pltpu.roll(ga, sft, axis=0), 1.0)
        b_sh = jnp.where(keep, pltpu.roll(gb, sft, axis=0), 0.0)
        gb = ga * b_sh + gb
        ga = ga * a_sh
        sft *= 2
    hprev = hprev_ref[bi, 0:1, :]
    hend = ga * hprev + gb
    hin = jnp.where(srow == 0, hprev, pltpu.roll(hend, 1, axis=0))
    hprev_ref[bi, 0:1, :] = hend[sub - 1:sub, :]
    hs = jnp.stack([acs[p] * hin + bcs[p] for p in range(nrow)], axis=0).reshape(tr, D_RNN)
    yp = (hs * gate).astype(BF16)
    y = jnp.dot(unperm, yp, preferred_element_type=F32).astype(BF16)
    o_ref[bi] = x + jnp.dot(y, wout_ref[...], preferred_element_type=F32)


def _rglru_body(x_ref, *rest):
    xprev_ref, hprev_ref = rest[-2:]

    @pl.when(pl.program_id(1) == 0)
    def _():
        xprev_ref[...] = jnp.zeros_like(xprev_ref)
        hprev_ref[...] = jnp.zeros_like(hprev_ref)

    for bi in range(x_ref.shape[0]):
        _rglru_tile(bi, x_ref, *rest)


def _rglru(x, g, w_in, conv_w, conv_b, wa, ba, wx, bx, lam, w_out):
    b, s, d = x.shape
    tr = min(RGLRU_ROWS, s)
    zero = jnp.zeros((C_BLOCKS // 2, C_BLOCK_W, C_BLOCK_W), wa.dtype)
    even = jnp.concatenate([wa[0::2], zero, wx[0::2], zero], axis=2)
    odd = jnp.concatenate([zero, wa[1::2], zero, wx[1::2]], axis=2)
    wax = jnp.concatenate([even, odd], axis=1).astype(BF16)
    vec = lambda a: a.reshape(1, -1).astype(F32)
    args = [x, vec(g), w_in, conv_w.astype(F32), vec(conv_b), wax, vec(ba), vec(bx), vec(lam), w_out]
    nb = min(RGLRU_BATCH, b)
    tok = pl.BlockSpec((nb, tr, d), lambda bi, i: (bi, i, 0))
    return pl.pallas_call(
        _rglru_body,
        out_shape=jax.ShapeDtypeStruct((b, s, d), F32),
        grid=(b // nb, s // tr),
        in_specs=[tok] + [_const_spec(a.shape) for a in args[1:]],
        out_specs=tok,
        scratch_shapes=[pltpu.VMEM((nb, CONV_W - 1, V7X_SUBLANES, D_RNN), F32),
                        pltpu.VMEM((nb, V7X_SUBLANES, D_RNN), F32)],
        compiler_params=_params("parallel", "arbitrary"),
        name="rglru",
    )(*args)


def kernel(x, norm_ffn, ffn_w_gate, ffn_w_up, ffn_w_down, norm_mix, ab_w_in, ab_w_out, att_rel_bias, rwkv_mu, rwkv_w0, rwkv_w_up, rwkv_a0, rwkv_a_up, rwkv_g_up, rwkv_k_k, rwkv_k_a, rwkv_r_k, rwkv_ln_w, rwkv_ln_b, rwkv_v0, rwkv_v_down, rwkv_v_up, c_w_in, c_conv_w, c_conv_b, c_wa, c_ba, c_wx, c_bx, c_lambda, c_w_out, norm_final):
    b, s, d = x.shape
    t = b * s
    depth = norm_mix.shape[0]
    ffn_w = (ffn_w_gate.astype(F32), ffn_w_up.astype(F32), ffn_w_down.astype(F32))
    ffn_calls = [(l, j) for l in range(depth) for j in range(2)]
    gfin = norm_final.reshape(1, d).astype(F32)
    wb = tuple(w[0, 0].astype(BF16) for w in ffn_w)

    def ffn(x, wb, lead, mixer=None, extra=()):
        idx = ffn_calls.index(lead)
        last = idx + 1 == len(ffn_calls)
        casts = [] if last else [(w, ffn_calls[idx + 1]) for w in ffn_w]
        x, cast = _ffn(x, norm_ffn[lead].reshape(1, d), wb, gfin, last, mixer, casts + list(extra))
        return x, cast[:len(casts)], cast[len(casts):]

    x = x.astype(F32).reshape(t, d)
    v_first = None
    for l in range(depth):
        mixer = None
        if l % 2 == 0:
            i = l // 2
            x, wb, (w_in, w_out) = ffn(x, wb, (l, 0), extra=[(ab_w_in.astype(F32), (i,)), (ab_w_out.astype(F32), (i,))])
            qkv, pb = _ab_in(x, norm_mix[l].reshape(1, d), w_in)
            oa = _attention(qkv.reshape(b, s, -1), att_rel_bias[i])
            prm = dict(mu=rwkv_mu[i], w0=rwkv_w0[i], w_up=rwkv_w_up[i], a0=rwkv_a0[i], a_up=rwkv_a_up[i],
                       g_up=rwkv_g_up[i], k_k=rwkv_k_k[i], k_a=rwkv_k_a[i], r_k=rwkv_r_k[i],
                       ln_w=rwkv_ln_w[i], ln_b=rwkv_ln_b[i])
            if i == 0:
                ob, v_first = _rwkv(pb.reshape(b, s, -1), None, prm, None)
            else:
                ob = _rwkv(pb.reshape(b, s, -1), v_first, prm,
                           (rwkv_v0[i - 1], rwkv_v_down[i - 1], rwkv_v_up[i - 1]))
            mixer = (oa.reshape(t, -1), ob.reshape(t, -1), w_out)
        else:
            j = l // 2
            x, wb, (cw_in, cw_out) = ffn(x, wb, (l, 0), extra=[(c_w_in.astype(F32), (j,)), (c_w_out.astype(F32), (j,))])
            x = _rglru(x.reshape(b, s, d), norm_mix[l], cw_in, c_conv_w[j], c_conv_b[j], c_wa[j], c_ba[j],
                       c_wx[j], c_bx[j], c_lambda[j], cw_out).reshape(t, d)
        x, wb, _ = ffn(x, wb, (l, 1), mixer)
    return x.reshape(b, s, d)
```
